```python
import math
import jax, jax.numpy as jnp
from jax import lax
import numpy as np

D_MODEL = 1024
BATCH = 8
SEQ = 2048
DEPTH = 1
DEC_BATCH = 128
DEC_SEQ = 1
PAST_LEN = 16384
PAGE_SIZE = 128

RW_HEADS = 8
RW_HEAD_DIM = 64
RW_WIDTH = RW_HEADS * RW_HEAD_DIM
LORA_W = 64
LORA_A = 64
LORA_G = 128
SHIFT_WIDTH = 3 * RW_WIDTH + LORA_W + LORA_A + LORA_G
SSM_WIDTH = 512
SSM_GROUP = 16
SSM_GROUPS = SSM_WIDTH // SSM_GROUP
SSM_STATE = 64
IN_WIDTH = SHIFT_WIDTH + SSM_WIDTH + 2 * D_MODEL
D_FF = 2816
CONV_W = 3
PLE_DIM = 256
EPS = 1e-6
GN_EPS = 64e-5

kernel_name = 'rwkv7_s5_gated_hybrid_step'


def rmsnorm(x, g):
    xf = x.astype(jnp.float32)
    y = xf * lax.rsqrt(jnp.mean(xf * xf, axis=-1, keepdims=True) + EPS)
    return (y * g.astype(jnp.float32)).astype(x.dtype)


def wkv_recurrence(r, w, k, v, kk, a, s0):
    def step(S, inp):
        r_t, w_t, k_t, v_t, kk_t, a_t = inp
        sa = jnp.einsum('bhvk,bhk->bhv', S, -kk_t)
        S = (S * w_t[:, :, None, :]
             + sa[..., None] * (kk_t * a_t)[:, :, None, :]
             + v_t[..., None] * k_t[:, :, None, :])
        y_t = jnp.einsum('bhvk,bhk->bhv', S, r_t)
        return S, y_t
    xs = tuple(jnp.swapaxes(t.astype(jnp.float32), 0, 1) for t in (r, w, k, v, kk, a))
    S, ys = lax.scan(step, s0.astype(jnp.float32), xs)
    return jnp.swapaxes(ys, 0, 1), S


def s5_discretise(A_re, A_im, log_dt, B_re, B_im):
    dt = jnp.exp(log_dt.astype(jnp.float32))[:, None]
    lr, li = A_re.astype(jnp.float32), A_im.astype(jnp.float32)
    mag = jnp.exp(lr * dt)
    ar, ai = mag * jnp.cos(li * dt), mag * jnp.sin(li * dt)
    den = lr * lr + li * li
    fr = ((ar - 1.0) * lr + ai * li) / den
    fi = (ai * lr - (ar - 1.0) * li) / den
    Br, Bi = B_re.astype(jnp.float32), B_im.astype(jnp.float32)
    br = fr[..., None] * Br - fi[..., None] * Bi
    bi = fr[..., None] * Bi + fi[..., None] * Br
    return ar, ai, br, bi


def complex_affine_combine(e1, e2):
    a1r, a1i, b1r, b1i = e1
    a2r, a2i, b2r, b2i = e2
    return (a2r * a1r - a2i * a1i,
            a2r * a1i + a2i * a1r,
            a2r * b1r - a2i * b1i + b2r,
            a2r * b1i + a2i * b1r + b2i)


def layer(x, p, st_shift, st_wkv, st_re, st_im, st_conv, L):
    B, T, _ = x.shape
    f32 = jnp.float32
    h = rmsnorm(x, L['ln1_g'])
    proj = h @ L['w_in']
    p_rw = proj[..., :SHIFT_WIDTH]
    u = proj[..., SHIFT_WIDTH:SHIFT_WIDTH + SSM_WIDTH]
    gate_logits = proj[..., SHIFT_WIDTH + SSM_WIDTH:]

    prev = jnp.concatenate([st_shift[:, None].astype(p_rw.dtype), p_rw[:, :-1]], axis=1)
    xs = p_rw + (prev - p_rw) * L['mu_shift']
    new_shift = p_rw[:, -1]
    r, k, v, xw, xa, xg = jnp.split(
        xs, [RW_WIDTH, 2 * RW_WIDTH, 3 * RW_WIDTH, 3 * RW_WIDTH + LORA_W,
             3 * RW_WIDTH + LORA_W + LORA_A], axis=-1)
    wlog = -jax.nn.softplus(-(L['w0'] + jnp.tanh(xw) @ L['w2'])) - 0.5
    decay = jnp.exp(-jnp.exp(wlog.astype(f32)))
    a = jax.nn.sigmoid(L['a0'] + xa @ L['a2'])
    g = jax.nn.sigmoid(xg) @ L['g2']
    heads = lambda t: t.reshape(B, T, RW_HEADS, RW_HEAD_DIM)
    kk = heads(k * L['k_k']).astype(f32)
    kk = kk / jnp.maximum(jnp.sqrt(jnp.sum(kk * kk, axis=-1, keepdims=True)), 1e-12)
    k = k * (1.0 + (a - 1.0) * L['k_a'])
    rh, kh, vh, ah = heads(r), heads(k), heads(v), heads(a)
    y, S_new = wkv_recurrence(rh, heads(decay), kh, vh, kk, ah, st_wkv)
    mu_ = jnp.mean(y, axis=-1, keepdims=True)
    var = jnp.mean(jnp.square(y - mu_), axis=-1, keepdims=True)
    y = ((y - mu_) * lax.rsqrt(var + GN_EPS)).reshape(B, T, RW_WIDTH)
    y = y * L['lnx_g'].astype(f32) + L['lnx_b'].astype(f32)
    bonus = jnp.sum((rh * kh * L['r_k']).astype(f32), axis=-1, keepdims=True) * vh.astype(f32)
    y = ((y + bonus.reshape(B, T, RW_WIDTH)) * g.astype(f32)).astype(x.dtype)
    rw_out = y @ L['w_rw_out']

    ar, ai, br, bi = s5_discretise(L['A_re'], L['A_im'], L['log_dt'], L['B_re'], L['B_im'])
    ug = u.reshape(B, T, SSM_GROUPS, SSM_GROUP).astype(f32)
    bu_r = jnp.einsum('btgc,gnc->btgn', ug, br)
    bu_i = jnp.einsum('btgc,gnc->btgn', ug, bi)
    x0r, x0i = st_re.astype(f32), st_im.astype(f32)
    bu_r = bu_r.at[:, 0].add(ar * x0r - ai * x0i)
    bu_i = bu_i.at[:, 0].add(ar * x0i + ai * x0r)
    Ar = jnp.broadcast_to(ar, bu_r.shape)
    Ai = jnp.broadcast_to(ai, bu_i.shape)
    _, _, sr, si = lax.associative_scan(complex_affine_combine, (Ar, Ai, bu_r, bu_i), axis=1)
    yc = (jnp.einsum('gcn,btgn->btgc', L['C_re'].astype(f32), sr)
          - jnp.einsum('gcn,btgn->btgc', L['C_im'].astype(f32), si))
    ys = yc.reshape(B, T, SSM_WIDTH) + L['D_skip'].astype(f32) * u.astype(f32)
    z = jax.nn.gelu(ys).astype(x.dtype)
    zz = z @ L['w_glu']
    s5_out = zz[..., :D_MODEL] * jax.nn.sigmoid(zz[..., D_MODEL:])

    g_rw = jax.nn.sigmoid(gate_logits[..., :D_MODEL])
    g_s5 = jax.nn.sigmoid(gate_logits[..., D_MODEL:])
    x = x + ((g_rw * rw_out + g_s5 * s5_out) @ L['w_out']).astype(x.dtype)

    h2 = rmsnorm(x, L['ln2_g'])
    ab = h2 @ L['w_ffn_in']
    a_up, b_up = ab[..., :D_FF], ab[..., D_FF:]
    a_ext = jnp.concatenate([st_conv.astype(a_up.dtype), a_up], axis=1)
    cw = L['conv_w']
    a_conv = (cw[0] * a_ext[:, :T] + cw[1] * a_ext[:, 1:T + 1] + cw[2] * a_ext[:, 2:T + 2]
              + L['conv_b'])
    new_conv = a_ext[:, T:]
    x = x + ((jax.nn.gelu(a_conv) * b_up) @ L['w_ffn_out']).astype(x.dtype)

    pg = jax.nn.sigmoid(rmsnorm(x, L['ln3_g']) @ L['w_ple_gate'])
    x = x + (pg * (p @ L['w_ple'])).astype(x.dtype)
    return x, (new_shift, S_new, sr[:, -1], si[:, -1], new_conv)


def setup_inputs(seed: int = 0) -> dict:
    key = jax.random.key(seed)
    ks = iter(jax.random.split(key, 48))
    f32 = jnp.float32
    nrm = lambda shape, s: jax.random.normal(next(ks), shape, f32) * s
    uni = lambda shape, lo, hi: jax.random.uniform(next(ks), shape, f32, lo, hi)
    Ld = DEPTH
    n_idx = jnp.arange(SSM_STATE, dtype=f32)
    return {
        'x_prompt': nrm((BATCH, SEQ, D_MODEL), 1.0),
        'x_sample': nrm((DEC_BATCH, DEC_SEQ, D_MODEL), 1.0),
        'p_prompt': nrm((DEPTH, BATCH, SEQ, PLE_DIM), 1.0),
        'p_sample': nrm((DEPTH, DEC_BATCH, DEC_SEQ, PLE_DIM), 1.0),
        'state_shift': nrm((DEPTH, DEC_BATCH, SHIFT_WIDTH), 1.0),
        'state_wkv': nrm((DEPTH, DEC_BATCH, RW_HEADS, RW_HEAD_DIM, RW_HEAD_DIM), 0.5),
        'state_ssm_re': nrm((DEPTH, DEC_BATCH, SSM_GROUPS, SSM_STATE), 0.5),
        'state_ssm_im': nrm((DEPTH, DEC_BATCH, SSM_GROUPS, SSM_STATE), 0.5),
        'state_conv': nrm((DEPTH, DEC_BATCH, CONV_W - 1, D_FF), 1.0),
        'ln1_g': 1.0 + nrm((Ld, D_MODEL), 0.02),
        'w_in': nrm((Ld, D_MODEL, IN_WIDTH), D_MODEL ** -0.5),
        'mu_shift': uni((Ld, SHIFT_WIDTH), 0.0, 1.0),
        'w0': uni((Ld, RW_WIDTH), -6.0, 1.0),
        'w2': nrm((Ld, LORA_W, RW_WIDTH), 0.1 * LORA_W ** -0.5),
        'a0': nrm((Ld, RW_WIDTH), 0.1),
        'a2': nrm((Ld, LORA_A, RW_WIDTH), 0.1 * LORA_A ** -0.5),
        'g2': nrm((Ld, LORA_G, RW_WIDTH), LORA_G ** -0.5),
        'k_k': 0.85 + nrm((Ld, RW_WIDTH), 0.02),
        'k_a': 1.0 + nrm((Ld, RW_WIDTH), 0.02),
        'r_k': nrm((Ld, RW_HEADS, RW_HEAD_DIM), 0.1),
        'lnx_g': 1.0 + nrm((Ld, RW_WIDTH), 0.02),
        'lnx_b': nrm((Ld, RW_WIDTH), 0.02),
        'w_rw_out': nrm((Ld, RW_WIDTH, D_MODEL), RW_WIDTH ** -0.5),
        'A_re': -0.5 + nrm((Ld, SSM_GROUPS, SSM_STATE), 0.01),
        'A_im': jnp.pi * n_idx + nrm((Ld, SSM_GROUPS, SSM_STATE), 0.01),
        'log_dt': uni((Ld, SSM_GROUPS), math.log(1e-3), math.log(1e-1)),
        'B_re': nrm((Ld, SSM_GROUPS, SSM_STATE, SSM_GROUP), (2 * SSM_GROUP) ** -0.5),
        'B_im': nrm((Ld, SSM_GROUPS, SSM_STATE, SSM_GROUP), (2 * SSM_GROUP) ** -0.5),
        'C_re': nrm((Ld, SSM_GROUPS, SSM_GROUP, SSM_STATE), (2 * SSM_STATE) ** -0.5),
        'C_im': nrm((Ld, SSM_GROUPS, SSM_GROUP, SSM_STATE), (2 * SSM_STATE) ** -0.5),
        'D_skip': nrm((Ld, SSM_WIDTH), 1.0),
        'w_glu': nrm((Ld, SSM_WIDTH, 2 * D_MODEL), SSM_WIDTH ** -0.5),
        'w_out': nrm((Ld, D_MODEL, D_MODEL), D_MODEL ** -0.5),
        'ln2_g': 1.0 + nrm((Ld, D_MODEL), 0.02),
        'w_ffn_in': nrm((Ld, D_MODEL, 2 * D_FF), D_MODEL ** -0.5),
        'conv_w': nrm((Ld, CONV_W, D_FF), CONV_W ** -0.5),
        'conv_b': nrm((Ld, D_FF), 0.02),
        'w_ffn_out': nrm((Ld, D_FF, D_MODEL), D_FF ** -0.5),
        'ln3_g': 1.0 + nrm((Ld, D_MODEL), 0.02),
        'w_ple_gate': nrm((Ld, D_MODEL, D_MODEL), D_MODEL ** -0.5),
        'w_ple': nrm((Ld, PLE_DIM, D_MODEL), PLE_DIM ** -0.5),
        'final_g': 1.0 + nrm((D_MODEL,), 0.02),
    }


def reference(x_prompt, x_sample, p_prompt, p_sample, state_shift, state_wkv, state_ssm_re,
              state_ssm_im, state_conv, ln1_g, w_in, mu_shift, w0, w2, a0, a2, g2, k_k, k_a,
              r_k, lnx_g, lnx_b, w_rw_out, A_re, A_im, log_dt, B_re, B_im, C_re, C_im, D_skip,
              w_glu, w_out, ln2_g, w_ffn_in, conv_w, conv_b, w_ffn_out, ln3_g, w_ple_gate,
              w_ple, final_g):
    xp, xs = x_prompt, x_sample
    Bp = x_prompt.shape[0]
    pst = [[] for _ in range(5)]
    sst = [[] for _ in range(5)]
    for i in range(DEPTH):
        L = dict(ln1_g=ln1_g[i], w_in=w_in[i], mu_shift=mu_shift[i], w0=w0[i], w2=w2[i],
                 a0=a0[i], a2=a2[i], g2=g2[i], k_k=k_k[i], k_a=k_a[i], r_k=r_k[i],
                 lnx_g=lnx_g[i], lnx_b=lnx_b[i], w_rw_out=w_rw_out[i], A_re=A_re[i],
                 A_im=A_im[i], log_dt=log_dt[i], B_re=B_re[i], B_im=B_im[i], C_re=C_re[i],
                 C_im=C_im[i], D_skip=D_skip[i], w_glu=w_glu[i], w_out=w_out[i],
                 ln2_g=ln2_g[i], w_ffn_in=w_ffn_in[i], conv_w=conv_w[i], conv_b=conv_b[i],
                 w_ffn_out=w_ffn_out[i], ln3_g=ln3_g[i], w_ple_gate=w_ple_gate[i],
                 w_ple=w_ple[i])
        z_shift = jnp.zeros((Bp, SHIFT_WIDTH), xp.dtype)
        z_wkv = jnp.zeros((Bp, RW_HEADS, RW_HEAD_DIM, RW_HEAD_DIM), jnp.float32)
        z_ssm = jnp.zeros((Bp, SSM_GROUPS, SSM_STATE), jnp.float32)
        z_conv = jnp.zeros((Bp, CONV_W - 1, D_FF), xp.dtype)
        xp, sp = layer(xp, p_prompt[i], z_shift, z_wkv, z_ssm, z_ssm, z_conv, L)
        xs, ss = layer(xs, p_sample[i], state_shift[i], state_wkv[i], state_ssm_re[i],
                       state_ssm_im[i], state_conv[i], L)
        for j in range(5):
            pst[j].append(sp[j])
            sst[j].append(ss[j])
    y_prompt = rmsnorm(xp, final_g)
    y_sample = rmsnorm(xs, final_g)
    return (y_prompt, y_sample,
            jnp.stack(pst[0]), jnp.stack(pst[1]), jnp.stack(pst[2]), jnp.stack(pst[3]), jnp.stack(pst[4]),
            jnp.stack(sst[0]), jnp.stack(sst[1]), jnp.stack(sst[2]), jnp.stack(sst[3]), jnp.stack(sst[4]))
```

```python
import functools
import math

import jax
import jax.numpy as jnp
from jax import lax
from jax.experimental import pallas as pl
from jax.experimental.pallas import tpu as pltpu

F32 = jnp.float32
BF16 = jnp.bfloat16

D_MODEL = 1024
RW_HEADS = 8
HEAD_DIM = 64
RW_WIDTH = RW_HEADS * HEAD_DIM
LORA_W = 64
LORA_A = 64
LORA_G = 128
SHIFT_WIDTH = 3 * RW_WIDTH + LORA_W + LORA_A + LORA_G
SSM_WIDTH = 512
SSM_GROUP = 16
SSM_GROUPS = SSM_WIDTH // SSM_GROUP
SSM_STATE = 64
GATE_OFF = SHIFT_WIDTH + SSM_WIDTH
IN_WIDTH = SHIFT_WIDTH + SSM_WIDTH + 2 * D_MODEL
D_FF = 2816
PLE_DIM = 256
EPS = 1e-6
GN_EPS = 64e-5

LANES = 128
WKV_CHUNK = 64
S5_CHUNK = 16
N_PAIRS = RW_HEADS // 2
VMEM_LIMIT = 56 * 1024 * 1024
TM_PROJ = 512
TT_RWKV = 256
TM_TAIL = 256
TB_STEP = 16


def _dot(a, b):
    return jnp.dot(a.astype(BF16), b.astype(BF16), preferred_element_type=F32)


def _dot_nt(a, b):
    return lax.dot_general(a.astype(BF16), b.astype(BF16), (((1,), (1,)), ((), ())),
                           preferred_element_type=F32)


def _dot_tn(a, b):
    return lax.dot_general(a.astype(BF16), b.astype(BF16), (((0,), (0,)), ((), ())),
                           preferred_element_type=F32)


def _split3(x):
    h1 = x.astype(BF16)
    r1 = x - h1.astype(F32)
    h2 = r1.astype(BF16)
    h3 = (r1 - h2.astype(F32)).astype(BF16)
    return h1, h2, h3


def _dot_exact_lhs(a_bf16, x):
    h1, h2, h3 = _split3(x)
    d = lambda h: jnp.dot(a_bf16, h, preferred_element_type=F32)
    return d(h1) + d(h2) + d(h3)


def _sigmoid(x):
    return 1.0 / (1.0 + jnp.exp(-x))


def _gelu(x):
    c = math.sqrt(2.0 / math.pi)
    return 0.5 * x * (1.0 + jnp.tanh(c * (x + 0.044715 * (x * x * x))))


def _rmsnorm(x, g):
    return x * lax.rsqrt(jnp.mean(x * x, axis=-1, keepdims=True) + EPS) * g


def _full_spec(shape):
    nd = len(shape)
    return pl.BlockSpec(shape, lambda *_: (0,) * nd)


def _params(sem):
    return pltpu.CompilerParams(dimension_semantics=sem, vmem_limit_bytes=VMEM_LIMIT)


def _proj_kernel(x_ref, g_ref, w_ref, prw_ref, u_ref, gl_ref):
    h = _rmsnorm(x_ref[...], g_ref[...]).astype(BF16)
    prw_ref[...] = jnp.dot(h, w_ref[:, :SHIFT_WIDTH], preferred_element_type=F32)
    u_ref[...] = jnp.dot(h, w_ref[:, SHIFT_WIDTH:GATE_OFF], preferred_element_type=F32)
    gl_ref[...] = jnp.dot(h, w_ref[:, GATE_OFF:], preferred_element_type=F32)


def _proj(x2d, ln1_g, w_in_bf, tm):
    n = x2d.shape[0]
    row = lambda w: pl.BlockSpec((tm, w), lambda i: (i, 0))
    return pl.pallas_call(
        _proj_kernel,
        grid=(n // tm,),
        in_specs=[row(D_MODEL), _full_spec((1, D_MODEL)), _full_spec((D_MODEL, IN_WIDTH))],
        out_specs=[row(SHIFT_WIDTH), row(SSM_WIDTH), row(2 * D_MODEL)],
        out_shape=[jax.ShapeDtypeStruct((n, SHIFT_WIDTH), F32),
                   jax.ShapeDtypeStruct((n, SSM_WIDTH), F32),
                   jax.ShapeDtypeStruct((n, 2 * D_MODEL), F32)],
        compiler_params=_params(("arbitrary",)),
        name="proj",
    )(x2d, ln1_g.reshape(1, D_MODEL), w_in_bf)


def _rwkv_token_math(p, prev, mu, w0, a0, k_k, k_a, r_k, wla, g2, eones):
    xs = p + (prev - p) * mu
    r = xs[:, 0:RW_WIDTH]
    k = xs[:, RW_WIDTH:2 * RW_WIDTH]
    v = xs[:, 2 * RW_WIDTH:3 * RW_WIDTH]
    xwa = xs[:, 3 * RW_WIDTH:3 * RW_WIDTH + LORA_W + LORA_A]
    xg = xs[:, 3 * RW_WIDTH + LORA_W + LORA_A:]
    lane = lax.broadcasted_iota(jnp.int32, xwa.shape, 1)
    la = _dot(jnp.where(lane < LORA_W, jnp.tanh(xwa), xwa), wla)
    wpre = w0 + la[:, :RW_WIDTH]
    softplus = jnp.maximum(-wpre, 0.0) + jnp.log(1.0 + jnp.exp(-jnp.abs(wpre)))
    lw = -jnp.exp(-softplus - 0.5)
    a = _sigmoid(a0 + la[:, RW_WIDTH:])
    g = _dot(_sigmoid(xg), g2)
    kkr = k * k_k
    ss = _dot(kkr * kkr, eones)
    kk = kkr / jnp.maximum(jnp.sqrt(ss), 1e-12)
    kp = k * (1.0 + (a - 1.0) * k_a)
    bonus = _dot(r * kp * r_k, eones) * v
    return r, v, lw, a, g, kk, kp, bonus


def _rwkv_output(y, bonus, g, lnx_g, lnx_b, eones, w_o):
    emean = eones * (1.0 / HEAD_DIM)
    mean = _dot(y, emean)
    d = y - mean
    var = _dot(d * d, emean)
    yn = d * lax.rsqrt(var + GN_EPS) * lnx_g + lnx_b
    return _dot((yn + bonus) * g, w_o)


def _head_masks(shape):
    lane = lax.broadcasted_iota(jnp.int32, shape, 1)
    return lane < HEAD_DIM, lane >= HEAD_DIM


def _stack_heads(x, m_a, m_b):
    return jnp.concatenate([jnp.where(m_a, x, 0.0), jnp.where(m_b, x, 0.0)], axis=0)


def _unit_lower_inverse(n):
    sz = n.shape[0]
    eye = (lax.broadcasted_iota(jnp.int32, (sz, sz), 0)
           == lax.broadcasted_iota(jnp.int32, (sz, sz), 1)).astype(F32)
    t = eye + n
    p = n
    for _ in range(5):
        p = _dot(p, p)
        t = t + _dot(p, t)
    return t


def _rwkv_prompt_kernel(p_ref, mu_ref, w0_ref, a0_ref, kk_ref, ka_ref, rk_ref, lg_ref, lb_ref,
                        wla_ref, g2_ref, e_ref, ltri_ref, wo_ref,
                        out_ref, sout_ref,
                        carry_ref, sp_ref, y_ref, *, tt):
    i = pl.program_id(1)

    @pl.when(i == 0)
    def _():
        carry_ref[...] = jnp.zeros_like(carry_ref)
        sp_ref[...] = jnp.zeros_like(sp_ref)

    p = p_ref[...]
    rows = lax.broadcasted_iota(jnp.int32, (tt, 1), 0)
    prev = jnp.where(rows == 0, carry_ref[...], pltpu.roll(p, 1, axis=0))
    carry_ref[...] = p[tt - 1:tt, :]
    eones = e_ref[...]
    r, v, lw, a, g, kk, kp, bonus = _rwkv_token_math(
        p, prev, mu_ref[...], w0_ref[...], a0_ref[...], kk_ref[...], ka_ref[...], rk_ref[...],
        wla_ref[...], g2_ref[...], eones)

    c = _dot_exact_lhs(ltri_ref[...], lw)
    at = -kk * jnp.exp(c - lw)
    rt = r * jnp.exp(c)
    einv = jnp.exp(-c)
    bvec = kk * a
    bt = bvec * einv
    kt = kp * einv

    m_a, m_b = _head_masks((WKV_CHUNK, LANES))
    ri = lax.broadcasted_iota(jnp.int32, (2 * WKV_CHUNK, 2 * WKV_CHUNK), 0)
    ci = lax.broadcasted_iota(jnp.int32, (2 * WKV_CHUNK, 2 * WKV_CHUNK), 1)
    same_head = (ri >= WKV_CHUNK) == (ci >= WKV_CHUNK)
    strict = same_head & (ri > ci)
    incl = same_head & (ri >= ci)

    for ch in range(tt // WKV_CHUNK):
        sl = slice(ch * WKV_CHUNK, (ch + 1) * WKV_CHUNK)
        c_end = c[(ch + 1) * WKV_CHUNK - 1:(ch + 1) * WKV_CHUNK, :]
        g_end = jnp.exp(c_end)
        e_rest = jnp.exp(c_end - c[sl, :])
        for j in range(N_PAIRS):
            ln = slice(j * LANES, (j + 1) * LANES)
            sp = sp_ref[j]
            ars = _dot_nt(jnp.concatenate([at[sl, ln], rt[sl, ln]], axis=0), sp)
            la = _stack_heads(at[sl, ln], m_a, m_b)
            lr = _stack_heads(rt[sl, ln], m_a, m_b)
            rb = _stack_heads(bt[sl, ln], m_a, m_b)
            rk = _stack_heads(kt[sl, ln], m_a, m_b)
            vs = _stack_heads(v[sl, ln], m_a, m_b)
            gm = _dot_nt(jnp.concatenate([la, lr], axis=0), jnp.concatenate([rb, rk], axis=0))
            n_ab = jnp.where(strict, gm[:LANES, :LANES], 0.0)
            a_ak = jnp.where(strict, gm[:LANES, LANES:], 0.0)
            a_rb = jnp.where(incl, gm[LANES:, :LANES], 0.0)
            a_rk = jnp.where(incl, gm[LANES:, LANES:], 0.0)
            tinv = _unit_lower_inverse(n_ab)
            xs_ = _stack_heads(ars[:WKV_CHUNK], m_a, m_b) + _dot(a_ak, vs)
            us = _dot(tinv, xs_)
            uv = jnp.concatenate([us, vs], axis=0)
            ys = _dot(jnp.concatenate([a_rb, a_rk], axis=1), uv)
            y_ref[sl, ln] = ys[:WKV_CHUNK] + ys[WKV_CHUNK:] + ars[WKV_CHUNK:]
            bk = jnp.concatenate([_stack_heads(bvec[sl, ln] * e_rest[:, ln], m_a, m_b),
                                  _stack_heads(kp[sl, ln] * e_rest[:, ln], m_a, m_b)], axis=0)
            sp_ref[j] = sp * g_end[:, ln] + _dot_tn(uv, bk)

    out_ref[...] = _rwkv_output(y_ref[...], bonus, g, lg_ref[...], lb_ref[...], eones, wo_ref[...])

    @pl.when(i == pl.num_programs(1) - 1)
    def _():
        for j in range(N_PAIRS):
            sp = sp_ref[j]
            sout_ref[0, 2 * j] = sp[:HEAD_DIM, :HEAD_DIM]
            sout_ref[0, 2 * j + 1] = sp[HEAD_DIM:, HEAD_DIM:]


def _rwkv_consts():
    idx = jnp.arange(RW_WIDTH)
    eones = (idx[:, None] // HEAD_DIM == idx[None, :] // HEAD_DIM).astype(BF16)
    return eones


def _rwkv_weights(w2, a2, g2, w_rw_out):
    wla = jnp.zeros((LORA_W + LORA_A, 2 * RW_WIDTH), F32)
    wla = wla.at[:LORA_W, :RW_WIDTH].set(w2).at[LORA_W:, RW_WIDTH:].set(a2)
    return wla.astype(BF16), g2.astype(BF16), w_rw_out.astype(BF16)


def _row_params(mu_shift, w0, a0, k_k, k_a, r_k, lnx_g, lnx_b):
    return (mu_shift.reshape(1, SHIFT_WIDTH), w0.reshape(1, RW_WIDTH), a0.reshape(1, RW_WIDTH),
            k_k.reshape(1, RW_WIDTH), k_a.reshape(1, RW_WIDTH), r_k.reshape(1, RW_WIDTH),
            lnx_g.reshape(1, RW_WIDTH), lnx_b.reshape(1, RW_WIDTH))


def _rwkv_prompt(prw, rowp, wla, g2b, eones, wo, nb, t, tt):
    nt = t // tt
    idx = jnp.arange(tt)
    ltri = ((idx[:, None] // WKV_CHUNK == idx[None, :] // WKV_CHUNK)
            & (idx[:, None] >= idx[None, :])).astype(BF16)
    consts = list(rowp) + [wla, g2b, eones, ltri, wo]
    return pl.pallas_call(
        functools.partial(_rwkv_prompt_kernel, tt=tt),
        grid=(nb, nt),
        in_specs=[pl.BlockSpec((tt, SHIFT_WIDTH), lambda b, i: (b * nt + i, 0))]
        + [_full_spec(c.shape) for c in consts],
        out_specs=[pl.BlockSpec((tt, D_MODEL), lambda b, i: (b * nt + i, 0)),
                   pl.BlockSpec((1, RW_HEADS, HEAD_DIM, HEAD_DIM), lambda b, i: (b, 0, 0, 0))],
        out_shape=[jax.ShapeDtypeStruct((nb * t, D_MODEL), F32),
                   jax.ShapeDtypeStruct((nb, RW_HEADS, HEAD_DIM, HEAD_DIM), F32)],
        scratch_shapes=[pltpu.VMEM((1, SHIFT_WIDTH), F32),
                        pltpu.VMEM((N_PAIRS, LANES, LANES), F32),
                        pltpu.VMEM((tt, RW_WIDTH), F32)],
        compiler_params=_params(("arbitrary", "arbitrary")),
        name="rwkv_prompt",
    )(prw, *consts)


def _rwkv_step_kernel(p_ref, prev_ref, s_ref, mu_ref, w0_ref, a0_ref, kk_ref, ka_ref, rk_ref,
                      lg_ref, lb_ref, wla_ref, g2_ref, e_ref, wo_ref,
                      out_ref, sout_ref,
                      q_ref, wr_ref, bv_ref, kp_ref, v_ref, w_ref, sa_ref, y1_ref, *, tb):
    eones = e_ref[...]
    r, v, lw, a, g, kk, kp, bonus = _rwkv_token_math(
        p_ref[...], prev_ref[...], mu_ref[...], w0_ref[...], a0_ref[...], kk_ref[...],
        ka_ref[...], rk_ref[...], wla_ref[...], g2_ref[...], eones)
    w = jnp.exp(lw)
    bvec = kk * a
    q_ref[...] = -kk
    wr_ref[...] = w * r
    bv_ref[...] = bvec
    kp_ref[...] = kp
    v_ref[...] = v
    w_ref[...] = w

    ri = lax.broadcasted_iota(jnp.int32, (HEAD_DIM, HEAD_DIM), 0)
    ci = lax.broadcasted_iota(jnp.int32, (HEAD_DIM, HEAD_DIM), 1)
    eye = ri == ci
    zpad = jnp.zeros((6, HEAD_DIM), F32)

    def body(b, carry):
        q_row, wr_row, bv_row, kp_row, v_row, w_row = (
            ref[pl.ds(b, 1), :] for ref in (q_ref, wr_ref, bv_ref, kp_ref, v_ref, w_ref))
        sa_parts, y1_parts = [], []
        for h in range(RW_HEADS):
            hs = slice(h * HEAD_DIM, (h + 1) * HEAD_DIM)
            s = s_ref[b, h]
            x2 = jnp.concatenate([q_row[:, hs], wr_row[:, hs], zpad], axis=0)
            r2 = _dot_nt(x2, s)
            sa = r2[0:1]
            sa_parts.append(sa)
            y1_parts.append(r2[1:2])
            dsa = jnp.where(eye, jnp.broadcast_to(sa, (HEAD_DIM, HEAD_DIM)), 0.0)
            dv = jnp.where(eye, jnp.broadcast_to(v_row[:, hs], (HEAD_DIM, HEAD_DIM)), 0.0)
            bb = jnp.broadcast_to(bv_row[:, hs], (HEAD_DIM, HEAD_DIM))
            kb = jnp.broadcast_to(kp_row[:, hs], (HEAD_DIM, HEAD_DIM))
            lhs = jnp.concatenate([dsa, dv], axis=1)
            l1, l2, l3 = _split3(lhs)
            r1, r2_, r3 = _split3(jnp.concatenate([bb, kb], axis=0))
            d = lambda x, y: jnp.dot(x, y, preferred_element_type=F32)
            inc = (d(l1, r1) + d(l1, r2_) + d(l2, r1)) + (d(l1, r3) + d(l2, r2_) + d(l3, r1))
            sout_ref[b, h] = s * w_row[:, hs] + inc
        sa_ref[pl.ds(b, 1), :] = jnp.concatenate(sa_parts, axis=1)
        y1_ref[pl.ds(b, 1), :] = jnp.concatenate(y1_parts, axis=1)
        return carry

    lax.fori_loop(0, tb, body, 0)

    sa = sa_ref[...]
    y = y1_ref[...] + sa * _dot(bvec * r, eones) + v * _dot(kp * r, eones)
    out_ref[...] = _rwkv_output(y, bonus, g, lg_ref[...], lb_ref[...], eones, wo_ref[...])


def _rwkv_step(prw, prev, state, rowp, wla, g2b, eones, wo, tb):
    n = prw.shape[0]
    consts = list(rowp) + [wla, g2b, eones, wo]
    rowspec = lambda w: pl.BlockSpec((tb, w), lambda i: (i, 0))
    sspec = pl.BlockSpec((tb, RW_HEADS, HEAD_DIM, HEAD_DIM), lambda i: (i, 0, 0, 0))
    return pl.pallas_call(
        functools.partial(_rwkv_step_kernel, tb=tb),
        grid=(n // tb,),
        in_specs=[rowspec(SHIFT_WIDTH), rowspec(SHIFT_WIDTH), sspec]
        + [_full_spec(c.shape) for c in consts],
        out_specs=[rowspec(D_MODEL), sspec],
        out_shape=[jax.ShapeDtypeStruct((n, D_MODEL), F32),
                   jax.ShapeDtypeStruct((n, RW_HEADS, HEAD_DIM, HEAD_DIM), F32)],
        scratch_shapes=[pltpu.VMEM((tb, RW_WIDTH), F32) for _ in range(8)],
        compiler_params=_params(("arbitrary",)),
        name="rwkv_step",
    )(prw, prev, state, *consts)


def _s5_tables(A_re, A_im, log_dt, B_re, B_im, C_re, C_im):
    dt = jnp.exp(log_dt)[:, None]
    lr, li = A_re, A_im
    mag = jnp.exp(lr * dt)
    ar, ai = mag * jnp.cos(li * dt), mag * jnp.sin(li * dt)
    den = lr * lr + li * li
    fr = ((ar - 1.0) * lr + ai * li) / den
    fi = (ai * lr - (ar - 1.0) * li) / den
    br = fr[..., None] * B_re - fi[..., None] * B_im
    bi = fr[..., None] * B_im + fi[..., None] * B_re
    lags = jnp.arange(S5_CHUNK + 1, dtype=F32)[:, None, None]
    pm = jnp.exp(lr * dt * lags)
    er, ei = pm * jnp.cos(li * dt * lags), pm * jnp.sin(li * dt * lags)
    return ar, ai, br, bi, er, ei


def _s5_prompt_kernel(u_ref, m_ref, pin_ref, cre_ref, cim_ref, a16_ref,
                      y_ref, xr_ref, xi_ref,
                      bre_ref, bim_ref, xpr_ref, xpi_ref, *, nb, nch):
    ub = u_ref[0].astype(BF16)
    bc = jnp.dot(ub, pin_ref[0], preferred_element_type=F32)
    bre_ref[...] = bc[:, :SSM_STATE]
    bim_ref[...] = bc[:, SSM_STATE:]
    ar = a16_ref[0, 0:1, :]
    ai = a16_ref[0, 1:2, :]

    def step(cidx, carry):
        xr, xi = carry
        o = pl.multiple_of(cidx * nb, nb)
        xpr_ref[pl.ds(o, nb), :] = xr
        xpi_ref[pl.ds(o, nb), :] = xi
        nr = ar * xr - ai * xi + bre_ref[pl.ds(o, nb), :]
        ni = ar * xi + ai * xr + bim_ref[pl.ds(o, nb), :]
        return nr, ni

    z = jnp.zeros((nb, SSM_STATE), F32)
    xr, xi = lax.fori_loop(0, nch, step, (z, z))
    xr_ref[0] = xr
    xi_ref[0] = xi
    y_ref[0] = (jnp.dot(ub, m_ref[0], preferred_element_type=F32)
                + _dot(xpr_ref[...], cre_ref[0]) + _dot(xpi_ref[...], cim_ref[0]))


def _s5_prompt(u, tabs, C_re, C_im, nb, t):
    ar, ai, br, bi, er, ei = tabs
    L, G, N, C = S5_CHUNK, SSM_GROUPS, SSM_STATE, SSM_GROUP
    nch = t // L
    nc = nch * nb
    ce_r = C_re[None] * er[:, :, None, :] - C_im[None] * ei[:, :, None, :]
    ce_i = C_re[None] * ei[:, :, None, :] + C_im[None] * er[:, :, None, :]
    kl = (jnp.einsum('lgcn,gnd->lgcd', ce_r[:L], br, precision=lax.Precision.HIGHEST)
          - jnp.einsum('lgcn,gnd->lgcd', ce_i[:L], bi, precision=lax.Precision.HIGHEST))
    jj = jnp.arange(L)
    lag = jj[None, :] - jj[:, None]
    m = kl[jnp.clip(lag, 0, L - 1)]
    m = jnp.where((lag >= 0)[:, :, None, None, None], m, 0.0)
    m = m.transpose(2, 0, 4, 1, 3).reshape(G, L * C, L * C)
    erj, eij = er[L - 1 - jj], ei[L - 1 - jj]
    pr = erj[..., None] * br[None] - eij[..., None] * bi[None]
    pi = erj[..., None] * bi[None] + eij[..., None] * br[None]
    pin = jnp.concatenate([pr, pi], axis=2).transpose(1, 0, 3, 2).reshape(G, L * C, 2 * N)
    cre = ce_r[1:].transpose(1, 3, 0, 2).reshape(G, N, L * C)
    cim = (-ce_i[1:]).transpose(1, 3, 0, 2).reshape(G, N, L * C)
    a16 = jnp.stack([er[L], ei[L]], axis=1)
    u2 = u.reshape(nb, nch, L, G, C).transpose(3, 1, 0, 2, 4).reshape(G, nc, L * C)
    gspec = lambda shape: pl.BlockSpec((1,) + shape, lambda g: (g, 0, 0))
    y2, xr, xi = pl.pallas_call(
        functools.partial(_s5_prompt_kernel, nb=nb, nch=nch),
        grid=(G,),
        in_specs=[gspec((nc, L * C)), gspec((L * C, L * C)), gspec((L * C, 2 * N)),
                  gspec((N, L * C)), gspec((N, L * C)), gspec((2, N))],
        out_specs=[gspec((nc, L * C)), gspec((nb, N)), gspec((nb, N))],
        out_shape=[jax.ShapeDtypeStruct((G, nc, L * C), F32),
                   jax.ShapeDtypeStruct((G, nb, N), F32),
                   jax.ShapeDtypeStruct((G, nb, N), F32)],
        scratch_shapes=[pltpu.VMEM((nc, N), F32) for _ in range(4)],
        compiler_params=_params(("arbitrary",)),
        name="s5_prompt",
    )(u2, m.astype(BF16), pin.astype(BF16), cre.astype(BF16), cim.astype(BF16), a16)
    ys = y2.reshape(G, nch, nb, L, C).transpose(2, 1, 3, 0, 4).reshape(nb * t, SSM_WIDTH)
    return ys, xr.transpose(1, 0, 2), xi.transpose(1, 0, 2)


def _s5_step_kernel(u_ref, xr_ref, xi_ref, ar_ref, ai_ref, bdr_ref, bdi_ref, cdr_ref, cdi_ref,
                    y_ref, nr_ref, ni_ref):
    ub = u_ref[...].astype(BF16)
    xr, xi, ar, ai = xr_ref[...], xi_ref[...], ar_ref[...], ai_ref[...]
    nr = ar * xr - ai * xi + jnp.dot(ub, bdr_ref[...], preferred_element_type=F32)
    ni = ar * xi + ai * xr + jnp.dot(ub, bdi_ref[...], preferred_element_type=F32)
    nr_ref[...] = nr
    ni_ref[...] = ni
    y_ref[...] = _dot(nr, cdr_ref[...]) - _dot(ni, cdi_ref[...])


def _s5_step(u, st_re, st_im, tabs, C_re, C_im):
    ar, ai, br, bi, _, _ = tabs
    G, N, C = SSM_GROUPS, SSM_STATE, SSM_GROUP
    n = u.shape[0]
    eye = jnp.eye(G, dtype=F32)
    bd = lambda b: (eye[:, None, :, None] * b.transpose(0, 2, 1)[:, :, None, :]).reshape(G * C, G * N)
    cd = lambda c: (eye[:, None, :, None] * c.transpose(0, 2, 1)[:, :, None, :]).reshape(G * N, G * C)
    ops = (u, st_re.reshape(n, G * N), st_im.reshape(n, G * N),
           ar.reshape(1, G * N), ai.reshape(1, G * N),
           bd(br).astype(BF16), bd(bi).astype(BF16), cd(C_re).astype(BF16), cd(C_im).astype(BF16))
    y, nr, ni = pl.pallas_call(
        _s5_step_kernel,
        grid=(1,),
        in_specs=[_full_spec(o.shape) for o in ops],
        out_specs=[_full_spec((n, SSM_WIDTH)), _full_spec((n, G * N)), _full_spec((n, G * N))],
        out_shape=[jax.ShapeDtypeStruct((n, SSM_WIDTH), F32),
                   jax.ShapeDtypeStruct((n, G * N), F32),
                   jax.ShapeDtypeStruct((n, G * N), F32)],
        compiler_params=_params(("arbitrary",)),
        name="s5_step",
    )(*ops)
    return y, nr.reshape(n, G, N), ni.reshape(n, G, N)


def _tail_kernel(*refs, tm, tiles_per_seq, decode, final):
    if decode:
        (x_ref, rw_ref, ys_ref, u_ref, gl_ref, pe_ref, st2_ref, st1_ref,
         dsk_ref, wglu_ref, wout_ref, ln2_ref, wfi_ref, cw_ref, cb_ref, wfo_ref,
         ln3_ref, wpg_ref, wple_ref, fin_ref, y_ref, conv_ref) = refs
    else:
        (x_ref, rw_ref, ys_ref, u_ref, gl_ref, pe_ref,
         dsk_ref, wglu_ref, wout_ref, ln2_ref, wfi_ref, cw_ref, cb_ref, wfo_ref,
         ln3_ref, wpg_ref, wple_ref, fin_ref, y_ref, conv_ref, carry_ref) = refs

    x = x_ref[...]
    z = _gelu(ys_ref[...] + dsk_ref[...] * u_ref[...])
    zz = _dot(z, wglu_ref[...])
    s5 = zz[:, :D_MODEL] * _sigmoid(zz[:, D_MODEL:])
    gl = gl_ref[...]
    merged = _sigmoid(gl[:, :D_MODEL]) * rw_ref[...] + _sigmoid(gl[:, D_MODEL:]) * s5
    x = x + _dot(merged, wout_ref[...])

    h2 = _rmsnorm(x, ln2_ref[...]).astype(BF16)
    a_up = jnp.dot(h2, wfi_ref[:, :D_FF], preferred_element_type=F32)
    b_up = jnp.dot(h2, wfi_ref[:, D_FF:], preferred_element_type=F32)
    cw = cw_ref[...]
    if decode:
        prev2 = st2_ref[...]
        prev1 = st1_ref[...]
        conv_ref[...] = a_up
    else:
        i = pl.program_id(0)

        @pl.when(i % tiles_per_seq == 0)
        def _():
            carry_ref[...] = jnp.zeros_like(carry_ref)

        rows = lax.broadcasted_iota(jnp.int32, (tm, 1), 0)
        c0 = carry_ref[0:1, :]
        c1 = carry_ref[1:2, :]
        prev1 = jnp.where(rows == 0, c1, pltpu.roll(a_up, 1, axis=0))
        prev2 = jnp.where(rows == 0, c0, jnp.where(rows == 1, c1, pltpu.roll(a_up, 2, axis=0)))
        carry_ref[...] = a_up[tm - 2:tm, :]
        conv_ref[0] = a_up[tm - 2:tm, :]
    a_conv = cw[0:1] * prev2 + cw[1:2] * prev1 + cw[2:3] * a_up + cb_ref[...]
    x = x + _dot(_gelu(a_conv) * b_up, wfo_ref[...])

    pg = _sigmoid(_dot(_rmsnorm(x, ln3_ref[...]), wpg_ref[...]))
    x = x + pg * _dot(pe_ref[...], wple_ref[...])
    y_ref[...] = _rmsnorm(x, fin_ref[...]) if final else x


def _tail(x2d, rw, ys, u, gl, pe, st_conv, D_skip, wglu, wout, ln2_g, wfi, conv_w, conv_b, wfo,
          ln3_g, wpg, wple, final_g, *, tm, tiles_per_seq, decode, final):
    n = x2d.shape[0]
    row = lambda w: pl.BlockSpec((tm, w), lambda i: (i, 0))
    consts = [D_skip.reshape(1, SSM_WIDTH), wglu, wout, ln2_g.reshape(1, D_MODEL), wfi,
              conv_w, conv_b.reshape(1, D_FF), wfo, ln3_g.reshape(1, D_MODEL), wpg, wple,
              final_g.reshape(1, D_MODEL)]
    acts = [x2d, rw, ys, u, gl, pe]
    in_specs = [row(a.shape[1]) for a in acts]
    scratch = []
    if decode:
        acts += [st_conv[:, 0], st_conv[:, 1]]
        in_specs += [row(D_FF), row(D_FF)]
        conv_spec = row(D_FF)
        conv_shape = (n, D_FF)
    else:
        nseq = n // (tm * tiles_per_seq)
        conv_spec = pl.BlockSpec((1, 2, D_FF), lambda i: (i // tiles_per_seq, 0, 0))
        conv_shape = (nseq, 2, D_FF)
        scratch = [pltpu.VMEM((2, D_FF), F32)]
    in_specs += [pl.BlockSpec(c.shape, lambda i, nd=c.ndim: (0,) * nd,
                              pipeline_mode=pl.Buffered(1)) for c in consts]
    y, conv = pl.pallas_call(
        functools.partial(_tail_kernel, tm=tm, tiles_per_seq=tiles_per_seq, decode=decode,
                          final=final),
        grid=(n // tm,),
        in_specs=in_specs,
        out_specs=[row(D_MODEL), conv_spec],
        out_shape=[jax.ShapeDtypeStruct((n, D_MODEL), F32),
                   jax.ShapeDtypeStruct(conv_shape, F32)],
        scratch_shapes=scratch,
        compiler_params=_params(("arbitrary",)),
        name="tail_decode" if decode else "tail_prompt",
    )(*acts, *consts)
    if decode:
        conv = jnp.stack([st_conv[:, 1], conv], axis=1)
    return y, conv


def _layer(x, pe, states, W, *, decode, final, tm_proj, tt, tm_tail, tb):
    nb, t, _ = x.shape
    n = nb * t
    x2d = x.reshape(n, D_MODEL)
    prw, u, gl = _proj(x2d, W['ln1_g'], W['w_in'], tm_proj)
    if decode:
        st_shift, st_wkv, st_re, st_im, st_conv = states
        rw, s_new = _rwkv_step(prw, st_shift, st_wkv, W['rowp'], W['wla'], W['g2'], W['eones'],
                               W['w_rw_out'], tb)
        ys, xr, xi = _s5_step(u, st_re, st_im, W['s5'], W['C_re'], W['C_im'])
        new_shift = prw
    else:
        st_conv = None
        rw, s_new = _rwkv_prompt(prw, W['rowp'], W['wla'], W['g2'], W['eones'], W['w_rw_out'],
                                 nb, t, tt)
        ys, xr, xi = _s5_prompt(u, W['s5'], W['C_re'], W['C_im'], nb, t)
        new_shift = prw.reshape(nb, t, SHIFT_WIDTH)[:, -1]
    y, new_conv = _tail(x2d, rw, ys, u, gl, pe.reshape(n, PLE_DIM), st_conv, W['D_skip'],
                        W['w_glu'], W['w_out'], W['ln2_g'], W['w_ffn_in'], W['conv_w'],
                        W['conv_b'], W['w_ffn_out'], W['ln3_g'], W['w_ple_gate'], W['w_ple'],
                        W['final_g'], tm=tm_tail, tiles_per_seq=max(t // tm_tail, 1),
                        decode=decode, final=final)
    return y.reshape(nb, t, D_MODEL), (new_shift, s_new, xr, xi, new_conv)


def kernel(x_prompt, x_sample, p_prompt, p_sample, state_shift, state_wkv, state_ssm_re, state_ssm_im, state_conv, ln1_g, w_in, mu_shift, w0, w2, a0, a2, g2, k_k, k_a, r_k, lnx_g, lnx_b, w_rw_out, A_re, A_im, log_dt, B_re, B_im, C_re, C_im, D_skip, w_glu, w_out, ln2_g, w_ffn_in, conv_w, conv_b, w_ffn_out, ln3_g, w_ple_gate, w_ple, final_g):
    depth = w_in.shape[0]
    xp, xs = x_prompt, x_sample
    pst = [[] for _ in range(5)]
    sst = [[] for _ in range(5)]
    for i in range(depth):
        wla, g2b, wo = _rwkv_weights(w2[i], a2[i], g2[i], w_rw_out[i])
        W = dict(
            ln1_g=ln1_g[i], w_in=w_in[i].astype(BF16),
            rowp=_row_params(mu_shift[i], w0[i], a0[i], k_k[i], k_a[i], r_k[i].reshape(-1),
                             lnx_g[i], lnx_b[i]),
            wla=wla, g2=g2b, eones=_rwkv_consts(), w_rw_out=wo,
            s5=_s5_tables(A_re[i], A_im[i], log_dt[i], B_re[i], B_im[i], C_re[i], C_im[i]),
            C_re=C_re[i], C_im=C_im[i], D_skip=D_skip[i],
            w_glu=w_glu[i].astype(BF16), w_out=w_out[i].astype(BF16), ln2_g=ln2_g[i],
            w_ffn_in=w_ffn_in[i].astype(BF16), conv_w=conv_w[i], conv_b=conv_b[i],
            w_ffn_out=w_ffn_out[i].astype(BF16), ln3_g=ln3_g[i],
            w_ple_gate=w_ple_gate[i].astype(BF16), w_ple=w_ple[i].astype(BF16), final_g=final_g)
        final = i == depth - 1
        tp = xp.shape[1]
        xp, sp = _layer(xp, p_prompt[i], None, W, decode=False, final=final,
                        tm_proj=min(TM_PROJ, tp), tt=min(TT_RWKV, tp), tm_tail=min(TM_TAIL, tp),
                        tb=None)
        ns = xs.shape[0]
        xs, ss = _layer(xs, p_sample[i],
                        (state_shift[i], state_wkv[i], state_ssm_re[i], state_ssm_im[i],
                         state_conv[i]),
                        W, decode=True, final=final, tm_proj=ns, tt=None, tm_tail=ns,
                        tb=min(TB_STEP, ns))
        for j in range(5):
            pst[j].append(sp[j])
            sst[j].append(ss[j])
    return (xp, xs,
            jnp.stack(pst[0]), jnp.stack(pst[1]), jnp.stack(pst[2]), jnp.stack(pst[3]),
            jnp.stack(pst[4]),
            jnp.stack(sst[0]), jnp.stack(sst[1]), jnp.stack(sst[2]), jnp.stack(sst[3]),
            jnp.stack(sst[4]))
```

```python
import functools
import math

import jax
import jax.numpy as jnp
from jax import lax
from jax.experimental import pallas as pl
from jax.experimental.pallas import tpu as pltpu

F32 = jnp.float32
BF16 = jnp.bfloat16

D_MODEL = 1024
RW_HEADS = 8
HEAD_DIM = 64
RW_WIDTH = RW_HEADS * HEAD_DIM
LORA_W = 64
LORA_A = 64
LORA_G = 128
SHIFT_WIDTH = 3 * RW_WIDTH + LORA_W + LORA_A + LORA_G
SSM_WIDTH = 512
SSM_GROUP = 16
SSM_GROUPS = SSM_WIDTH // SSM_GROUP
SSM_STATE = 64
GATE_OFF = SHIFT_WIDTH + SSM_WIDTH
IN_WIDTH = SHIFT_WIDTH + SSM_WIDTH + 2 * D_MODEL
D_FF = 2816
PLE_DIM = 256
EPS = 1e-6
GN_EPS = 64e-5

LANES = 128
WKV_CHUNK = 64
S5_CHUNK = 16
N_PAIRS = RW_HEADS // 2
VMEM_LIMIT = 56 * 1024 * 1024
TM_PROJ = 512
TT_RWKV = 256
TM_TAIL = 256
TB_STEP = 16


def _dot(a, b):
    return jnp.dot(a.astype(BF16), b.astype(BF16), preferred_element_type=F32)


def _dot_nt(a, b):
    return lax.dot_general(a.astype(BF16), b.astype(BF16), (((1,), (1,)), ((), ())),
                           preferred_element_type=F32)


def _dot_tn(a, b):
    return lax.dot_general(a.astype(BF16), b.astype(BF16), (((0,), (0,)), ((), ())),
                           preferred_element_type=F32)


def _split3(x):
    h1 = x.astype(BF16)
    r1 = x - h1.astype(F32)
    h2 = r1.astype(BF16)
    h3 = (r1 - h2.astype(F32)).astype(BF16)
    return h1, h2, h3


def _dot_exact_lhs(a_bf16, x):
    h1, h2, h3 = _split3(x)
    d = lambda h: jnp.dot(a_bf16, h, preferred_element_type=F32)
    return d(h1) + d(h2) + d(h3)


def _sigmoid(x):
    return 1.0 / (1.0 + jnp.exp(-x))


def _gelu(x):
    c = math.sqrt(2.0 / math.pi)
    return 0.5 * x * (1.0 + jnp.tanh(c * (x + 0.044715 * (x * x * x))))


def _rmsnorm(x, g):
    return x * lax.rsqrt(jnp.mean(x * x, axis=-1, keepdims=True) + EPS) * g


def _full_spec(shape):
    nd = len(shape)
    return pl.BlockSpec(shape, lambda *_: (0,) * nd)


def _params(sem):
    return pltpu.CompilerParams(dimension_semantics=sem, vmem_limit_bytes=VMEM_LIMIT)


def _proj_kernel(x_ref, g_ref, w_ref, prw_ref, u_ref, gl_ref):
    h = _rmsnorm(x_ref[...], g_ref[...]).astype(BF16)
    prw_ref[...] = jnp.dot(h, w_ref[:, :SHIFT_WIDTH], preferred_element_type=F32)
    u_ref[...] = jnp.dot(h, w_ref[:, SHIFT_WIDTH:GATE_OFF], preferred_element_type=F32)
    gl_ref[...] = jnp.dot(h, w_ref[:, GATE_OFF:], preferred_element_type=F32)


def _proj(x2d, ln1_g, w_in_bf, tm):
    n = x2d.shape[0]
    row = lambda w: pl.BlockSpec((tm, w), lambda i: (i, 0))
    return pl.pallas_call(
        _proj_kernel,
        grid=(n // tm,),
        in_specs=[row(D_MODEL), _full_spec((1, D_MODEL)), _full_spec((D_MODEL, IN_WIDTH))],
        out_specs=[row(SHIFT_WIDTH), row(SSM_WIDTH), row(2 * D_MODEL)],
        out_shape=[jax.ShapeDtypeStruct((n, SHIFT_WIDTH), F32),
                   jax.ShapeDtypeStruct((n, SSM_WIDTH), F32),
                   jax.ShapeDtypeStruct((n, 2 * D_MODEL), F32)],
        compiler_params=_params(("arbitrary",)),
        name="proj",
    )(x2d, ln1_g.reshape(1, D_MODEL), w_in_bf)


def _rwkv_token_math(p, prev, mu, w0, a0, k_k, k_a, r_k, wla, g2, eones):
    xs = p + (prev - p) * mu
    r = xs[:, 0:RW_WIDTH]
    k = xs[:, RW_WIDTH:2 * RW_WIDTH]
    v = xs[:, 2 * RW_WIDTH:3 * RW_WIDTH]
    xwa = xs[:, 3 * RW_WIDTH:3 * RW_WIDTH + LORA_W + LORA_A]
    xg = xs[:, 3 * RW_WIDTH + LORA_W + LORA_A:]
    lane = lax.broadcasted_iota(jnp.int32, xwa.shape, 1)
    la = _dot(jnp.where(lane < LORA_W, jnp.tanh(xwa), xwa), wla)
    wpre = w0 + la[:, :RW_WIDTH]
    softplus = jnp.maximum(-wpre, 0.0) + jnp.log(1.0 + jnp.exp(-jnp.abs(wpre)))
    lw = -jnp.exp(-softplus - 0.5)
    a = _sigmoid(a0 + la[:, RW_WIDTH:])
    g = _dot(_sigmoid(xg), g2)
    kkr = k * k_k
    ss = _dot(kkr * kkr, eones)
    kk = kkr / jnp.maximum(jnp.sqrt(ss), 1e-12)
    kp = k * (1.0 + (a - 1.0) * k_a)
    bonus = _dot(r * kp * r_k, eones) * v
    return r, v, lw, a, g, kk, kp, bonus


def _rwkv_output(y, bonus, g, lnx_g, lnx_b, eones, w_o):
    emean = eones * (1.0 / HEAD_DIM)
    mean = _dot(y, emean)
    d = y - mean
    var = _dot(d * d, emean)
    yn = d * lax.rsqrt(var + GN_EPS) * lnx_g + lnx_b
    return _dot((yn + bonus) * g, w_o)


def _head_masks(shape):
    lane = lax.broadcasted_iota(jnp.int32, shape, 1)
    return lane < HEAD_DIM, lane >= HEAD_DIM


def _stack_heads(x, m_a, m_b):
    return jnp.concatenate([jnp.where(m_a, x, 0.0), jnp.where(m_b, x, 0.0)], axis=0)


def _rwkv_prompt_kernel(p_ref, mu_ref, w0_ref, a0_ref, kk_ref, ka_ref, rk_ref, lg_ref, lb_ref,
                        wla_ref, g2_ref, e_ref, ltri_ref, wo_ref,
                        out_ref, sout_ref,
                        carry_ref, sp_ref, y_ref, *, tt):
    i = pl.program_id(1)

    @pl.when(i == 0)
    def _():
        carry_ref[...] = jnp.zeros_like(carry_ref)
        sp_ref[...] = jnp.zeros_like(sp_ref)

    p = p_ref[...]
    rows = lax.broadcasted_iota(jnp.int32, (tt, 1), 0)
    prev = jnp.where(rows == 0, carry_ref[...], pltpu.roll(p, 1, axis=0))
    carry_ref[...] = p[tt - 1:tt, :]
    eones = e_ref[...]
    r, v, lw, a, g, kk, kp, bonus = _rwkv_token_math(
        p, prev, mu_ref[...], w0_ref[...], a0_ref[...], kk_ref[...], ka_ref[...], rk_ref[...],
        wla_ref[...], g2_ref[...], eones)

    c = _dot_exact_lhs(ltri_ref[...], lw)
    at = -kk * jnp.exp(c - lw)
    rt = r * jnp.exp(c)
    einv = jnp.exp(-c)
    bvec = kk * a
    bt = bvec * einv
    kt = kp * einv

    m_a, m_b = _head_masks((WKV_CHUNK, LANES))
    ri = lax.broadcasted_iota(jnp.int32, (2 * WKV_CHUNK, 2 * WKV_CHUNK), 0)
    ci = lax.broadcasted_iota(jnp.int32, (2 * WKV_CHUNK, 2 * WKV_CHUNK), 1)
    same_head = (ri >= WKV_CHUNK) == (ci >= WKV_CHUNK)
    strict = same_head & (ri > ci)
    incl = same_head & (ri >= ci)

    nch = tt // WKV_CHUNK
    blocks = [(ch, j) for ch in range(nch) for j in range(N_PAIRS)]
    rows_of = lambda ch: slice(ch * WKV_CHUNK, (ch + 1) * WKV_CHUNK)
    lanes_of = lambda j: slice(j * LANES, (j + 1) * LANES)
    blk = lambda x, ch, j: x[rows_of(ch), lanes_of(j)]
    stack = lambda x, ch, j: _stack_heads(blk(x, ch, j), m_a, m_b)
    vss = [stack(v, ch, j) for ch, j in blocks]
    gms = [_dot_nt(jnp.concatenate([stack(at, ch, j), stack(rt, ch, j)], axis=0),
                   jnp.concatenate([stack(bt, ch, j), stack(kt, ch, j)], axis=0))
           for ch, j in blocks]
    n_abs = [jnp.where(strict, gm[:LANES, :LANES], 0.0) for gm in gms]
    wvs = [_dot(jnp.where(strict, gm[:LANES, LANES:], 0.0), vs) for gm, vs in zip(gms, vss)]
    a_ys = [jnp.concatenate([jnp.where(incl, gm[LANES:, :LANES], 0.0),
                             jnp.where(incl, gm[LANES:, LANES:], 0.0)], axis=1) for gm in gms]
    eye = (ri == ci).astype(F32)
    tinvs = [eye + n for n in n_abs]
    pows = n_abs
    for _ in range(5):
        pows = [_dot(pw, pw) for pw in pows]
        tinvs = [t + _dot(pw, t) for pw, t in zip(pows, tinvs)]

    sps = [sp_ref[j] for j in range(N_PAIRS)]
    for ch in range(nch):
        c_end = c[(ch + 1) * WKV_CHUNK - 1:(ch + 1) * WKV_CHUNK, :]
        g_end = jnp.exp(c_end)
        e_rest = jnp.exp(c_end - c[rows_of(ch), :])
        bhat = bvec[rows_of(ch), :] * e_rest
        khat = kp[rows_of(ch), :] * e_rest
        idx = [ch * N_PAIRS + j for j in range(N_PAIRS)]
        arss = [_dot_nt(jnp.concatenate([blk(at, ch, j), blk(rt, ch, j)], axis=0), sps[j])
                for j in range(N_PAIRS)]
        uss = [_dot(tinvs[idx[j]], _stack_heads(arss[j][:WKV_CHUNK], m_a, m_b) + wvs[idx[j]])
               for j in range(N_PAIRS)]
        uvs = [jnp.concatenate([uss[j], vss[idx[j]]], axis=0) for j in range(N_PAIRS)]
        yss = [_dot(a_ys[idx[j]], uvs[j]) for j in range(N_PAIRS)]
        incs = [_dot_tn(uvs[j], jnp.concatenate(
            [_stack_heads(bhat[:, lanes_of(j)], m_a, m_b),
             _stack_heads(khat[:, lanes_of(j)], m_a, m_b)], axis=0)) for j in range(N_PAIRS)]
        for j in range(N_PAIRS):
            y_ref[rows_of(ch), lanes_of(j)] = (yss[j][:WKV_CHUNK] + yss[j][WKV_CHUNK:]
                                               + arss[j][WKV_CHUNK:])
            sps[j] = sps[j] * g_end[:, lanes_of(j)] + incs[j]
    for j in range(N_PAIRS):
        sp_ref[j] = sps[j]

    out_ref[...] = _rwkv_output(y_ref[...], bonus, g, lg_ref[...], lb_ref[...], eones, wo_ref[...])

    @pl.when(i == pl.num_programs(1) - 1)
    def _():
        for j in range(N_PAIRS):
            sp = sp_ref[j]
            sout_ref[0, 2 * j] = sp[:HEAD_DIM, :HEAD_DIM]
            sout_ref[0, 2 * j + 1] = sp[HEAD_DIM:, HEAD_DIM:]


def _rwkv_consts():
    idx = jnp.arange(RW_WIDTH)
    eones = (idx[:, None] // HEAD_DIM == idx[None, :] // HEAD_DIM).astype(BF16)
    return eones


def _rwkv_weights(w2, a2, g2, w_rw_out):
    wla = jnp.zeros((LORA_W + LORA_A, 2 * RW_WIDTH), F32)
    wla = wla.at[:LORA_W, :RW_WIDTH].set(w2).at[LORA_W:, RW_WIDTH:].set(a2)
    return wla.astype(BF16), g2.astype(BF16), w_rw_out.astype(BF16)


def _row_params(mu_shift, w0, a0, k_k, k_a, r_k, lnx_g, lnx_b):
    return (mu_shift.reshape(1, SHIFT_WIDTH), w0.reshape(1, RW_WIDTH), a0.reshape(1, RW_WIDTH),
            k_k.reshape(1, RW_WIDTH), k_a.reshape(1, RW_WIDTH), r_k.reshape(1, RW_WIDTH),
            lnx_g.reshape(1, RW_WIDTH), lnx_b.reshape(1, RW_WIDTH))


def _rwkv_prompt(prw, rowp, wla, g2b, eones, wo, nb, t, tt):
    nt = t // tt
    idx = jnp.arange(tt)
    ltri = ((idx[:, None] // WKV_CHUNK == idx[None, :] // WKV_CHUNK)
            & (idx[:, None] >= idx[None, :])).astype(BF16)
    consts = list(rowp) + [wla, g2b, eones, ltri, wo]
    return pl.pallas_call(
        functools.partial(_rwkv_prompt_kernel, tt=tt),
        grid=(nb, nt),
        in_specs=[pl.BlockSpec((tt, SHIFT_WIDTH), lambda b, i: (b * nt + i, 0))]
        + [_full_spec(c.shape) for c in consts],
        out_specs=[pl.BlockSpec((tt, D_MODEL), lambda b, i: (b * nt + i, 0)),
                   pl.BlockSpec((1, RW_HEADS, HEAD_DIM, HEAD_DIM), lambda b, i: (b, 0, 0, 0))],
        out_shape=[jax.ShapeDtypeStruct((nb * t, D_MODEL), F32),
                   jax.ShapeDtypeStruct((nb, RW_HEADS, HEAD_DIM, HEAD_DIM), F32)],
        scratch_shapes=[pltpu.VMEM((1, SHIFT_WIDTH), F32),
                        pltpu.VMEM((N_PAIRS, LANES, LANES), F32),
                        pltpu.VMEM((tt, RW_WIDTH), F32)],
        compiler_params=_params(("arbitrary", "arbitrary")),
        name="rwkv_prompt",
    )(prw, *consts)


def _rwkv_step_kernel(p_ref, prev_ref, s_ref, mu_ref, w0_ref, a0_ref, kk_ref, ka_ref, rk_ref,
                      lg_ref, lb_ref, wla_ref, g2_ref, e_ref, wo_ref,
                      out_ref, sout_ref,
                      q_ref, wr_ref, bv_ref, kp_ref, v_ref, w_ref, sa_ref, y1_ref, *, tb):
    eones = e_ref[...]
    r, v, lw, a, g, kk, kp, bonus = _rwkv_token_math(
        p_ref[...], prev_ref[...], mu_ref[...], w0_ref[...], a0_ref[...], kk_ref[...],
        ka_ref[...], rk_ref[...], wla_ref[...], g2_ref[...], eones)
    w = jnp.exp(lw)
    bvec = kk * a
    q_ref[...] = -kk
    wr_ref[...] = w * r
    bv_ref[...] = bvec
    kp_ref[...] = kp
    v_ref[...] = v
    w_ref[...] = w

    ri = lax.broadcasted_iota(jnp.int32, (HEAD_DIM, HEAD_DIM), 0)
    ci = lax.broadcasted_iota(jnp.int32, (HEAD_DIM, HEAD_DIM), 1)
    eye = ri == ci
    zpad = jnp.zeros((6, HEAD_DIM), F32)

    def body(b, carry):
        q_row, wr_row, bv_row, kp_row, v_row, w_row = (
            ref[pl.ds(b, 1), :] for ref in (q_ref, wr_ref, bv_ref, kp_ref, v_ref, w_ref))
        hsl = [slice(h * HEAD_DIM, (h + 1) * HEAD_DIM) for h in range(RW_HEADS)]
        ss = [s_ref[b, h] for h in range(RW_HEADS)]
        r2s = [_dot_nt(jnp.concatenate([q_row[:, hs], wr_row[:, hs], zpad], axis=0), s)
               for hs, s in zip(hsl, ss)]
        incs = []
        for hs, r2 in zip(hsl, r2s):
            dsa = jnp.where(eye, jnp.broadcast_to(r2[0:1], (HEAD_DIM, HEAD_DIM)), 0.0)
            dv = jnp.where(eye, jnp.broadcast_to(v_row[:, hs], (HEAD_DIM, HEAD_DIM)), 0.0)
            bb = jnp.broadcast_to(bv_row[:, hs], (HEAD_DIM, HEAD_DIM))
            kb = jnp.broadcast_to(kp_row[:, hs], (HEAD_DIM, HEAD_DIM))
            lhs = jnp.concatenate([dsa, dv], axis=1)
            rhs = jnp.concatenate([bb, kb], axis=0)
            l1 = lhs.astype(BF16)
            l2 = (lhs - l1.astype(F32)).astype(BF16)
            r1 = rhs.astype(BF16)
            r2_ = (rhs - r1.astype(F32)).astype(BF16)
            d = lambda x, y: jnp.dot(x, y, preferred_element_type=F32)
            incs.append(d(l1, r1) + (d(l1, r2_) + d(l2, r1)))
        for h in range(RW_HEADS):
            sout_ref[b, h] = ss[h] * w_row[:, hsl[h]] + incs[h]
        sa_ref[pl.ds(b, 1), :] = jnp.concatenate([r2[0:1] for r2 in r2s], axis=1)
        y1_ref[pl.ds(b, 1), :] = jnp.concatenate([r2[1:2] for r2 in r2s], axis=1)
        return carry

    lax.fori_loop(0, tb, body, 0)

    sa = sa_ref[...]
    y = y1_ref[...] + sa * _dot(bvec * r, eones) + v * _dot(kp * r, eones)
    out_ref[...] = _rwkv_output(y, bonus, g, lg_ref[...], lb_ref[...], eones, wo_ref[...])


def _rwkv_step(prw, prev, state, rowp, wla, g2b, eones, wo, tb):
    n = prw.shape[0]
    consts = list(rowp) + [wla, g2b, eones, wo]
    rowspec = lambda w: pl.BlockSpec((tb, w), lambda i: (i, 0))
    sspec = pl.BlockSpec((tb, RW_HEADS, HEAD_DIM, HEAD_DIM), lambda i: (i, 0, 0, 0))
    return pl.pallas_call(
        functools.partial(_rwkv_step_kernel, tb=tb),
        grid=(n // tb,),
        in_specs=[rowspec(SHIFT_WIDTH), rowspec(SHIFT_WIDTH), sspec]
        + [_full_spec(c.shape) for c in consts],
        out_specs=[rowspec(D_MODEL), sspec],
        out_shape=[jax.ShapeDtypeStruct((n, D_MODEL), F32),
                   jax.ShapeDtypeStruct((n, RW_HEADS, HEAD_DIM, HEAD_DIM), F32)],
        scratch_shapes=[pltpu.VMEM((tb, RW_WIDTH), F32) for _ in range(8)],
        compiler_params=_params(("arbitrary",)),
        name="rwkv_step",
    )(prw, prev, state, *consts)


def _s5_tables(A_re, A_im, log_dt, B_re, B_im, C_re, C_im):
    dt = jnp.exp(log_dt)[:, None]
    lr, li = A_re, A_im
    mag = jnp.exp(lr * dt)
    ar, ai = mag * jnp.cos(li * dt), mag * jnp.sin(li * dt)
    den = lr * lr + li * li
    fr = ((ar - 1.0) * lr + ai * li) / den
    fi = (ai * lr - (ar - 1.0) * li) / den
    br = fr[..., None] * B_re - fi[..., None] * B_im
    bi = fr[..., None] * B_im + fi[..., None] * B_re
    lags = jnp.arange(S5_CHUNK + 1, dtype=F32)[:, None, None]
    pm = jnp.exp(lr * dt * lags)
    er, ei = pm * jnp.cos(li * dt * lags), pm * jnp.sin(li * dt * lags)
    return ar, ai, br, bi, er, ei


S5_GQ = LANES // SSM_GROUP


def _s5_prompt_kernel(u_ref, m_ref, pre_ref, pim_ref, cre_ref, cim_ref, a16_ref,
                      y_ref, xr_ref, xi_ref,
                      u2_ref, bre_ref, bim_ref, xpr_ref, xpi_ref, *, nb, nch):
    L, C, GQ = S5_CHUNK, SSM_GROUP, S5_GQ
    npair = GQ // 2

    def relayout_in(b, carry):
        uts = [u_ref[b, pl.ds(j, nch, stride=L), :].T for j in range(L)]
        for g in range(GQ):
            for hf in range(2):
                xt = jnp.concatenate([ut[g * C:(g + 1) * C, :]
                                      for ut in uts[hf * L // 2:(hf + 1) * L // 2]], axis=0)
                u2_ref[g, hf, pl.ds(b, nch, stride=nb), :] = xt.T
        return carry

    u2 = lambda g: jnp.concatenate([u2_ref[g, 0], u2_ref[g, 1]], axis=1)

    lax.fori_loop(0, nb, relayout_in, 0)

    for m in range(npair):
        ub2 = jnp.concatenate([u2(2 * m), u2(2 * m + 1)], axis=1).astype(BF16)
        bre_ref[m] = jnp.dot(ub2, pre_ref[m], preferred_element_type=F32)
        bim_ref[m] = jnp.dot(ub2, pim_ref[m], preferred_element_type=F32)

    ars = [a16_ref[m, 0:1, :] for m in range(npair)]
    ais = [a16_ref[m, 1:2, :] for m in range(npair)]

    def step(cidx, carry):
        o = pl.multiple_of(cidx * nb, nb)
        out = []
        for m in range(npair):
            xr, xi = carry[2 * m], carry[2 * m + 1]
            xpr_ref[m, pl.ds(o, nb), :] = xr
            xpi_ref[m, pl.ds(o, nb), :] = xi
            out.append(ars[m] * xr - ais[m] * xi + bre_ref[m, pl.ds(o, nb), :])
            out.append(ars[m] * xi + ais[m] * xr + bim_ref[m, pl.ds(o, nb), :])
        return tuple(out)

    z = jnp.zeros((nb, LANES), F32)
    fin = lax.fori_loop(0, nch, step, (z,) * GQ)
    for m in range(npair):
        xr_ref[m] = fin[2 * m]
        xi_ref[m] = fin[2 * m + 1]

    for m in range(npair):
        ys = _dot(xpr_ref[m], cre_ref[m]) + _dot(xpi_ref[m], cim_ref[m])
        for k in range(2):
            g = 2 * m + k
            yg = _dot(u2(g), m_ref[g]) + ys[:, k * L * C:(k + 1) * L * C]
            u2_ref[g, 0] = yg[:, :LANES]
            u2_ref[g, 1] = yg[:, LANES:]

    def relayout_out(b, carry):
        for hf in range(2):
            yts = [u2_ref[g, hf, pl.ds(b, nch, stride=nb), :].T for g in range(GQ)]
            for i in range(L // 2):
                yt = jnp.concatenate([t_[i * C:(i + 1) * C, :] for t_ in yts], axis=0)
                y_ref[b, pl.ds(hf * L // 2 + i, nch, stride=L), :] = yt.T
        return carry

    lax.fori_loop(0, nb, relayout_out, 0)


def _s5_prompt(u, tabs, C_re, C_im, nb, t):
    ar, ai, br, bi, er, ei = tabs
    L, G, N, C = S5_CHUNK, SSM_GROUPS, SSM_STATE, SSM_GROUP
    nch = t // L
    nc = nch * nb
    nq = SSM_WIDTH // LANES
    ce_r = C_re[None] * er[:, :, None, :] - C_im[None] * ei[:, :, None, :]
    ce_i = C_re[None] * ei[:, :, None, :] + C_im[None] * er[:, :, None, :]
    kl = (jnp.einsum('lgcn,gnd->lgcd', ce_r[:L], br, precision=lax.Precision.HIGHEST)
          - jnp.einsum('lgcn,gnd->lgcd', ce_i[:L], bi, precision=lax.Precision.HIGHEST))
    jj = jnp.arange(L)
    lag = jj[None, :] - jj[:, None]
    m = kl[jnp.clip(lag, 0, L - 1)]
    m = jnp.where((lag >= 0)[:, :, None, None, None], m, 0.0)
    m = m.transpose(2, 0, 4, 1, 3).reshape(G, L * C, L * C)
    erj, eij = er[L - 1 - jj], ei[L - 1 - jj]
    pr = erj[..., None] * br[None] - eij[..., None] * bi[None]
    pi = erj[..., None] * bi[None] + eij[..., None] * br[None]
    flat_in = lambda p: p.transpose(1, 0, 3, 2).reshape(G, L * C, N)
    cre = ce_r[1:].transpose(1, 3, 0, 2).reshape(G, N, L * C)
    cim = (-ce_i[1:]).transpose(1, 3, 0, 2).reshape(G, N, L * C)

    def pair_diag(w):
        g2, r, c = w.shape[0] // 2, w.shape[1], w.shape[2]
        w = w.reshape(g2, 2, r, c)
        eye = jnp.eye(2, dtype=w.dtype)
        return (w[:, :, :, None, :] * eye[None, :, None, :, None]).reshape(g2, 2 * r, 2 * c)

    a16 = jnp.stack([er[L].reshape(G // 2, 2 * N), ei[L].reshape(G // 2, 2 * N)], axis=1)
    npair = S5_GQ // 2
    spec = lambda blk: pl.BlockSpec(blk, lambda q: (q, 0, 0))
    y, xr, xi = pl.pallas_call(
        functools.partial(_s5_prompt_kernel, nb=nb, nch=nch),
        grid=(nq,),
        in_specs=[pl.BlockSpec((nb, t, LANES), lambda q: (0, 0, q), pipeline_mode=pl.Buffered(1)),
                  spec((S5_GQ, L * C, L * C)),
                  spec((npair, 2 * L * C, 2 * N)), spec((npair, 2 * L * C, 2 * N)),
                  spec((npair, 2 * N, 2 * L * C)), spec((npair, 2 * N, 2 * L * C)),
                  spec((npair, 2, 2 * N))],
        out_specs=[pl.BlockSpec((nb, t, LANES), lambda q: (0, 0, q)),
                   spec((npair, nb, 2 * N)), spec((npair, nb, 2 * N))],
        out_shape=[jax.ShapeDtypeStruct((nb, t, SSM_WIDTH), F32),
                   jax.ShapeDtypeStruct((G // 2, nb, 2 * N), F32),
                   jax.ShapeDtypeStruct((G // 2, nb, 2 * N), F32)],
        scratch_shapes=[pltpu.VMEM((S5_GQ, 2, nc, LANES), F32)]
        + [pltpu.VMEM((npair, nc, 2 * N), F32) for _ in range(4)],
        compiler_params=_params(("arbitrary",)),
        name="s5_prompt",
    )(u.reshape(nb, t, SSM_WIDTH), m.astype(BF16),
      pair_diag(flat_in(pr)).astype(BF16), pair_diag(flat_in(pi)).astype(BF16),
      pair_diag(cre).astype(BF16), pair_diag(cim).astype(BF16), a16)
    unpair = lambda x: x.reshape(G // 2, nb, 2, N).transpose(1, 0, 2, 3).reshape(nb, G, N)
    return y.reshape(nb * t, SSM_WIDTH), unpair(xr), unpair(xi)


def _s5_step_kernel(u_ref, xr_ref, xi_ref, ar_ref, ai_ref, bdr_ref, bdi_ref, cdr_ref, cdi_ref,
                    y_ref, nr_ref, ni_ref):
    ub = u_ref[...].astype(BF16)
    xr, xi, ar, ai = xr_ref[...], xi_ref[...], ar_ref[...], ai_ref[...]
    nr = ar * xr - ai * xi + jnp.dot(ub, bdr_ref[...], preferred_element_type=F32)
    ni = ar * xi + ai * xr + jnp.dot(ub, bdi_ref[...], preferred_element_type=F32)
    nr_ref[...] = nr
    ni_ref[...] = ni
    y_ref[...] = _dot(nr, cdr_ref[...]) - _dot(ni, cdi_ref[...])


def _s5_step(u, st_re, st_im, tabs, C_re, C_im):
    ar, ai, br, bi, _, _ = tabs
    G, N, C = SSM_GROUPS, SSM_STATE, SSM_GROUP
    n = u.shape[0]
    eye = jnp.eye(G, dtype=F32)
    bd = lambda b: (eye[:, None, :, None] * b.transpose(0, 2, 1)[:, :, None, :]).reshape(G * C, G * N)
    cd = lambda c: (eye[:, None, :, None] * c.transpose(0, 2, 1)[:, :, None, :]).reshape(G * N, G * C)
    ops = (u, st_re.reshape(n, G * N), st_im.reshape(n, G * N),
           ar.reshape(1, G * N), ai.reshape(1, G * N),
           bd(br).astype(BF16), bd(bi).astype(BF16), cd(C_re).astype(BF16), cd(C_im).astype(BF16))
    y, nr, ni = pl.pallas_call(
        _s5_step_kernel,
        grid=(1,),
        in_specs=[_full_spec(o.shape) for o in ops],
        out_specs=[_full_spec((n, SSM_WIDTH)), _full_spec((n, G * N)), _full_spec((n, G * N))],
        out_shape=[jax.ShapeDtypeStruct((n, SSM_WIDTH), F32),
                   jax.ShapeDtypeStruct((n, G * N), F32),
                   jax.ShapeDtypeStruct((n, G * N), F32)],
        compiler_params=_params(("arbitrary",)),
        name="s5_step",
    )(*ops)
    return y, nr.reshape(n, G, N), ni.reshape(n, G, N)


def _tail_kernel(*refs, tm, tiles_per_seq, decode, final):
    if decode:
        (x_ref, rw_ref, ys_ref, u_ref, gl_ref, pe_ref, st2_ref, st1_ref,
         dsk_ref, wglu_ref, wout_ref, ln2_ref, wfi_ref, cw_ref, cb_ref, wfo_ref,
         ln3_ref, wpg_ref, wple_ref, fin_ref, y_ref, conv_ref) = refs
    else:
        (x_ref, rw_ref, ys_ref, u_ref, gl_ref, pe_ref,
         dsk_ref, wglu_ref, wout_ref, ln2_ref, wfi_ref, cw_ref, cb_ref, wfo_ref,
         ln3_ref, wpg_ref, wple_ref, fin_ref, y_ref, conv_ref, carry_ref) = refs

    x = x_ref[...]
    z = _gelu(ys_ref[...] + dsk_ref[...] * u_ref[...])
    zz = _dot(z, wglu_ref[...])
    s5 = zz[:, :D_MODEL] * _sigmoid(zz[:, D_MODEL:])
    gl = gl_ref[...]
    merged = _sigmoid(gl[:, :D_MODEL]) * rw_ref[...] + _sigmoid(gl[:, D_MODEL:]) * s5
    x = x + _dot(merged, wout_ref[...])

    h2 = _rmsnorm(x, ln2_ref[...]).astype(BF16)
    a_up = jnp.dot(h2, wfi_ref[:, :D_FF], preferred_element_type=F32)
    b_up = jnp.dot(h2, wfi_ref[:, D_FF:], preferred_element_type=F32)
    cw = cw_ref[...]
    if decode:
        prev2 = st2_ref[...]
        prev1 = st1_ref[...]
        conv_ref[...] = a_up
    else:
        i = pl.program_id(0)

        @pl.when(i % tiles_per_seq == 0)
        def _():
            carry_ref[...] = jnp.zeros_like(carry_ref)

        rows = lax.broadcasted_iota(jnp.int32, (tm, 1), 0)
        c0 = carry_ref[0:1, :]
        c1 = carry_ref[1:2, :]
        prev1 = jnp.where(rows == 0, c1, pltpu.roll(a_up, 1, axis=0))
        prev2 = jnp.where(rows == 0, c0, jnp.where(rows == 1, c1, pltpu.roll(a_up, 2, axis=0)))
        carry_ref[...] = a_up[tm - 2:tm, :]
        conv_ref[0] = a_up[tm - 2:tm, :]
    a_conv = cw[0:1] * prev2 + cw[1:2] * prev1 + cw[2:3] * a_up + cb_ref[...]
    x = x + _dot(_gelu(a_conv) * b_up, wfo_ref[...])

    pg = _sigmoid(_dot(_rmsnorm(x, ln3_ref[...]), wpg_ref[...]))
    x = x + pg * _dot(pe_ref[...], wple_ref[...])
    y_ref[...] = _rmsnorm(x, fin_ref[...]) if final else x


def _tail(x2d, rw, ys, u, gl, pe, st_conv, D_skip, wglu, wout, ln2_g, wfi, conv_w, conv_b, wfo,
          ln3_g, wpg, wple, final_g, *, tm, tiles_per_seq, decode, final):
    n = x2d.shape[0]
    row = lambda w: pl.BlockSpec((tm, w), lambda i: (i, 0))
    consts = [D_skip.reshape(1, SSM_WIDTH), wglu, wout, ln2_g.reshape(1, D_MODEL), wfi,
              conv_w, conv_b.reshape(1, D_FF), wfo, ln3_g.reshape(1, D_MODEL), wpg, wple,
              final_g.reshape(1, D_MODEL)]
    acts = [x2d, rw, ys, u, gl, pe]
    in_specs = [row(a.shape[1]) for a in acts]
    scratch = []
    if decode:
        acts += [st_conv[:, 0], st_conv[:, 1]]
        in_specs += [row(D_FF), row(D_FF)]
        conv_spec = row(D_FF)
        conv_shape = (n, D_FF)
    else:
        nseq = n // (tm * tiles_per_seq)
        conv_spec = pl.BlockSpec((1, 2, D_FF), lambda i: (i // tiles_per_seq, 0, 0))
        conv_shape = (nseq, 2, D_FF)
        scratch = [pltpu.VMEM((2, D_FF), F32)]
    in_specs += [pl.BlockSpec(c.shape, lambda i, nd=c.ndim: (0,) * nd,
                              pipeline_mode=pl.Buffered(1)) for c in consts]
    y, conv = pl.pallas_call(
        functools.partial(_tail_kernel, tm=tm, tiles_per_seq=tiles_per_seq, decode=decode,
                          final=final),
        grid=(n // tm,),
        in_specs=in_specs,
        out_specs=[row(D_MODEL), conv_spec],
        out_shape=[jax.ShapeDtypeStruct((n, D_MODEL), F32),
                   jax.ShapeDtypeStruct(conv_shape, F32)],
        scratch_shapes=scratch,
        compiler_params=_params(("arbitrary",)),
        name="tail_decode" if decode else "tail_prompt",
    )(*acts, *consts)
    if decode:
        conv = jnp.stack([st_conv[:, 1], conv], axis=1)
    return y, conv


def _layer(x, pe, states, W, *, decode, final, tm_proj, tt, tm_tail, tb):
    nb, t, _ = x.shape
    n = nb * t
    x2d = x.reshape(n, D_MODEL)
    prw, u, gl = _proj(x2d, W['ln1_g'], W['w_in'], tm_proj)
    if decode:
        st_shift, st_wkv, st_re, st_im, st_conv = states
        rw, s_new = _rwkv_step(prw, st_shift, st_wkv, W['rowp'], W['wla'], W['g2'], W['eones'],
                               W['w_rw_out'], tb)
        ys, xr, xi = _s5_step(u, st_re, st_im, W['s5'], W['C_re'], W['C_im'])
        new_shift = prw
    else:
        st_conv = None
        rw, s_new = _rwkv_prompt(prw, W['rowp'], W['wla'], W['g2'], W['eones'], W['w_rw_out'],
                                 nb, t, tt)
        ys, xr, xi = _s5_prompt(u, W['s5'], W['C_re'], W['C_im'], nb, t)
        new_shift = prw.reshape(nb, t, SHIFT_WIDTH)[:, -1]
    y, new_conv = _tail(x2d, rw, ys, u, gl, pe.reshape(n, PLE_DIM), st_conv, W['D_skip'],
                        W['w_glu'], W['w_out'], W['ln2_g'], W['w_ffn_in'], W['conv_w'],
                        W['conv_b'], W['w_ffn_out'], W['ln3_g'], W['w_ple_gate'], W['w_ple'],
                        W['final_g'], tm=tm_tail, tiles_per_seq=max(t // tm_tail, 1),
                        decode=decode, final=final)
    return y.reshape(nb, t, D_MODEL), (new_shift, s_new, xr, xi, new_conv)


def kernel(x_prompt, x_sample, p_prompt, p_sample, state_shift, state_wkv, state_ssm_re, state_ssm_im, state_conv, ln1_g, w_in, mu_shift, w0, w2, a0, a2, g2, k_k, k_a, r_k, lnx_g, lnx_b, w_rw_out, A_re, A_im, log_dt, B_re, B_im, C_re, C_im, D_skip, w_glu, w_out, ln2_g, w_ffn_in, conv_w, conv_b, w_ffn_out, ln3_g, w_ple_gate, w_ple, final_g):
    depth = w_in.shape[0]
    xp, xs = x_prompt, x_sample
    pst = [[] for _ in range(5)]
    sst = [[] for _ in range(5)]
    for i in range(depth):
        wla, g2b, wo = _rwkv_weights(w2[i], a2[i], g2[i], w_rw_out[i])
        W = dict(
            ln1_g=ln1_g[i], w_in=w_in[i].astype(BF16),
            rowp=_row_params(mu_shift[i], w0[i], a0[i], k_k[i], k_a[i], r_k[i].reshape(-1),
                             lnx_g[i], lnx_b[i]),
            wla=wla, g2=g2b, eones=_rwkv_consts(), w_rw_out=wo,
            s5=_s5_tables(A_re[i], A_im[i], log_dt[i], B_re[i], B_im[i], C_re[i], C_im[i]),
            C_re=C_re[i], C_im=C_im[i], D_skip=D_skip[i],
            w_glu=w_glu[i].astype(BF16), w_out=w_out[i].astype(BF16), ln2_g=ln2_g[i],
            w_ffn_in=w_ffn_in[i].astype(BF16), conv_w=conv_w[i], conv_b=conv_b[i],
            w_ffn_out=w_ffn_out[i].astype(BF16), ln3_g=ln3_g[i],
            w_ple_gate=w_ple_gate[i].astype(BF16), w_ple=w_ple[i].astype(BF16), final_g=final_g)
        final = i == depth - 1
        tp = xp.shape[1]
        xp, sp = _layer(xp, p_prompt[i], None, W, decode=False, final=final,
                        tm_proj=min(TM_PROJ, tp), tt=min(TT_RWKV, tp), tm_tail=min(TM_TAIL, tp),
                        tb=None)
        ns = xs.shape[0]
        xs, ss = _layer(xs, p_sample[i],
                        (state_shift[i], state_wkv[i], state_ssm_re[i], state_ssm_im[i],
                         state_conv[i]),
                        W, decode=True, final=final, tm_proj=ns, tt=None, tm_tail=ns,
                        tb=min(TB_STEP, ns))
        for j in range(5):
            pst[j].append(sp[j])
            sst[j].append(ss[j])
    return (xp, xs,
            jnp.stack(pst[0]), jnp.stack(pst[1]), jnp.stack(pst[2]), jnp.stack(pst[3]),
            jnp.stack(pst[4]),
            jnp.stack(sst[0]), jnp.stack(sst[1]), jnp.stack(sst[2]), jnp.stack(sst[3]),
            jnp.stack(sst[4]))
```

```python
import functools
import math

import jax
import jax.numpy as jnp
from jax import lax
from jax.experimental import pallas as pl
from jax.experimental.pallas import tpu as pltpu

F32 = jnp.float32
BF16 = jnp.bfloat16

D_MODEL = 1024
RW_HEADS = 8
HEAD_DIM = 64
RW_WIDTH = RW_HEADS * HEAD_DIM
LORA_W = 64
LORA_A = 64
LORA_G = 128
SHIFT_WIDTH = 3 * RW_WIDTH + LORA_W + LORA_A + LORA_G
SSM_WIDTH = 512
SSM_GROUP = 16
SSM_GROUPS = SSM_WIDTH // SSM_GROUP
SSM_STATE = 64
GATE_OFF = SHIFT_WIDTH + SSM_WIDTH
IN_WIDTH = SHIFT_WIDTH + SSM_WIDTH + 2 * D_MODEL
D_FF = 2816
PLE_DIM = 256
EPS = 1e-6
GN_EPS = 64e-5

LANES = 128
WKV_CHUNK = 64
S5_CHUNK = 16
N_PAIRS = RW_HEADS // 2
VMEM_LIMIT = 56 * 1024 * 1024
TM_PROJ = 512
TT_RWKV = 256
TM_TAIL = 256
TB_STEP = 16


def _dot(a, b):
    return jnp.dot(a.astype(BF16), b.astype(BF16), preferred_element_type=F32)


def _dot_nt(a, b):
    return lax.dot_general(a.astype(BF16), b.astype(BF16), (((1,), (1,)), ((), ())),
                           preferred_element_type=F32)


def _dot_tn(a, b):
    return lax.dot_general(a.astype(BF16), b.astype(BF16), (((0,), (0,)), ((), ())),
                           preferred_element_type=F32)


def _split3(x):
    h1 = x.astype(BF16)
    r1 = x - h1.astype(F32)
    h2 = r1.astype(BF16)
    h3 = (r1 - h2.astype(F32)).astype(BF16)
    return h1, h2, h3


def _dot_exact_lhs(a_bf16, x):
    h1, h2, h3 = _split3(x)
    d = lambda h: jnp.dot(a_bf16, h, preferred_element_type=F32)
    return d(h1) + d(h2) + d(h3)


def _sigmoid(x):
    return 1.0 / (1.0 + jnp.exp(-x))


def _gelu(x):
    c = math.sqrt(2.0 / math.pi)
    return 0.5 * x * (1.0 + jnp.tanh(c * (x + 0.044715 * (x * x * x))))


def _rmsnorm(x, g):
    return x * lax.rsqrt(jnp.mean(x * x, axis=-1, keepdims=True) + EPS) * g


def _full_spec(shape):
    nd = len(shape)
    return pl.BlockSpec(shape, lambda *_: (0,) * nd)


def _params(sem):
    return pltpu.CompilerParams(dimension_semantics=sem, vmem_limit_bytes=VMEM_LIMIT)


def _proj_kernel(x_ref, g_ref, w_ref, prw_ref, u_ref, gl_ref):
    h = _rmsnorm(x_ref[...], g_ref[...]).astype(BF16)
    prw_ref[...] = jnp.dot(h, w_ref[:, :SHIFT_WIDTH], preferred_element_type=F32)
    u_ref[...] = jnp.dot(h, w_ref[:, SHIFT_WIDTH:GATE_OFF], preferred_element_type=F32)
    gl_ref[...] = jnp.dot(h, w_ref[:, GATE_OFF:], preferred_element_type=F32)


def _proj(x2d, ln1_g, w_in_bf, tm):
    n = x2d.shape[0]
    row = lambda w: pl.BlockSpec((tm, w), lambda i: (i, 0))
    return pl.pallas_call(
        _proj_kernel,
        grid=(n // tm,),
        in_specs=[row(D_MODEL), _full_spec((1, D_MODEL)), _full_spec((D_MODEL, IN_WIDTH))],
        out_specs=[row(SHIFT_WIDTH), row(SSM_WIDTH), row(2 * D_MODEL)],
        out_shape=[jax.ShapeDtypeStruct((n, SHIFT_WIDTH), F32),
                   jax.ShapeDtypeStruct((n, SSM_WIDTH), F32),
                   jax.ShapeDtypeStruct((n, 2 * D_MODEL), F32)],
        compiler_params=_params(("arbitrary",)),
        name="proj",
    )(x2d, ln1_g.reshape(1, D_MODEL), w_in_bf)


def _rwkv_token_math(p, prev, mu, w0, a0, k_k, k_a, r_k, wla, g2, eones):
    xs = p + (prev - p) * mu
    r = xs[:, 0:RW_WIDTH]
    k = xs[:, RW_WIDTH:2 * RW_WIDTH]
    v = xs[:, 2 * RW_WIDTH:3 * RW_WIDTH]
    xwa = xs[:, 3 * RW_WIDTH:3 * RW_WIDTH + LORA_W + LORA_A]
    xg = xs[:, 3 * RW_WIDTH + LORA_W + LORA_A:]
    lane = lax.broadcasted_iota(jnp.int32, xwa.shape, 1)
    la = _dot(jnp.where(lane < LORA_W, jnp.tanh(xwa), xwa), wla)
    wpre = w0 + la[:, :RW_WIDTH]
    softplus = jnp.maximum(-wpre, 0.0) + jnp.log(1.0 + jnp.exp(-jnp.abs(wpre)))
    lw = -jnp.exp(-softplus - 0.5)
    a = _sigmoid(a0 + la[:, RW_WIDTH:])
    g = _dot(_sigmoid(xg), g2)
    kkr = k * k_k
    ss = _dot(kkr * kkr, eones)
    kk = kkr / jnp.maximum(jnp.sqrt(ss), 1e-12)
    kp = k * (1.0 + (a - 1.0) * k_a)
    bonus = _dot(r * kp * r_k, eones) * v
    return r, v, lw, a, g, kk, kp, bonus


def _rwkv_output(y, bonus, g, lnx_g, lnx_b, eones, w_o):
    emean = eones * (1.0 / HEAD_DIM)
    mean = _dot(y, emean)
    d = y - mean
    var = _dot(d * d, emean)
    yn = d * lax.rsqrt(var + GN_EPS) * lnx_g + lnx_b
    return _dot((yn + bonus) * g, w_o)


def _head_masks(shape):
    lane = lax.broadcasted_iota(jnp.int32, shape, 1)
    return lane < HEAD_DIM, lane >= HEAD_DIM


def _stack_heads(x, m_a, m_b):
    return jnp.concatenate([jnp.where(m_a, x, 0.0), jnp.where(m_b, x, 0.0)], axis=0)


def _rwkv_prompt_kernel(p_ref, mu_ref, w0_ref, a0_ref, kk_ref, ka_ref, rk_ref, lg_ref, lb_ref,
                        wla_ref, g2_ref, e_ref, ltri_ref, wo_ref,
                        out_ref, sout_ref,
                        carry_ref, sp_ref, y_ref, *, tt):
    i = pl.program_id(1)

    @pl.when(i == 0)
    def _():
        carry_ref[...] = jnp.zeros_like(carry_ref)
        sp_ref[...] = jnp.zeros_like(sp_ref)

    p = p_ref[...]
    rows = lax.broadcasted_iota(jnp.int32, (tt, 1), 0)
    prev = jnp.where(rows == 0, carry_ref[...], pltpu.roll(p, 1, axis=0))
    carry_ref[...] = p[tt - 1:tt, :]
    eones = e_ref[...]
    r, v, lw, a, g, kk, kp, bonus = _rwkv_token_math(
        p, prev, mu_ref[...], w0_ref[...], a0_ref[...], kk_ref[...], ka_ref[...], rk_ref[...],
        wla_ref[...], g2_ref[...], eones)

    c = _dot_exact_lhs(ltri_ref[...], lw)
    at = -kk * jnp.exp(c - lw)
    rt = r * jnp.exp(c)
    einv = jnp.exp(-c)
    bvec = kk * a
    bt = bvec * einv
    kt = kp * einv

    m_a, m_b = _head_masks((WKV_CHUNK, LANES))
    ri = lax.broadcasted_iota(jnp.int32, (2 * WKV_CHUNK, 2 * WKV_CHUNK), 0)
    ci = lax.broadcasted_iota(jnp.int32, (2 * WKV_CHUNK, 2 * WKV_CHUNK), 1)
    same_head = (ri >= WKV_CHUNK) == (ci >= WKV_CHUNK)
    strict = same_head & (ri > ci)
    incl = same_head & (ri >= ci)

    nch = tt // WKV_CHUNK
    blocks = [(ch, j) for ch in range(nch) for j in range(N_PAIRS)]
    rows_of = lambda ch: slice(ch * WKV_CHUNK, (ch + 1) * WKV_CHUNK)
    lanes_of = lambda j: slice(j * LANES, (j + 1) * LANES)
    blk = lambda x, ch, j: x[rows_of(ch), lanes_of(j)]
    stack = lambda x, ch, j: _stack_heads(blk(x, ch, j), m_a, m_b)
    vss = [stack(v, ch, j) for ch, j in blocks]
    gms = [_dot_nt(jnp.concatenate([stack(at, ch, j), stack(rt, ch, j)], axis=0),
                   jnp.concatenate([stack(bt, ch, j), stack(kt, ch, j)], axis=0))
           for ch, j in blocks]
    n_abs = [jnp.where(strict, gm[:LANES, :LANES], 0.0) for gm in gms]
    wvs = [_dot(jnp.where(strict, gm[:LANES, LANES:], 0.0), vs) for gm, vs in zip(gms, vss)]
    a_ys = [jnp.concatenate([jnp.where(incl, gm[LANES:, :LANES], 0.0),
                             jnp.where(incl, gm[LANES:, LANES:], 0.0)], axis=1) for gm in gms]
    eye = (ri == ci).astype(F32)
    tinvs = [eye + n for n in n_abs]
    pows = n_abs
    for _ in range(5):
        pows = [_dot(pw, pw) for pw in pows]
        tinvs = [t + _dot(pw, t) for pw, t in zip(pows, tinvs)]

    sps = [sp_ref[j] for j in range(N_PAIRS)]
    for ch in range(nch):
        c_end = c[(ch + 1) * WKV_CHUNK - 1:(ch + 1) * WKV_CHUNK, :]
        g_end = jnp.exp(c_end)
        e_rest = jnp.exp(c_end - c[rows_of(ch), :])
        bhat = bvec[rows_of(ch), :] * e_rest
        khat = kp[rows_of(ch), :] * e_rest
        idx = [ch * N_PAIRS + j for j in range(N_PAIRS)]
        arss = [_dot_nt(jnp.concatenate([blk(at, ch, j), blk(rt, ch, j)], axis=0), sps[j])
                for j in range(N_PAIRS)]
        uss = [_dot(tinvs[idx[j]], _stack_heads(arss[j][:WKV_CHUNK], m_a, m_b) + wvs[idx[j]])
               for j in range(N_PAIRS)]
        uvs = [jnp.concatenate([uss[j], vss[idx[j]]], axis=0) for j in range(N_PAIRS)]
        yss = [_dot(a_ys[idx[j]], uvs[j]) for j in range(N_PAIRS)]
        incs = [_dot_tn(uvs[j], jnp.concatenate(
            [_stack_heads(bhat[:, lanes_of(j)], m_a, m_b),
             _stack_heads(khat[:, lanes_of(j)], m_a, m_b)], axis=0)) for j in range(N_PAIRS)]
        for j in range(N_PAIRS):
            y_ref[rows_of(ch), lanes_of(j)] = (yss[j][:WKV_CHUNK] + yss[j][WKV_CHUNK:]
                                               + arss[j][WKV_CHUNK:])
            sps[j] = sps[j] * g_end[:, lanes_of(j)] + incs[j]
    for j in range(N_PAIRS):
        sp_ref[j] = sps[j]

    out_ref[...] = _rwkv_output(y_ref[...], bonus, g, lg_ref[...], lb_ref[...], eones, wo_ref[...])

    @pl.when(i == pl.num_programs(1) - 1)
    def _():
        for j in range(N_PAIRS):
            sp = sp_ref[j]
            sout_ref[0, 2 * j] = sp[:HEAD_DIM, :HEAD_DIM]
            sout_ref[0, 2 * j + 1] = sp[HEAD_DIM:, HEAD_DIM:]


def _rwkv_consts():
    idx = jnp.arange(RW_WIDTH)
    eones = (idx[:, None] // HEAD_DIM == idx[None, :] // HEAD_DIM).astype(BF16)
    return eones


def _rwkv_weights(w2, a2, g2, w_rw_out):
    wla = jnp.zeros((LORA_W + LORA_A, 2 * RW_WIDTH), F32)
    wla = wla.at[:LORA_W, :RW_WIDTH].set(w2).at[LORA_W:, RW_WIDTH:].set(a2)
    return wla.astype(BF16), g2.astype(BF16), w_rw_out.astype(BF16)


def _row_params(mu_shift, w0, a0, k_k, k_a, r_k, lnx_g, lnx_b):
    return (mu_shift.reshape(1, SHIFT_WIDTH), w0.reshape(1, RW_WIDTH), a0.reshape(1, RW_WIDTH),
            k_k.reshape(1, RW_WIDTH), k_a.reshape(1, RW_WIDTH), r_k.reshape(1, RW_WIDTH),
            lnx_g.reshape(1, RW_WIDTH), lnx_b.reshape(1, RW_WIDTH))


def _rwkv_prompt(prw, rowp, wla, g2b, eones, wo, nb, t, tt):
    nt = t // tt
    idx = jnp.arange(tt)
    ltri = ((idx[:, None] // WKV_CHUNK == idx[None, :] // WKV_CHUNK)
            & (idx[:, None] >= idx[None, :])).astype(BF16)
    consts = list(rowp) + [wla, g2b, eones, ltri, wo]
    return pl.pallas_call(
        functools.partial(_rwkv_prompt_kernel, tt=tt),
        grid=(nb, nt),
        in_specs=[pl.BlockSpec((tt, SHIFT_WIDTH), lambda b, i: (b * nt + i, 0))]
        + [_full_spec(c.shape) for c in consts],
        out_specs=[pl.BlockSpec((tt, D_MODEL), lambda b, i: (b * nt + i, 0)),
                   pl.BlockSpec((1, RW_HEADS, HEAD_DIM, HEAD_DIM), lambda b, i: (b, 0, 0, 0))],
        out_shape=[jax.ShapeDtypeStruct((nb * t, D_MODEL), F32),
                   jax.ShapeDtypeStruct((nb, RW_HEADS, HEAD_DIM, HEAD_DIM), F32)],
        scratch_shapes=[pltpu.VMEM((1, SHIFT_WIDTH), F32),
                        pltpu.VMEM((N_PAIRS, LANES, LANES), F32),
                        pltpu.VMEM((tt, RW_WIDTH), F32)],
        compiler_params=_params(("arbitrary", "arbitrary")),
        name="rwkv_prompt",
    )(prw, *consts)


def _rwkv_step_kernel(p_ref, prev_ref, s_ref, mu_ref, w0_ref, a0_ref, kk_ref, ka_ref, rk_ref,
                      lg_ref, lb_ref, wla_ref, g2_ref, e_ref, wo_ref,
                      out_ref, sout_ref,
                      q_ref, wr_ref, bv_ref, kp_ref, v_ref, w_ref, sa_ref, y1_ref, *, tb):
    eones = e_ref[...]
    r, v, lw, a, g, kk, kp, bonus = _rwkv_token_math(
        p_ref[...], prev_ref[...], mu_ref[...], w0_ref[...], a0_ref[...], kk_ref[...],
        ka_ref[...], rk_ref[...], wla_ref[...], g2_ref[...], eones)
    w = jnp.exp(lw)
    bvec = kk * a
    q_ref[...] = -kk
    wr_ref[...] = w * r
    bv_ref[...] = bvec
    kp_ref[...] = kp
    v_ref[...] = v
    w_ref[...] = w

    ri = lax.broadcasted_iota(jnp.int32, (HEAD_DIM, HEAD_DIM), 0)
    ci = lax.broadcasted_iota(jnp.int32, (HEAD_DIM, HEAD_DIM), 1)
    eye = ri == ci
    zpad = jnp.zeros((6, HEAD_DIM), F32)

    def body(b, carry):
        q_row, wr_row, bv_row, kp_row, v_row, w_row = (
            ref[pl.ds(b, 1), :] for ref in (q_ref, wr_ref, bv_ref, kp_ref, v_ref, w_ref))
        hsl = [slice(h * HEAD_DIM, (h + 1) * HEAD_DIM) for h in range(RW_HEADS)]
        ss = [s_ref[b, h] for h in range(RW_HEADS)]
        r2s = [_dot_nt(jnp.concatenate([q_row[:, hs], wr_row[:, hs], zpad], axis=0), s)
               for hs, s in zip(hsl, ss)]
        incs = []
        for hs, r2 in zip(hsl, r2s):
            dsa = jnp.where(eye, jnp.broadcast_to(r2[0:1], (HEAD_DIM, HEAD_DIM)), 0.0)
            dv = jnp.where(eye, jnp.broadcast_to(v_row[:, hs], (HEAD_DIM, HEAD_DIM)), 0.0)
            bb = jnp.broadcast_to(bv_row[:, hs], (HEAD_DIM, HEAD_DIM))
            kb = jnp.broadcast_to(kp_row[:, hs], (HEAD_DIM, HEAD_DIM))
            lhs = jnp.concatenate([dsa, dv], axis=1)
            rhs = jnp.concatenate([bb, kb], axis=0)
            l1 = lhs.astype(BF16)
            l2 = (lhs - l1.astype(F32)).astype(BF16)
            r1 = rhs.astype(BF16)
            r2_ = (rhs - r1.astype(F32)).astype(BF16)
            d = lambda x, y: jnp.dot(x, y, preferred_element_type=F32)
            incs.append(d(l1, r1) + (d(l1, r2_) + d(l2, r1)))
        for h in range(RW_HEADS):
            sout_ref[b, h] = ss[h] * w_row[:, hsl[h]] + incs[h]
        sa_ref[pl.ds(b, 1), :] = jnp.concatenate([r2[0:1] for r2 in r2s], axis=1)
        y1_ref[pl.ds(b, 1), :] = jnp.concatenate([r2[1:2] for r2 in r2s], axis=1)
        return carry

    lax.fori_loop(0, tb, body, 0)

    sa = sa_ref[...]
    y = y1_ref[...] + sa * _dot(bvec * r, eones) + v * _dot(kp * r, eones)
    out_ref[...] = _rwkv_output(y, bonus, g, lg_ref[...], lb_ref[...], eones, wo_ref[...])


def _rwkv_step(prw, prev, state, rowp, wla, g2b, eones, wo, tb):
    n = prw.shape[0]
    consts = list(rowp) + [wla, g2b, eones, wo]
    rowspec = lambda w: pl.BlockSpec((tb, w), lambda i: (i, 0))
    sspec = pl.BlockSpec((tb, RW_HEADS, HEAD_DIM, HEAD_DIM), lambda i: (i, 0, 0, 0))
    return pl.pallas_call(
        functools.partial(_rwkv_step_kernel, tb=tb),
        grid=(n // tb,),
        in_specs=[rowspec(SHIFT_WIDTH), rowspec(SHIFT_WIDTH), sspec]
        + [_full_spec(c.shape) for c in consts],
        out_specs=[rowspec(D_MODEL), sspec],
        out_shape=[jax.ShapeDtypeStruct((n, D_MODEL), F32),
                   jax.ShapeDtypeStruct((n, RW_HEADS, HEAD_DIM, HEAD_DIM), F32)],
        scratch_shapes=[pltpu.VMEM((tb, RW_WIDTH), F32) for _ in range(8)],
        compiler_params=_params(("arbitrary",)),
        name="rwkv_step",
    )(prw, prev, state, *consts)


def _zoh(lr, li, dt):
    mag = jnp.exp(lr * dt)
    ar, ai = mag * jnp.cos(li * dt), mag * jnp.sin(li * dt)
    den = lr * lr + li * li
    fr = ((ar - 1.0) * lr + ai * li) / den
    fi = (ai * lr - (ar - 1.0) * li) / den
    return ar, ai, fr, fi


def _dot_split(a, b):
    a1 = a.astype(BF16)
    a2 = (a - a1.astype(F32)).astype(BF16)
    b1 = b.astype(BF16)
    b2 = (b - b1.astype(F32)).astype(BF16)
    d = lambda x, y: jnp.dot(x, y, preferred_element_type=F32)
    return d(a1, b1) + (d(a1, b2) + d(a2, b1))


def _s5_prep_kernel(lrc_ref, lic_ref, lrr_ref, lir_ref, ldt_ref, bret_ref, bimt_ref,
                    brl_ref, bil_ref, crl_ref, cil_ref,
                    m_ref, pre_ref, pim_ref, cre_ref, cim_ref, a16_ref,
                    brt_ref, bit_ref, arr_ref, air_ref):
    L, C, N = S5_CHUNK, SSM_GROUP, SSM_STATE
    lag = lax.broadcasted_iota(jnp.int32, (1, LANES), 1).astype(F32)
    sel_row = lax.broadcasted_iota(jnp.int32, (LANES, L * C), 0)
    sel_lag = lax.shift_right_logical(lax.broadcasted_iota(jnp.int32, (LANES, L * C), 1), 4)
    pick = lambda cond: jnp.where(cond, 1.0, 0.0).astype(BF16)
    r_same, r_next, r_rev = (pick(sel_row == sel_lag), pick(sel_row == sel_lag + 1),
                             pick(sel_row == (L - 1) - sel_lag))

    def spread(e, r):
        h1, h2, h3 = _split3(e)
        d = lambda h: jnp.dot(h, r, preferred_element_type=F32)
        return d(h1) + d(h2) + d(h3)

    lane2 = lax.broadcasted_iota(jnp.int32, (C, L * C), 1)
    zero = jnp.zeros((N, L * C), F32)
    a16r, a16i = [], []
    for k in range(2):
        dt = jnp.exp(ldt_ref[k])
        lrc, lic = lrc_ref[k], lic_ref[k]
        mag = jnp.exp(lrc * dt * lag)
        er, ei = mag * jnp.cos(lic * dt * lag), mag * jnp.sin(lic * dt * lag)
        e0r, e0i = spread(er, r_same), spread(ei, r_same)
        e1r, e1i = spread(er, r_next), spread(ei, r_next)
        e2r, e2i = spread(er, r_rev), spread(ei, r_rev)
        _, _, frc, fic = _zoh(lrc, lic, dt)
        bbr = frc * brl_ref[k] - fic * bil_ref[k]
        bbi = frc * bil_ref[k] + fic * brl_ref[k]
        crl, cil = crl_ref[k], cil_ref[k]
        halves = lambda x: jnp.concatenate([x, zero] if k == 0 else [zero, x], axis=1)
        rows = slice(k * N, (k + 1) * N)
        pre_ref[0, rows, :] = halves(e2r * bbr - e2i * bbi).astype(BF16)
        pim_ref[0, rows, :] = halves(e2r * bbi + e2i * bbr).astype(BF16)
        cre_ref[0, rows, :] = halves(crl * e1r - cil * e1i).astype(BF16)
        cim_ref[0, rows, :] = halves(-(crl * e1i + cil * e1r)).astype(BF16)

        lrr, lir = lrr_ref[k], lir_ref[k]
        arr, air, frr, fir = _zoh(lrr, lir, dt)
        brt = frr * bret_ref[k] - fir * bimt_ref[k]
        bit = frr * bimt_ref[k] + fir * bret_ref[k]
        brt_ref[k], bit_ref[k], arr_ref[k], air_ref[k] = brt, bit, arr, air
        m16 = jnp.exp(lrr * dt * L)
        a16r.append(m16 * jnp.cos(lir * dt * L))
        a16i.append(m16 * jnp.sin(lir * dt * L))
        klt = (_dot_split(brt, crl * e0r - cil * e0i)
               - _dot_split(bit, crl * e0i + cil * e0r))
        for j in range(L):
            blk = klt if j == 0 else jnp.where(lane2 >= j * C, pltpu.roll(klt, j * C, axis=1), 0.0)
            m_ref[k, j * C:(j + 1) * C, :] = blk.astype(BF16)
    a16_ref[0, 0:1, :] = jnp.concatenate(a16r, axis=1)
    a16_ref[0, 1:2, :] = jnp.concatenate(a16i, axis=1)


def _s5_prep(A_re, A_im, log_dt, B_re, B_im, C_re, C_im):
    L, G, N, C = S5_CHUNK, SSM_GROUPS, SSM_STATE, SSM_GROUP
    tr = lambda x: x.transpose(0, 2, 1)
    lanes_l = lambda x: jnp.tile(x, (1, 1, L))
    ins = [A_re[:, :, None], A_im[:, :, None], A_re[:, None, :], A_im[:, None, :],
           log_dt[:, None, None], tr(B_re), tr(B_im),
           lanes_l(B_re), lanes_l(B_im), lanes_l(tr(C_re)), lanes_l(tr(C_im))]
    two = lambda a: pl.BlockSpec((2,) + a.shape[1:], lambda p: (p, 0, 0))
    one = lambda r, c: pl.BlockSpec((1, r, c), lambda p: (p, 0, 0))
    outs = [((G, L * C, L * C), BF16), ((G // 2, 2 * N, 2 * L * C), BF16),
            ((G // 2, 2 * N, 2 * L * C), BF16), ((G // 2, 2 * N, 2 * L * C), BF16),
            ((G // 2, 2 * N, 2 * L * C), BF16), ((G // 2, 2, 2 * N), F32),
            ((G, C, N), F32), ((G, C, N), F32), ((G, 1, N), F32), ((G, 1, N), F32)]
    out_specs = [pl.BlockSpec((2, L * C, L * C), lambda p: (p, 0, 0))] \
        + [one(2 * N, 2 * L * C)] * 4 + [one(2, 2 * N)] \
        + [pl.BlockSpec((2,) + s[1:], lambda p: (p, 0, 0)) for s, _ in outs[6:]]
    res = pl.pallas_call(
        _s5_prep_kernel,
        grid=(G // 2,),
        in_specs=[two(a) for a in ins],
        out_specs=out_specs,
        out_shape=[jax.ShapeDtypeStruct(s, d) for s, d in outs],
        compiler_params=_params(("arbitrary",)),
        name="s5_prep",
    )(*ins)
    names = ('m', 'pre', 'pim', 'cre', 'cim', 'a16', 'brt', 'bit', 'arr', 'air')
    return dict(zip(names, res))


S5_GQ = LANES // SSM_GROUP


def _s5_prompt_kernel(u_ref, m_ref, pre_ref, pim_ref, cre_ref, cim_ref, a16_ref,
                      y_ref, xr_ref, xi_ref,
                      u2_ref, bre_ref, bim_ref, xpr_ref, xpi_ref, *, nb, nch):
    L, C, GQ = S5_CHUNK, SSM_GROUP, S5_GQ
    npair = GQ // 2

    def relayout_in(b, carry):
        uts = [u_ref[b, pl.ds(j, nch, stride=L), :].T for j in range(L)]
        for g in range(GQ):
            for hf in range(2):
                xt = jnp.concatenate([ut[g * C:(g + 1) * C, :]
                                      for ut in uts[hf * L // 2:(hf + 1) * L // 2]], axis=0)
                u2_ref[g, hf, pl.ds(b, nch, stride=nb), :] = xt.T
        return carry

    u2 = lambda g: jnp.concatenate([u2_ref[g, 0], u2_ref[g, 1]], axis=1)

    lax.fori_loop(0, nb, relayout_in, 0)

    for m in range(npair):
        ub2 = jnp.concatenate([u2(2 * m), u2(2 * m + 1)], axis=1).astype(BF16)
        bre_ref[m] = _dot_nt(ub2, pre_ref[m])
        bim_ref[m] = _dot_nt(ub2, pim_ref[m])

    ars = [a16_ref[m, 0:1, :] for m in range(npair)]
    ais = [a16_ref[m, 1:2, :] for m in range(npair)]

    def step(cidx, carry):
        o = pl.multiple_of(cidx * nb, nb)
        out = []
        for m in range(npair):
            xr, xi = carry[2 * m], carry[2 * m + 1]
            xpr_ref[m, pl.ds(o, nb), :] = xr
            xpi_ref[m, pl.ds(o, nb), :] = xi
            out.append(ars[m] * xr - ais[m] * xi + bre_ref[m, pl.ds(o, nb), :])
            out.append(ars[m] * xi + ais[m] * xr + bim_ref[m, pl.ds(o, nb), :])
        return tuple(out)

    z = jnp.zeros((nb, LANES), F32)
    fin = lax.fori_loop(0, nch, step, (z,) * GQ)
    for m in range(npair):
        xr_ref[m] = fin[2 * m]
        xi_ref[m] = fin[2 * m + 1]

    for m in range(npair):
        ys = _dot(xpr_ref[m], cre_ref[m]) + _dot(xpi_ref[m], cim_ref[m])
        for k in range(2):
            g = 2 * m + k
            yg = _dot(u2(g), m_ref[g]) + ys[:, k * L * C:(k + 1) * L * C]
            u2_ref[g, 0] = yg[:, :LANES]
            u2_ref[g, 1] = yg[:, LANES:]

    def relayout_out(b, carry):
        for hf in range(2):
            yts = [u2_ref[g, hf, pl.ds(b, nch, stride=nb), :].T for g in range(GQ)]
            for i in range(L // 2):
                yt = jnp.concatenate([t_[i * C:(i + 1) * C, :] for t_ in yts], axis=0)
                y_ref[b, pl.ds(hf * L // 2 + i, nch, stride=L), :] = yt.T
        return carry

    lax.fori_loop(0, nb, relayout_out, 0)


def _s5_prompt(u, tabs, nb, t):
    L, G, N, C = S5_CHUNK, SSM_GROUPS, SSM_STATE, SSM_GROUP
    nch = t // L
    nc = nch * nb
    nq = SSM_WIDTH // LANES
    npair = S5_GQ // 2
    spec = lambda blk: pl.BlockSpec(blk, lambda q: (q, 0, 0))
    y, xr, xi = pl.pallas_call(
        functools.partial(_s5_prompt_kernel, nb=nb, nch=nch),
        grid=(nq,),
        in_specs=[pl.BlockSpec((nb, t, LANES), lambda q: (0, 0, q), pipeline_mode=pl.Buffered(1)),
                  spec((S5_GQ, L * C, L * C)),
                  spec((npair, 2 * N, 2 * L * C)), spec((npair, 2 * N, 2 * L * C)),
                  spec((npair, 2 * N, 2 * L * C)), spec((npair, 2 * N, 2 * L * C)),
                  spec((npair, 2, 2 * N))],
        out_specs=[pl.BlockSpec((nb, t, LANES), lambda q: (0, 0, q)),
                   spec((npair, nb, 2 * N)), spec((npair, nb, 2 * N))],
        out_shape=[jax.ShapeDtypeStruct((nb, t, SSM_WIDTH), F32),
                   jax.ShapeDtypeStruct((G // 2, nb, 2 * N), F32),
                   jax.ShapeDtypeStruct((G // 2, nb, 2 * N), F32)],
        scratch_shapes=[pltpu.VMEM((S5_GQ, 2, nc, LANES), F32)]
        + [pltpu.VMEM((npair, nc, 2 * N), F32) for _ in range(4)],
        compiler_params=_params(("arbitrary",)),
        name="s5_prompt",
    )(u.reshape(nb, t, SSM_WIDTH), tabs['m'], tabs['pre'], tabs['pim'], tabs['cre'], tabs['cim'],
      tabs['a16'])
    unpair = lambda x: x.reshape(G // 2, nb, 2, N).transpose(1, 0, 2, 3).reshape(nb, G, N)
    return y.reshape(nb * t, SSM_WIDTH), unpair(xr), unpair(xi)


def _s5_step_kernel(u_ref, xr_ref, xi_ref, ar_ref, ai_ref, brt_ref, bit_ref, cr_ref, ci_ref,
                    y_ref, nr_ref, ni_ref, bd_ref):
    G, N, C = SSM_GROUPS, SSM_STATE, SSM_GROUP

    def block_diag(slot, src_ref):
        bd_ref[slot] = jnp.zeros((G * C, G * N), BF16)
        for g in range(G):
            bd_ref[slot, g * C:(g + 1) * C, g * N:(g + 1) * N] = src_ref[g].astype(BF16)
        return bd_ref[slot]

    ub = u_ref[...].astype(BF16)
    xr, xi, ar, ai = xr_ref[...], xi_ref[...], ar_ref[...], ai_ref[...]
    nr = ar * xr - ai * xi + jnp.dot(ub, block_diag(0, brt_ref), preferred_element_type=F32)
    ni = ar * xi + ai * xr + jnp.dot(ub, block_diag(1, bit_ref), preferred_element_type=F32)
    nr_ref[...] = nr
    ni_ref[...] = ni
    y_ref[...] = _dot_nt(nr, block_diag(2, cr_ref)) - _dot_nt(ni, block_diag(3, ci_ref))


def _s5_step(u, st_re, st_im, tabs, C_re, C_im):
    G, N, C = SSM_GROUPS, SSM_STATE, SSM_GROUP
    n = u.shape[0]
    ops = (u, st_re.reshape(n, G * N), st_im.reshape(n, G * N),
           tabs['arr'].reshape(1, G * N), tabs['air'].reshape(1, G * N),
           tabs['brt'], tabs['bit'], C_re, C_im)
    y, nr, ni = pl.pallas_call(
        _s5_step_kernel,
        grid=(1,),
        in_specs=[_full_spec(o.shape) for o in ops],
        out_specs=[_full_spec((n, SSM_WIDTH)), _full_spec((n, G * N)), _full_spec((n, G * N))],
        out_shape=[jax.ShapeDtypeStruct((n, SSM_WIDTH), F32),
                   jax.ShapeDtypeStruct((n, G * N), F32),
                   jax.ShapeDtypeStruct((n, G * N), F32)],
        scratch_shapes=[pltpu.VMEM((4, G * C, G * N), BF16)],
        compiler_params=_params(("arbitrary",)),
        name="s5_step",
    )(*ops)
    return y, nr.reshape(n, G, N), ni.reshape(n, G, N)


def _tail_kernel(*refs, tm, tiles_per_seq, decode, final):
    if decode:
        (x_ref, rw_ref, ys_ref, u_ref, gl_ref, pe_ref, st2_ref, st1_ref,
         dsk_ref, wglu_ref, wout_ref, ln2_ref, wfi_ref, cw_ref, cb_ref, wfo_ref,
         ln3_ref, wpg_ref, wple_ref, fin_ref, y_ref, conv_ref) = refs
    else:
        (x_ref, rw_ref, ys_ref, u_ref, gl_ref, pe_ref,
         dsk_ref, wglu_ref, wout_ref, ln2_ref, wfi_ref, cw_ref, cb_ref, wfo_ref,
         ln3_ref, wpg_ref, wple_ref, fin_ref, y_ref, conv_ref, carry_ref) = refs

    x = x_ref[...]
    z = _gelu(ys_ref[...] + dsk_ref[...] * u_ref[...])
    zz = _dot(z, wglu_ref[...])
    s5 = zz[:, :D_MODEL] * _sigmoid(zz[:, D_MODEL:])
    gl = gl_ref[...]
    merged = _sigmoid(gl[:, :D_MODEL]) * rw_ref[...] + _sigmoid(gl[:, D_MODEL:]) * s5
    x = x + _dot(merged, wout_ref[...])

    h2 = _rmsnorm(x, ln2_ref[...]).astype(BF16)
    a_up = jnp.dot(h2, wfi_ref[:, :D_FF], preferred_element_type=F32)
    b_up = jnp.dot(h2, wfi_ref[:, D_FF:], preferred_element_type=F32)
    cw = cw_ref[...]
    if decode:
        prev2 = st2_ref[...]
        prev1 = st1_ref[...]
        conv_ref[...] = a_up
    else:
        i = pl.program_id(0)

        @pl.when(i % tiles_per_seq == 0)
        def _():
            carry_ref[...] = jnp.zeros_like(carry_ref)

        rows = lax.broadcasted_iota(jnp.int32, (tm, 1), 0)
        c0 = carry_ref[0:1, :]
        c1 = carry_ref[1:2, :]
        prev1 = jnp.where(rows == 0, c1, pltpu.roll(a_up, 1, axis=0))
        prev2 = jnp.where(rows == 0, c0, jnp.where(rows == 1, c1, pltpu.roll(a_up, 2, axis=0)))
        carry_ref[...] = a_up[tm - 2:tm, :]
        conv_ref[0] = a_up[tm - 2:tm, :]
    a_conv = cw[0:1] * prev2 + cw[1:2] * prev1 + cw[2:3] * a_up + cb_ref[...]
    x = x + _dot(_gelu(a_conv) * b_up, wfo_ref[...])

    pg = _sigmoid(_dot(_rmsnorm(x, ln3_ref[...]), wpg_ref[...]))
    x = x + pg * _dot(pe_ref[...], wple_ref[...])
    y_ref[...] = _rmsnorm(x, fin_ref[...]) if final else x


def _tail(x2d, rw, ys, u, gl, pe, st_conv, D_skip, wglu, wout, ln2_g, wfi, conv_w, conv_b, wfo,
          ln3_g, wpg, wple, final_g, *, tm, tiles_per_seq, decode, final):
    n = x2d.shape[0]
    row = lambda w: pl.BlockSpec((tm, w), lambda i: (i, 0))
    consts = [D_skip.reshape(1, SSM_WIDTH), wglu, wout, ln2_g.reshape(1, D_MODEL), wfi,
              conv_w, conv_b.reshape(1, D_FF), wfo, ln3_g.reshape(1, D_MODEL), wpg, wple,
              final_g.reshape(1, D_MODEL)]
    acts = [x2d, rw, ys, u, gl, pe]
    in_specs = [row(a.shape[1]) for a in acts]
    scratch = []
    if decode:
        acts += [st_conv[:, 0], st_conv[:, 1]]
        in_specs += [row(D_FF), row(D_FF)]
        conv_spec = row(D_FF)
        conv_shape = (n, D_FF)
    else:
        nseq = n // (tm * tiles_per_seq)
        conv_spec = pl.BlockSpec((1, 2, D_FF), lambda i: (i // tiles_per_seq, 0, 0))
        conv_shape = (nseq, 2, D_FF)
        scratch = [pltpu.VMEM((2, D_FF), F32)]
    in_specs += [pl.BlockSpec(c.shape, lambda i, nd=c.ndim: (0,) * nd,
                              pipeline_mode=pl.Buffered(1)) for c in consts]
    y, conv = pl.pallas_call(
        functools.partial(_tail_kernel, tm=tm, tiles_per_seq=tiles_per_seq, decode=decode,
                          final=final),
        grid=(n // tm,),
        in_specs=in_specs,
        out_specs=[row(D_MODEL), conv_spec],
        out_shape=[jax.ShapeDtypeStruct((n, D_MODEL), F32),
                   jax.ShapeDtypeStruct(conv_shape, F32)],
        scratch_shapes=scratch,
        compiler_params=_params(("arbitrary",)),
        name="tail_decode" if decode else "tail_prompt",
    )(*acts, *consts)
    if decode:
        conv = jnp.stack([st_conv[:, 1], conv], axis=1)
    return y, conv


def _layer(x, pe, states, W, *, decode, final, tm_proj, tt, tm_tail, tb):
    nb, t, _ = x.shape
    n = nb * t
    x2d = x.reshape(n, D_MODEL)
    prw, u, gl = _proj(x2d, W['ln1_g'], W['w_in'], tm_proj)
    if decode:
        st_shift, st_wkv, st_re, st_im, st_conv = states
        rw, s_new = _rwkv_step(prw, st_shift, st_wkv, W['rowp'], W['wla'], W['g2'], W['eones'],
                               W['w_rw_out'], tb)
        ys, xr, xi = _s5_step(u, st_re, st_im, W['s5'], W['C_re'], W['C_im'])
        new_shift = prw
    else:
        st_conv = None
        rw, s_new = _rwkv_prompt(prw, W['rowp'], W['wla'], W['g2'], W['eones'], W['w_rw_out'],
                                 nb, t, tt)
        ys, xr, xi = _s5_prompt(u, W['s5'], nb, t)
        new_shift = prw.reshape(nb, t, SHIFT_WIDTH)[:, -1]
    y, new_conv = _tail(x2d, rw, ys, u, gl, pe.reshape(n, PLE_DIM), st_conv, W['D_skip'],
                        W['w_glu'], W['w_out'], W['ln2_g'], W['w_ffn_in'], W['conv_w'],
                        W['conv_b'], W['w_ffn_out'], W['ln3_g'], W['w_ple_gate'], W['w_ple'],
                        W['final_g'], tm=tm_tail, tiles_per_seq=max(t // tm_tail, 1),
                        decode=decode, final=final)
    return y.reshape(nb, t, D_MODEL), (new_shift, s_new, xr, xi, new_conv)


def kernel(x_prompt, x_sample, p_prompt, p_sample, state_shift, state_wkv, state_ssm_re, state_ssm_im, state_conv, ln1_g, w_in, mu_shift, w0, w2, a0, a2, g2, k_k, k_a, r_k, lnx_g, lnx_b, w_rw_out, A_re, A_im, log_dt, B_re, B_im, C_re, C_im, D_skip, w_glu, w_out, ln2_g, w_ffn_in, conv_w, conv_b, w_ffn_out, ln3_g, w_ple_gate, w_ple, final_g):
    depth = w_in.shape[0]
    xp, xs = x_prompt, x_sample
    pst = [[] for _ in range(5)]
    sst = [[] for _ in range(5)]
    for i in range(depth):
        wla, g2b, wo = _rwkv_weights(w2[i], a2[i], g2[i], w_rw_out[i])
        W = dict(
            ln1_g=ln1_g[i], w_in=w_in[i].astype(BF16),
            rowp=_row_params(mu_shift[i], w0[i], a0[i], k_k[i], k_a[i], r_k[i].reshape(-1),
                             lnx_g[i], lnx_b[i]),
            wla=wla, g2=g2b, eones=_rwkv_consts(), w_rw_out=wo,
            s5=_s5_prep(A_re[i], A_im[i], log_dt[i], B_re[i], B_im[i], C_re[i], C_im[i]),
            C_re=C_re[i], C_im=C_im[i], D_skip=D_skip[i],
            w_glu=w_glu[i].astype(BF16), w_out=w_out[i].astype(BF16), ln2_g=ln2_g[i],
            w_ffn_in=w_ffn_in[i].astype(BF16), conv_w=conv_w[i], conv_b=conv_b[i],
            w_ffn_out=w_ffn_out[i].astype(BF16), ln3_g=ln3_g[i],
            w_ple_gate=w_ple_gate[i].astype(BF16), w_ple=w_ple[i].astype(BF16), final_g=final_g)
        final = i == depth - 1
        tp = xp.shape[1]
        xp, sp = _layer(xp, p_prompt[i], None, W, decode=False, final=final,
                        tm_proj=min(TM_PROJ, tp), tt=min(TT_RWKV, tp), tm_tail=min(TM_TAIL, tp),
                        tb=None)
        ns = xs.shape[0]
        xs, ss = _layer(xs, p_sample[i],
                        (state_shift[i], state_wkv[i], state_ssm_re[i], state_ssm_im[i],
                         state_conv[i]),
                        W, decode=True, final=final, tm_proj=ns, tt=None, tm_tail=ns,
                        tb=min(TB_STEP, ns))
        for j in range(5):
            pst[j].append(sp[j])
            sst[j].append(ss[j])
    return (xp, xs,
            jnp.stack(pst[0]), jnp.stack(pst[1]), jnp.stack(pst[2]), jnp.stack(pst[3]),
            jnp.stack(pst[4]),
            jnp.stack(sst[0]), jnp.stack(sst[1]), jnp.stack(sst[2]), jnp.stack(sst[3]),
            jnp.stack(sst[4]))
```

```python
import functools
import math

import jax
import jax.numpy as jnp
from jax import lax
from jax.experimental import pallas as pl
from jax.experimental.pallas import tpu as pltpu

F32 = jnp.float32
BF16 = jnp.bfloat16

D_MODEL = 1024
RW_HEADS = 8
HEAD_DIM = 64
RW_WIDTH = RW_HEADS * HEAD_DIM
LORA_W = 64
LORA_A = 64
LORA_G = 128
SHIFT_WIDTH = 3 * RW_WIDTH + LORA_W + LORA_A + LORA_G
SSM_WIDTH = 512
SSM_GROUP = 16
SSM_GROUPS = SSM_WIDTH // SSM_GROUP
SSM_STATE = 64
GATE_OFF = SHIFT_WIDTH + SSM_WIDTH
IN_WIDTH = SHIFT_WIDTH + SSM_WIDTH + 2 * D_MODEL
D_FF = 2816
PLE_DIM = 256
EPS = 1e-6
GN_EPS = 64e-5

LANES = 128
WKV_CHUNK = 64
S5_CHUNK = 16
N_PAIRS = RW_HEADS // 2
VMEM_LIMIT = 56 * 1024 * 1024
TM_PROJ = 512
TT_RWKV = 256
TM_TAIL = 256
TB_STEP = 16
TAIL_ROW_BLOCKS = 2


def _dot(a, b):
    return jnp.dot(a.astype(BF16), b.astype(BF16), preferred_element_type=F32)


def _dot_nt(a, b):
    return lax.dot_general(a.astype(BF16), b.astype(BF16), (((1,), (1,)), ((), ())),
                           preferred_element_type=F32)


def _dot_tn(a, b):
    return lax.dot_general(a.astype(BF16), b.astype(BF16), (((0,), (0,)), ((), ())),
                           preferred_element_type=F32)


def _split3(x):
    h1 = x.astype(BF16)
    r1 = x - h1.astype(F32)
    h2 = r1.astype(BF16)
    h3 = (r1 - h2.astype(F32)).astype(BF16)
    return h1, h2, h3


def _dot_exact_lhs(a_bf16, x):
    h1, h2, h3 = _split3(x)
    d = lambda h: jnp.dot(a_bf16, h, preferred_element_type=F32)
    return d(h1) + d(h2) + d(h3)


def _sigmoid(x):
    return 1.0 / (1.0 + jnp.exp(-x))


def _gelu(x):
    c = math.sqrt(2.0 / math.pi)
    return 0.5 * x * (1.0 + jnp.tanh(c * (x + 0.044715 * (x * x * x))))


def _rmsnorm(x, g):
    return x * lax.rsqrt(jnp.mean(x * x, axis=-1, keepdims=True) + EPS) * g


def _full_spec(shape):
    nd = len(shape)
    return pl.BlockSpec(shape, lambda *_: (0,) * nd)


def _params(sem):
    return pltpu.CompilerParams(dimension_semantics=sem, vmem_limit_bytes=VMEM_LIMIT)


def _proj_kernel(x_ref, g_ref, w_ref, prw_ref, u_ref, gl_ref):
    h = _rmsnorm(x_ref[...], g_ref[...]).astype(BF16)
    prw_ref[...] = jnp.dot(h, w_ref[:, :SHIFT_WIDTH], preferred_element_type=F32)
    u_ref[...] = jnp.dot(h, w_ref[:, SHIFT_WIDTH:GATE_OFF], preferred_element_type=F32)
    gl_ref[...] = jnp.dot(h, w_ref[:, GATE_OFF:], preferred_element_type=F32)


def _proj(x2d, ln1_g, w_in_bf, tm):
    n = x2d.shape[0]
    row = lambda w: pl.BlockSpec((tm, w), lambda i: (i, 0))
    return pl.pallas_call(
        _proj_kernel,
        grid=(n // tm,),
        in_specs=[row(D_MODEL), _full_spec((1, D_MODEL)), _full_spec((D_MODEL, IN_WIDTH))],
        out_specs=[row(SHIFT_WIDTH), row(SSM_WIDTH), row(2 * D_MODEL)],
        out_shape=[jax.ShapeDtypeStruct((n, SHIFT_WIDTH), F32),
                   jax.ShapeDtypeStruct((n, SSM_WIDTH), F32),
                   jax.ShapeDtypeStruct((n, 2 * D_MODEL), F32)],
        compiler_params=_params(("arbitrary",)),
        name="proj",
    )(x2d, ln1_g.reshape(1, D_MODEL), w_in_bf)


def _rwkv_token_math(p, prev, mu, w0, a0, k_k, k_a, r_k, wla, g2, eones):
    xs = p + (prev - p) * mu
    r = xs[:, 0:RW_WIDTH]
    k = xs[:, RW_WIDTH:2 * RW_WIDTH]
    v = xs[:, 2 * RW_WIDTH:3 * RW_WIDTH]
    xwa = xs[:, 3 * RW_WIDTH:3 * RW_WIDTH + LORA_W + LORA_A]
    xg = xs[:, 3 * RW_WIDTH + LORA_W + LORA_A:]
    lane = lax.broadcasted_iota(jnp.int32, xwa.shape, 1)
    la = _dot(jnp.where(lane < LORA_W, jnp.tanh(xwa), xwa), wla)
    wpre = w0 + la[:, :RW_WIDTH]
    softplus = jnp.maximum(-wpre, 0.0) + jnp.log(1.0 + jnp.exp(-jnp.abs(wpre)))
    lw = -jnp.exp(-softplus - 0.5)
    a = _sigmoid(a0 + la[:, RW_WIDTH:])
    g = _dot(_sigmoid(xg), g2)
    kkr = k * k_k
    ss = _dot(kkr * kkr, eones)
    kk = kkr / jnp.maximum(jnp.sqrt(ss), 1e-12)
    kp = k * (1.0 + (a - 1.0) * k_a)
    bonus = _dot(r * kp * r_k, eones) * v
    return r, v, lw, a, g, kk, kp, bonus


def _rwkv_output(y, bonus, g, lnx_g, lnx_b, eones, w_o):
    emean = eones * (1.0 / HEAD_DIM)
    mean = _dot(y, emean)
    d = y - mean
    var = _dot(d * d, emean)
    yn = d * lax.rsqrt(var + GN_EPS) * lnx_g + lnx_b
    return _dot((yn + bonus) * g, w_o)


def _head_masks(shape):
    lane = lax.broadcasted_iota(jnp.int32, shape, 1)
    return lane < HEAD_DIM, lane >= HEAD_DIM


def _stack_heads(x, m_a, m_b):
    return jnp.concatenate([jnp.where(m_a, x, 0.0), jnp.where(m_b, x, 0.0)], axis=0)


def _rwkv_prompt_kernel(p_ref, mu_ref, w0_ref, a0_ref, kk_ref, ka_ref, rk_ref, lg_ref, lb_ref,
                        wla_ref, g2_ref, e_ref, ltri_ref, wo_ref,
                        out_ref, sout_ref,
                        carry_ref, sp_ref, y_ref, *, tt):
    i = pl.program_id(1)

    @pl.when(i == 0)
    def _():
        carry_ref[...] = jnp.zeros_like(carry_ref)
        sp_ref[...] = jnp.zeros_like(sp_ref)

    p = p_ref[...]
    rows = lax.broadcasted_iota(jnp.int32, (tt, 1), 0)
    prev = jnp.where(rows == 0, carry_ref[...], pltpu.roll(p, 1, axis=0))
    carry_ref[...] = p[tt - 1:tt, :]
    eones = e_ref[...]
    r, v, lw, a, g, kk, kp, bonus = _rwkv_token_math(
        p, prev, mu_ref[...], w0_ref[...], a0_ref[...], kk_ref[...], ka_ref[...], rk_ref[...],
        wla_ref[...], g2_ref[...], eones)

    c = _dot_exact_lhs(ltri_ref[...], lw)
    at = -kk * jnp.exp(c - lw)
    rt = r * jnp.exp(c)
    einv = jnp.exp(-c)
    bvec = kk * a
    bt = bvec * einv
    kt = kp * einv

    m_a, m_b = _head_masks((WKV_CHUNK, LANES))
    ri = lax.broadcasted_iota(jnp.int32, (2 * WKV_CHUNK, 2 * WKV_CHUNK), 0)
    ci = lax.broadcasted_iota(jnp.int32, (2 * WKV_CHUNK, 2 * WKV_CHUNK), 1)
    same_head = (ri >= WKV_CHUNK) == (ci >= WKV_CHUNK)
    strict = same_head & (ri > ci)
    incl = same_head & (ri >= ci)

    nch = tt // WKV_CHUNK
    blocks = [(ch, j) for ch in range(nch) for j in range(N_PAIRS)]
    rows_of = lambda ch: slice(ch * WKV_CHUNK, (ch + 1) * WKV_CHUNK)
    lanes_of = lambda j: slice(j * LANES, (j + 1) * LANES)
    blk = lambda x, ch, j: x[rows_of(ch), lanes_of(j)]
    stack = lambda x, ch, j: _stack_heads(blk(x, ch, j), m_a, m_b)
    vss = [stack(v, ch, j) for ch, j in blocks]
    gms = [_dot_nt(jnp.concatenate([stack(at, ch, j), stack(rt, ch, j)], axis=0),
                   jnp.concatenate([stack(bt, ch, j), stack(kt, ch, j)], axis=0))
           for ch, j in blocks]
    n_abs = [jnp.where(strict, gm[:LANES, :LANES], 0.0) for gm in gms]
    wvs = [_dot(jnp.where(strict, gm[:LANES, LANES:], 0.0), vs) for gm, vs in zip(gms, vss)]
    a_ys = [jnp.concatenate([jnp.where(incl, gm[LANES:, :LANES], 0.0),
                             jnp.where(incl, gm[LANES:, LANES:], 0.0)], axis=1) for gm in gms]
    eye = (ri == ci).astype(F32)
    tinvs = [eye + n for n in n_abs]
    pows = n_abs
    for _ in range(5):
        pows = [_dot(pw, pw) for pw in pows]
        tinvs = [t + _dot(pw, t) for pw, t in zip(pows, tinvs)]

    sps = [sp_ref[j] for j in range(N_PAIRS)]
    for ch in range(nch):
        c_end = c[(ch + 1) * WKV_CHUNK - 1:(ch + 1) * WKV_CHUNK, :]
        g_end = jnp.exp(c_end)
        e_rest = jnp.exp(c_end - c[rows_of(ch), :])
        bhat = bvec[rows_of(ch), :] * e_rest
        khat = kp[rows_of(ch), :] * e_rest
        idx = [ch * N_PAIRS + j for j in range(N_PAIRS)]
        arss = [_dot_nt(jnp.concatenate([blk(at, ch, j), blk(rt, ch, j)], axis=0), sps[j])
                for j in range(N_PAIRS)]
        uss = [_dot(tinvs[idx[j]], _stack_heads(arss[j][:WKV_CHUNK], m_a, m_b) + wvs[idx[j]])
               for j in range(N_PAIRS)]
        uvs = [jnp.concatenate([uss[j], vss[idx[j]]], axis=0) for j in range(N_PAIRS)]
        yss = [_dot(a_ys[idx[j]], uvs[j]) for j in range(N_PAIRS)]
        incs = [_dot_tn(uvs[j], jnp.concatenate(
            [_stack_heads(bhat[:, lanes_of(j)], m_a, m_b),
             _stack_heads(khat[:, lanes_of(j)], m_a, m_b)], axis=0)) for j in range(N_PAIRS)]
        for j in range(N_PAIRS):
            y_ref[rows_of(ch), lanes_of(j)] = (yss[j][:WKV_CHUNK] + yss[j][WKV_CHUNK:]
                                               + arss[j][WKV_CHUNK:])
            sps[j] = sps[j] * g_end[:, lanes_of(j)] + incs[j]
    for j in range(N_PAIRS):
        sp_ref[j] = sps[j]

    out_ref[...] = _rwkv_output(y_ref[...], bonus, g, lg_ref[...], lb_ref[...], eones, wo_ref[...])

    @pl.when(i == pl.num_programs(1) - 1)
    def _():
        for j in range(N_PAIRS):
            sp = sp_ref[j]
            sout_ref[0, 2 * j] = sp[:HEAD_DIM, :HEAD_DIM]
            sout_ref[0, 2 * j + 1] = sp[HEAD_DIM:, HEAD_DIM:]


def _rwkv_consts():
    idx = jnp.arange(RW_WIDTH)
    eones = (idx[:, None] // HEAD_DIM == idx[None, :] // HEAD_DIM).astype(BF16)
    return eones


def _rwkv_weights(w2, a2, g2, w_rw_out):
    wla = jnp.zeros((LORA_W + LORA_A, 2 * RW_WIDTH), F32)
    wla = wla.at[:LORA_W, :RW_WIDTH].set(w2).at[LORA_W:, RW_WIDTH:].set(a2)
    return wla.astype(BF16), g2.astype(BF16), w_rw_out.astype(BF16)


def _row_params(mu_shift, w0, a0, k_k, k_a, r_k, lnx_g, lnx_b):
    return (mu_shift.reshape(1, SHIFT_WIDTH), w0.reshape(1, RW_WIDTH), a0.reshape(1, RW_WIDTH),
            k_k.reshape(1, RW_WIDTH), k_a.reshape(1, RW_WIDTH), r_k.reshape(1, RW_WIDTH),
            lnx_g.reshape(1, RW_WIDTH), lnx_b.reshape(1, RW_WIDTH))


def _rwkv_prompt(prw, rowp, wla, g2b, eones, wo, nb, t, tt):
    nt = t // tt
    idx = jnp.arange(tt)
    ltri = ((idx[:, None] // WKV_CHUNK == idx[None, :] // WKV_CHUNK)
            & (idx[:, None] >= idx[None, :])).astype(BF16)
    consts = list(rowp) + [wla, g2b, eones, ltri, wo]
    return pl.pallas_call(
        functools.partial(_rwkv_prompt_kernel, tt=tt),
        grid=(nb, nt),
        in_specs=[pl.BlockSpec((tt, SHIFT_WIDTH), lambda b, i: (b * nt + i, 0))]
        + [_full_spec(c.shape) for c in consts],
        out_specs=[pl.BlockSpec((tt, D_MODEL), lambda b, i: (b * nt + i, 0)),
                   pl.BlockSpec((1, RW_HEADS, HEAD_DIM, HEAD_DIM), lambda b, i: (b, 0, 0, 0))],
        out_shape=[jax.ShapeDtypeStruct((nb * t, D_MODEL), F32),
                   jax.ShapeDtypeStruct((nb, RW_HEADS, HEAD_DIM, HEAD_DIM), F32)],
        scratch_shapes=[pltpu.VMEM((1, SHIFT_WIDTH), F32),
                        pltpu.VMEM((N_PAIRS, LANES, LANES), F32),
                        pltpu.VMEM((tt, RW_WIDTH), F32)],
        compiler_params=_params(("arbitrary", "arbitrary")),
        name="rwkv_prompt",
    )(prw, *consts)


def _rwkv_step_kernel(p_ref, prev_ref, s_ref, mu_ref, w0_ref, a0_ref, kk_ref, ka_ref, rk_ref,
                      lg_ref, lb_ref, wla_ref, g2_ref, e_ref, wo_ref,
                      out_ref, sout_ref,
                      q_ref, wr_ref, bv_ref, kp_ref, v_ref, w_ref, sa_ref, y1_ref, *, tb):
    eones = e_ref[...]
    r, v, lw, a, g, kk, kp, bonus = _rwkv_token_math(
        p_ref[...], prev_ref[...], mu_ref[...], w0_ref[...], a0_ref[...], kk_ref[...],
        ka_ref[...], rk_ref[...], wla_ref[...], g2_ref[...], eones)
    w = jnp.exp(lw)
    bvec = kk * a
    q_ref[...] = -kk
    wr_ref[...] = w * r
    bv_ref[...] = bvec
    kp_ref[...] = kp
    v_ref[...] = v
    w_ref[...] = w

    ri = lax.broadcasted_iota(jnp.int32, (HEAD_DIM, HEAD_DIM), 0)
    ci = lax.broadcasted_iota(jnp.int32, (HEAD_DIM, HEAD_DIM), 1)
    eye = ri == ci
    zpad = jnp.zeros((6, HEAD_DIM), F32)

    def body(b, carry):
        q_row, wr_row, bv_row, kp_row, v_row, w_row = (
            ref[pl.ds(b, 1), :] for ref in (q_ref, wr_ref, bv_ref, kp_ref, v_ref, w_ref))
        hsl = [slice(h * HEAD_DIM, (h + 1) * HEAD_DIM) for h in range(RW_HEADS)]
        ss = [s_ref[b, h] for h in range(RW_HEADS)]
        r2s = [_dot_nt(jnp.concatenate([q_row[:, hs], wr_row[:, hs], zpad], axis=0), s)
               for hs, s in zip(hsl, ss)]
        incs = []
        for hs, r2 in zip(hsl, r2s):
            dsa = jnp.where(eye, jnp.broadcast_to(r2[0:1], (HEAD_DIM, HEAD_DIM)), 0.0)
            dv = jnp.where(eye, jnp.broadcast_to(v_row[:, hs], (HEAD_DIM, HEAD_DIM)), 0.0)
            bb = jnp.broadcast_to(bv_row[:, hs], (HEAD_DIM, HEAD_DIM))
            kb = jnp.broadcast_to(kp_row[:, hs], (HEAD_DIM, HEAD_DIM))
            lhs = jnp.concatenate([dsa, dv], axis=1)
            rhs = jnp.concatenate([bb, kb], axis=0)
            l1 = lhs.astype(BF16)
            l2 = (lhs - l1.astype(F32)).astype(BF16)
            r1 = rhs.astype(BF16)
            r2_ = (rhs - r1.astype(F32)).astype(BF16)
            d = lambda x, y: jnp.dot(x, y, preferred_element_type=F32)
            incs.append(d(l1, r1) + (d(l1, r2_) + d(l2, r1)))
        for h in range(RW_HEADS):
            sout_ref[b, h] = ss[h] * w_row[:, hsl[h]] + incs[h]
        sa_ref[pl.ds(b, 1), :] = jnp.concatenate([r2[0:1] for r2 in r2s], axis=1)
        y1_ref[pl.ds(b, 1), :] = jnp.concatenate([r2[1:2] for r2 in r2s], axis=1)
        return carry

    lax.fori_loop(0, tb, body, 0)

    sa = sa_ref[...]
    y = y1_ref[...] + sa * _dot(bvec * r, eones) + v * _dot(kp * r, eones)
    out_ref[...] = _rwkv_output(y, bonus, g, lg_ref[...], lb_ref[...], eones, wo_ref[...])


def _rwkv_step(prw, prev, state, rowp, wla, g2b, eones, wo, tb):
    n = prw.shape[0]
    consts = list(rowp) + [wla, g2b, eones, wo]
    rowspec = lambda w: pl.BlockSpec((tb, w), lambda i: (i, 0))
    sspec = pl.BlockSpec((tb, RW_HEADS, HEAD_DIM, HEAD_DIM), lambda i: (i, 0, 0, 0))
    return pl.pallas_call(
        functools.partial(_rwkv_step_kernel, tb=tb),
        grid=(n // tb,),
        in_specs=[rowspec(SHIFT_WIDTH), rowspec(SHIFT_WIDTH), sspec]
        + [_full_spec(c.shape) for c in consts],
        out_specs=[rowspec(D_MODEL), sspec],
        out_shape=[jax.ShapeDtypeStruct((n, D_MODEL), F32),
                   jax.ShapeDtypeStruct((n, RW_HEADS, HEAD_DIM, HEAD_DIM), F32)],
        scratch_shapes=[pltpu.VMEM((tb, RW_WIDTH), F32) for _ in range(8)],
        compiler_params=_params(("arbitrary",)),
        name="rwkv_step",
    )(prw, prev, state, *consts)


def _zoh(lr, li, dt):
    mag = jnp.exp(lr * dt)
    ar, ai = mag * jnp.cos(li * dt), mag * jnp.sin(li * dt)
    den = lr * lr + li * li
    fr = ((ar - 1.0) * lr + ai * li) / den
    fi = (ai * lr - (ar - 1.0) * li) / den
    return ar, ai, fr, fi


def _dot_split(a, b):
    a1 = a.astype(BF16)
    a2 = (a - a1.astype(F32)).astype(BF16)
    b1 = b.astype(BF16)
    b2 = (b - b1.astype(F32)).astype(BF16)
    d = lambda x, y: jnp.dot(x, y, preferred_element_type=F32)
    return d(a1, b1) + (d(a1, b2) + d(a2, b1))


def _s5_prep_kernel(lrc_ref, lic_ref, lrr_ref, lir_ref, ldt_ref, bret_ref, bimt_ref,
                    brl_ref, bil_ref, crl_ref, cil_ref,
                    m_ref, pre_ref, pim_ref, cre_ref, cim_ref, a16_ref,
                    brt_ref, bit_ref, arr_ref, air_ref):
    L, C, N = S5_CHUNK, SSM_GROUP, SSM_STATE
    lag = lax.broadcasted_iota(jnp.int32, (1, LANES), 1).astype(F32)
    sel_row = lax.broadcasted_iota(jnp.int32, (LANES, L * C), 0)
    sel_lag = lax.shift_right_logical(lax.broadcasted_iota(jnp.int32, (LANES, L * C), 1), 4)
    pick = lambda cond: jnp.where(cond, 1.0, 0.0).astype(BF16)
    r_same, r_next, r_rev = (pick(sel_row == sel_lag), pick(sel_row == sel_lag + 1),
                             pick(sel_row == (L - 1) - sel_lag))

    def spread(e, r):
        h1, h2, h3 = _split3(e)
        d = lambda h: jnp.dot(h, r, preferred_element_type=F32)
        return d(h1) + d(h2) + d(h3)

    lane2 = lax.broadcasted_iota(jnp.int32, (C, L * C), 1)
    zero = jnp.zeros((N, L * C), F32)
    a16r, a16i = [], []
    for k in range(2):
        dt = jnp.exp(ldt_ref[k])
        lrc, lic = lrc_ref[k], lic_ref[k]
        mag = jnp.exp(lrc * dt * lag)
        er, ei = mag * jnp.cos(lic * dt * lag), mag * jnp.sin(lic * dt * lag)
        e0r, e0i = spread(er, r_same), spread(ei, r_same)
        e1r, e1i = spread(er, r_next), spread(ei, r_next)
        e2r, e2i = spread(er, r_rev), spread(ei, r_rev)
        _, _, frc, fic = _zoh(lrc, lic, dt)
        bbr = frc * brl_ref[k] - fic * bil_ref[k]
        bbi = frc * bil_ref[k] + fic * brl_ref[k]
        crl, cil = crl_ref[k], cil_ref[k]
        halves = lambda x: jnp.concatenate([x, zero] if k == 0 else [zero, x], axis=1)
        rows = slice(k * N, (k + 1) * N)
        pre_ref[0, rows, :] = halves(e2r * bbr - e2i * bbi).astype(BF16)
        pim_ref[0, rows, :] = halves(e2r * bbi + e2i * bbr).astype(BF16)
        cre_ref[0, rows, :] = halves(crl * e1r - cil * e1i).astype(BF16)
        cim_ref[0, rows, :] = halves(-(crl * e1i + cil * e1r)).astype(BF16)

        lrr, lir = lrr_ref[k], lir_ref[k]
        arr, air, frr, fir = _zoh(lrr, lir, dt)
        brt = frr * bret_ref[k] - fir * bimt_ref[k]
        bit = frr * bimt_ref[k] + fir * bret_ref[k]
        brt_ref[k], bit_ref[k], arr_ref[k], air_ref[k] = brt, bit, arr, air
        m16 = jnp.exp(lrr * dt * L)
        a16r.append(m16 * jnp.cos(lir * dt * L))
        a16i.append(m16 * jnp.sin(lir * dt * L))
        klt = (_dot_split(brt, crl * e0r - cil * e0i)
               - _dot_split(bit, crl * e0i + cil * e0r))
        for j in range(L):
            blk = klt if j == 0 else jnp.where(lane2 >= j * C, pltpu.roll(klt, j * C, axis=1), 0.0)
            m_ref[k, j * C:(j + 1) * C, :] = blk.astype(BF16)
    a16_ref[0, 0:1, :] = jnp.concatenate(a16r, axis=1)
    a16_ref[0, 1:2, :] = jnp.concatenate(a16i, axis=1)


def _s5_prep(A_re, A_im, log_dt, B_re, B_im, C_re, C_im):
    L, G, N, C = S5_CHUNK, SSM_GROUPS, SSM_STATE, SSM_GROUP
    tr = lambda x: x.transpose(0, 2, 1)
    lanes_l = lambda x: jnp.tile(x, (1, 1, L))
    ins = [A_re[:, :, None], A_im[:, :, None], A_re[:, None, :], A_im[:, None, :],
           log_dt[:, None, None], tr(B_re), tr(B_im),
           lanes_l(B_re), lanes_l(B_im), lanes_l(tr(C_re)), lanes_l(tr(C_im))]
    two = lambda a: pl.BlockSpec((2,) + a.shape[1:], lambda p: (p, 0, 0))
    one = lambda r, c: pl.BlockSpec((1, r, c), lambda p: (p, 0, 0))
    outs = [((G, L * C, L * C), BF16), ((G // 2, 2 * N, 2 * L * C), BF16),
            ((G // 2, 2 * N, 2 * L * C), BF16), ((G // 2, 2 * N, 2 * L * C), BF16),
            ((G // 2, 2 * N, 2 * L * C), BF16), ((G // 2, 2, 2 * N), F32),
            ((G, C, N), F32), ((G, C, N), F32), ((G, 1, N), F32), ((G, 1, N), F32)]
    out_specs = [pl.BlockSpec((2, L * C, L * C), lambda p: (p, 0, 0))] \
        + [one(2 * N, 2 * L * C)] * 4 + [one(2, 2 * N)] \
        + [pl.BlockSpec((2,) + s[1:], lambda p: (p, 0, 0)) for s, _ in outs[6:]]
    res = pl.pallas_call(
        _s5_prep_kernel,
        grid=(G // 2,),
        in_specs=[two(a) for a in ins],
        out_specs=out_specs,
        out_shape=[jax.ShapeDtypeStruct(s, d) for s, d in outs],
        compiler_params=_params(("arbitrary",)),
        name="s5_prep",
    )(*ins)
    names = ('m', 'pre', 'pim', 'cre', 'cim', 'a16', 'brt', 'bit', 'arr', 'air')
    return dict(zip(names, res))


S5_GQ = LANES // SSM_GROUP


def _s5_prompt_kernel(u_ref, m_ref, pre_ref, pim_ref, cre_ref, cim_ref, a16_ref,
                      y_ref, xr_ref, xi_ref,
                      u2_ref, bre_ref, bim_ref, xpr_ref, xpi_ref, *, nb, nch):
    L, C, GQ = S5_CHUNK, SSM_GROUP, S5_GQ
    npair = GQ // 2

    def relayout_in(b, carry):
        uts = [u_ref[b, pl.ds(j, nch, stride=L), :].T for j in range(L)]
        for g in range(GQ):
            for hf in range(2):
                xt = jnp.concatenate([ut[g * C:(g + 1) * C, :]
                                      for ut in uts[hf * L // 2:(hf + 1) * L // 2]], axis=0)
                u2_ref[g, hf, pl.ds(b, nch, stride=nb), :] = xt.T
        return carry

    u2 = lambda g: jnp.concatenate([u2_ref[g, 0], u2_ref[g, 1]], axis=1)

    lax.fori_loop(0, nb, relayout_in, 0)

    for m in range(npair):
        ub2 = jnp.concatenate([u2(2 * m), u2(2 * m + 1)], axis=1).astype(BF16)
        bre_ref[m] = _dot_nt(ub2, pre_ref[m])
        bim_ref[m] = _dot_nt(ub2, pim_ref[m])

    ars = [a16_ref[m, 0:1, :] for m in range(npair)]
    ais = [a16_ref[m, 1:2, :] for m in range(npair)]

    def step(cidx, carry):
        o = pl.multiple_of(cidx * nb, nb)
        out = []
        for m in range(npair):
            xr, xi = carry[2 * m], carry[2 * m + 1]
            xpr_ref[m, pl.ds(o, nb), :] = xr
            xpi_ref[m, pl.ds(o, nb), :] = xi
            out.append(ars[m] * xr - ais[m] * xi + bre_ref[m, pl.ds(o, nb), :])
            out.append(ars[m] * xi + ais[m] * xr + bim_ref[m, pl.ds(o, nb), :])
        return tuple(out)

    z = jnp.zeros((nb, LANES), F32)
    fin = lax.fori_loop(0, nch, step, (z,) * GQ)
    for m in range(npair):
        xr_ref[m] = fin[2 * m]
        xi_ref[m] = fin[2 * m + 1]

    for m in range(npair):
        ys = _dot(xpr_ref[m], cre_ref[m]) + _dot(xpi_ref[m], cim_ref[m])
        for k in range(2):
            g = 2 * m + k
            yg = _dot(u2(g), m_ref[g]) + ys[:, k * L * C:(k + 1) * L * C]
            u2_ref[g, 0] = yg[:, :LANES]
            u2_ref[g, 1] = yg[:, LANES:]

    def relayout_out(b, carry):
        for hf in range(2):
            yts = [u2_ref[g, hf, pl.ds(b, nch, stride=nb), :].T for g in range(GQ)]
            for i in range(L // 2):
                yt = jnp.concatenate([t_[i * C:(i + 1) * C, :] for t_ in yts], axis=0)
                y_ref[b, pl.ds(hf * L // 2 + i, nch, stride=L), :] = yt.T
        return carry

    lax.fori_loop(0, nb, relayout_out, 0)


def _s5_prompt(u, tabs, nb, t):
    L, G, N, C = S5_CHUNK, SSM_GROUPS, SSM_STATE, SSM_GROUP
    nch = t // L
    nc = nch * nb
    nq = SSM_WIDTH // LANES
    npair = S5_GQ // 2
    spec = lambda blk: pl.BlockSpec(blk, lambda q: (q, 0, 0))
    y, xr, xi = pl.pallas_call(
        functools.partial(_s5_prompt_kernel, nb=nb, nch=nch),
        grid=(nq,),
        in_specs=[pl.BlockSpec((nb, t, LANES), lambda q: (0, 0, q), pipeline_mode=pl.Buffered(1)),
                  spec((S5_GQ, L * C, L * C)),
                  spec((npair, 2 * N, 2 * L * C)), spec((npair, 2 * N, 2 * L * C)),
                  spec((npair, 2 * N, 2 * L * C)), spec((npair, 2 * N, 2 * L * C)),
                  spec((npair, 2, 2 * N))],
        out_specs=[pl.BlockSpec((nb, t, LANES), lambda q: (0, 0, q)),
                   spec((npair, nb, 2 * N)), spec((npair, nb, 2 * N))],
        out_shape=[jax.ShapeDtypeStruct((nb, t, SSM_WIDTH), F32),
                   jax.ShapeDtypeStruct((G // 2, nb, 2 * N), F32),
                   jax.ShapeDtypeStruct((G // 2, nb, 2 * N), F32)],
        scratch_shapes=[pltpu.VMEM((S5_GQ, 2, nc, LANES), F32)]
        + [pltpu.VMEM((npair, nc, 2 * N), F32) for _ in range(4)],
        compiler_params=_params(("arbitrary",)),
        name="s5_prompt",
    )(u.reshape(nb, t, SSM_WIDTH), tabs['m'], tabs['pre'], tabs['pim'], tabs['cre'], tabs['cim'],
      tabs['a16'])
    unpair = lambda x: x.reshape(G // 2, nb, 2, N).transpose(1, 0, 2, 3).reshape(nb, G, N)
    return y.reshape(nb * t, SSM_WIDTH), unpair(xr), unpair(xi)


def _s5_step_kernel(u_ref, xr_ref, xi_ref, ar_ref, ai_ref, brt_ref, bit_ref, cr_ref, ci_ref,
                    y_ref, nr_ref, ni_ref, bd_ref):
    G, N, C = SSM_GROUPS, SSM_STATE, SSM_GROUP

    def block_diag(slot, src_ref):
        bd_ref[slot] = jnp.zeros((G * C, G * N), BF16)
        for g in range(G):
            bd_ref[slot, g * C:(g + 1) * C, g * N:(g + 1) * N] = src_ref[g].astype(BF16)
        return bd_ref[slot]

    ub = u_ref[...].astype(BF16)
    xr, xi, ar, ai = xr_ref[...], xi_ref[...], ar_ref[...], ai_ref[...]
    nr = ar * xr - ai * xi + jnp.dot(ub, block_diag(0, brt_ref), preferred_element_type=F32)
    ni = ar * xi + ai * xr + jnp.dot(ub, block_diag(1, bit_ref), preferred_element_type=F32)
    nr_ref[...] = nr
    ni_ref[...] = ni
    y_ref[...] = _dot_nt(nr, block_diag(2, cr_ref)) - _dot_nt(ni, block_diag(3, ci_ref))


def _s5_step(u, st_re, st_im, tabs, C_re, C_im):
    G, N, C = SSM_GROUPS, SSM_STATE, SSM_GROUP
    n = u.shape[0]
    ops = (u, st_re.reshape(n, G * N), st_im.reshape(n, G * N),
           tabs['arr'].reshape(1, G * N), tabs['air'].reshape(1, G * N),
           tabs['brt'], tabs['bit'], C_re, C_im)
    y, nr, ni = pl.pallas_call(
        _s5_step_kernel,
        grid=(1,),
        in_specs=[_full_spec(o.shape) for o in ops],
        out_specs=[_full_spec((n, SSM_WIDTH)), _full_spec((n, G * N)), _full_spec((n, G * N))],
        out_shape=[jax.ShapeDtypeStruct((n, SSM_WIDTH), F32),
                   jax.ShapeDtypeStruct((n, G * N), F32),
                   jax.ShapeDtypeStruct((n, G * N), F32)],
        scratch_shapes=[pltpu.VMEM((4, G * C, G * N), BF16)],
        compiler_params=_params(("arbitrary",)),
        name="s5_step",
    )(*ops)
    return y, nr.reshape(n, G, N), ni.reshape(n, G, N)


def _tail_kernel(*refs, tm, nsub, tiles_per_seq, decode, final):
    if decode:
        (x_ref, rw_ref, ys_ref, u_ref, gl_ref, pe_ref, st2_ref, st1_ref,
         dsk_ref, wglu_ref, wout_ref, ln2_ref, wfi_ref, cw_ref, cb_ref, wfo_ref,
         ln3_ref, wpg_ref, wple_ref, fin_ref, y_ref, conv_ref) = refs
    else:
        (x_ref, rw_ref, ys_ref, u_ref, gl_ref, pe_ref,
         dsk_ref, wglu_ref, wout_ref, ln2_ref, wfi_ref, cw_ref, cb_ref, wfo_ref,
         ln3_ref, wpg_ref, wple_ref, fin_ref, y_ref, conv_ref, carry_ref) = refs

    sub = tm // nsub
    cw = cw_ref[...]
    last2 = {}
    if not decode:
        @pl.when(pl.program_id(0) % tiles_per_seq == 0)
        def _():
            carry_ref[...] = jnp.zeros_like(carry_ref)

        last2[-1] = carry_ref[...]

    def row_block(s):
        rs = slice(s * sub, (s + 1) * sub)
        x = x_ref[rs, :]
        z = _gelu(ys_ref[rs, :] + dsk_ref[...] * u_ref[rs, :])
        zz = _dot(z, wglu_ref[...])
        yield
        s5 = zz[:, :D_MODEL] * _sigmoid(zz[:, D_MODEL:])
        gl = gl_ref[rs, :]
        merged = _sigmoid(gl[:, :D_MODEL]) * rw_ref[rs, :] + _sigmoid(gl[:, D_MODEL:]) * s5
        x = x + _dot(merged, wout_ref[...])
        yield
        h2 = _rmsnorm(x, ln2_ref[...]).astype(BF16)
        a_up = jnp.dot(h2, wfi_ref[:, :D_FF], preferred_element_type=F32)
        b_up = jnp.dot(h2, wfi_ref[:, D_FF:], preferred_element_type=F32)
        if decode:
            prev2 = st2_ref[rs, :]
            prev1 = st1_ref[rs, :]
            conv_ref[rs, :] = a_up
        else:
            last2[s] = a_up[sub - 2:sub, :]
        yield
        if not decode:
            rows = lax.broadcasted_iota(jnp.int32, (sub, 1), 0)
            c0 = last2[s - 1][0:1, :]
            c1 = last2[s - 1][1:2, :]
            prev1 = jnp.where(rows == 0, c1, pltpu.roll(a_up, 1, axis=0))
            prev2 = jnp.where(rows == 0, c0,
                              jnp.where(rows == 1, c1, pltpu.roll(a_up, 2, axis=0)))
        a_conv = cw[0:1] * prev2 + cw[1:2] * prev1 + cw[2:3] * a_up + cb_ref[...]
        x = x + _dot(_gelu(a_conv) * b_up, wfo_ref[...])
        yield
        pg = _sigmoid(_dot(_rmsnorm(x, ln3_ref[...]), wpg_ref[...]))
        x = x + pg * _dot(pe_ref[rs, :], wple_ref[...])
        y_ref[rs, :] = _rmsnorm(x, fin_ref[...]) if final else x
        yield

    blocks = [row_block(s) for s in range(nsub)]
    for _ in range(5):
        for blk in blocks:
            next(blk)
    if not decode:
        carry_ref[...] = last2[nsub - 1]
        conv_ref[0] = last2[nsub - 1]


def _tail(x2d, rw, ys, u, gl, pe, st_conv, D_skip, wglu, wout, ln2_g, wfi, conv_w, conv_b, wfo,
          ln3_g, wpg, wple, final_g, *, tm, tiles_per_seq, decode, final):
    n = x2d.shape[0]
    row = lambda w: pl.BlockSpec((tm, w), lambda i: (i, 0))
    consts = [D_skip.reshape(1, SSM_WIDTH), wglu, wout, ln2_g.reshape(1, D_MODEL), wfi,
              conv_w, conv_b.reshape(1, D_FF), wfo, ln3_g.reshape(1, D_MODEL), wpg, wple,
              final_g.reshape(1, D_MODEL)]
    acts = [x2d, rw, ys, u, gl, pe]
    in_specs = [row(a.shape[1]) for a in acts]
    scratch = []
    if decode:
        acts += [st_conv[:, 0], st_conv[:, 1]]
        in_specs += [row(D_FF), row(D_FF)]
        conv_spec = row(D_FF)
        conv_shape = (n, D_FF)
    else:
        nseq = n // (tm * tiles_per_seq)
        conv_spec = pl.BlockSpec((1, 2, D_FF), lambda i: (i // tiles_per_seq, 0, 0))
        conv_shape = (nseq, 2, D_FF)
        scratch = [pltpu.VMEM((2, D_FF), F32)]
    in_specs += [pl.BlockSpec(c.shape, lambda i, nd=c.ndim: (0,) * nd,
                              pipeline_mode=pl.Buffered(1)) for c in consts]
    y, conv = pl.pallas_call(
        functools.partial(_tail_kernel, tm=tm, nsub=1 if decode else TAIL_ROW_BLOCKS,
                          tiles_per_seq=tiles_per_seq, decode=decode,
                          final=final),
        grid=(n // tm,),
        in_specs=in_specs,
        out_specs=[row(D_MODEL), conv_spec],
        out_shape=[jax.ShapeDtypeStruct((n, D_MODEL), F32),
                   jax.ShapeDtypeStruct(conv_shape, F32)],
        scratch_shapes=scratch,
        compiler_params=_params(("arbitrary",)),
        name="tail_decode" if decode else "tail_prompt",
    )(*acts, *consts)
    if decode:
        conv = jnp.stack([st_conv[:, 1], conv], axis=1)
    return y, conv


def _layer(x, pe, states, W, *, decode, final, tm_proj, tt, tm_tail, tb):
    nb, t, _ = x.shape
    n = nb * t
    x2d = x.reshape(n, D_MODEL)
    prw, u, gl = _proj(x2d, W['ln1_g'], W['w_in'], tm_proj)
    if decode:
        st_shift, st_wkv, st_re, st_im, st_conv = states
        rw, s_new = _rwkv_step(prw, st_shift, st_wkv, W['rowp'], W['wla'], W['g2'], W['eones'],
                               W['w_rw_out'], tb)
        ys, xr, xi = _s5_step(u, st_re, st_im, W['s5'], W['C_re'], W['C_im'])
        new_shift = prw
    else:
        st_conv = None
        rw, s_new = _rwkv_prompt(prw, W['rowp'], W['wla'], W['g2'], W['eones'], W['w_rw_out'],
                                 nb, t, tt)
        ys, xr, xi = _s5_prompt(u, W['s5'], nb, t)
        new_shift = prw.reshape(nb, t, SHIFT_WIDTH)[:, -1]
    y, new_conv = _tail(x2d, rw, ys, u, gl, pe.reshape(n, PLE_DIM), st_conv, W['D_skip'],
                        W['w_glu'], W['w_out'], W['ln2_g'], W['w_ffn_in'], W['conv_w'],
                        W['conv_b'], W['w_ffn_out'], W['ln3_g'], W['w_ple_gate'], W['w_ple'],
                        W['final_g'], tm=tm_tail, tiles_per_seq=max(t // tm_tail, 1),
                        decode=decode, final=final)
    return y.reshape(nb, t, D_MODEL), (new_shift, s_new, xr, xi, new_conv)


def kernel(x_prompt, x_sample, p_prompt, p_sample, state_shift, state_wkv, state_ssm_re, state_ssm_im, state_conv, ln1_g, w_in, mu_shift, w0, w2, a0, a2, g2, k_k, k_a, r_k, lnx_g, lnx_b, w_rw_out, A_re, A_im, log_dt, B_re, B_im, C_re, C_im, D_skip, w_glu, w_out, ln2_g, w_ffn_in, conv_w, conv_b, w_ffn_out, ln3_g, w_ple_gate, w_ple, final_g):
    depth = w_in.shape[0]
    xp, xs = x_prompt, x_sample
    pst = [[] for _ in range(5)]
    sst = [[] for _ in range(5)]
    for i in range(depth):
        wla, g2b, wo = _rwkv_weights(w2[i], a2[i], g2[i], w_rw_out[i])
        W = dict(
            ln1_g=ln1_g[i], w_in=w_in[i].astype(BF16),
            rowp=_row_params(mu_shift[i], w0[i], a0[i], k_k[i], k_a[i], r_k[i].reshape(-1),
                             lnx_g[i], lnx_b[i]),
            wla=wla, g2=g2b, eones=_rwkv_consts(), w_rw_out=wo,
            s5=_s5_prep(A_re[i], A_im[i], log_dt[i], B_re[i], B_im[i], C_re[i], C_im[i]),
            C_re=C_re[i], C_im=C_im[i], D_skip=D_skip[i],
            w_glu=w_glu[i].astype(BF16), w_out=w_out[i].astype(BF16), ln2_g=ln2_g[i],
            w_ffn_in=w_ffn_in[i].astype(BF16), conv_w=conv_w[i], conv_b=conv_b[i],
            w_ffn_out=w_ffn_out[i].astype(BF16), ln3_g=ln3_g[i],
            w_ple_gate=w_ple_gate[i].astype(BF16), w_ple=w_ple[i].astype(BF16), final_g=final_g)
        final = i == depth - 1
        tp = xp.shape[1]
        xp, sp = _layer(xp, p_prompt[i], None, W, decode=False, final=final,
                        tm_proj=min(TM_PROJ, tp), tt=min(TT_RWKV, tp), tm_tail=min(TM_TAIL, tp),
                        tb=None)
        ns = xs.shape[0]
        xs, ss = _layer(xs, p_sample[i],
                        (state_shift[i], state_wkv[i], state_ssm_re[i], state_ssm_im[i],
                         state_conv[i]),
                        W, decode=True, final=final, tm_proj=ns, tt=None, tm_tail=ns,
                        tb=min(TB_STEP, ns))
        for j in range(5):
            pst[j].append(sp[j])
            sst[j].append(ss[j])
    return (xp, xs,
            jnp.stack(pst[0]), jnp.stack(pst[1]), jnp.stack(pst[2]), jnp.stack(pst[3]),
            jnp.stack(pst[4]),
            jnp.stack(sst[0]), jnp.stack(sst[1]), jnp.stack(sst[2]), jnp.stack(sst[3]),
            jnp.stack(sst[4]))
```

```python
import functools
import math

import jax
import jax.numpy as jnp
from jax import lax
from jax.experimental import pallas as pl
from jax.experimental.pallas import tpu as pltpu

F32 = jnp.float32
BF16 = jnp.bfloat16

D_MODEL = 1024
RW_HEADS = 8
HEAD_DIM = 64
RW_WIDTH = RW_HEADS * HEAD_DIM
LORA_W = 64
LORA_A = 64
LORA_G = 128
SHIFT_WIDTH = 3 * RW_WIDTH + LORA_W + LORA_A + LORA_G
SSM_WIDTH = 512
SSM_GROUP = 16
SSM_GROUPS = SSM_WIDTH // SSM_GROUP
SSM_STATE = 64
GATE_OFF = SHIFT_WIDTH + SSM_WIDTH
IN_WIDTH = SHIFT_WIDTH + SSM_WIDTH + 2 * D_MODEL
D_FF = 2816
PLE_DIM = 256
EPS = 1e-6
GN_EPS = 64e-5

LANES = 128
WKV_CHUNK = 64
S5_CHUNK = 16
N_PAIRS = RW_HEADS // 2
VMEM_LIMIT = 56 * 1024 * 1024
TM_PROJ = 512
TT_RWKV = 256
TM_TAIL = 256
RWKV_SEQ_GROUPS = 2
TAIL_ROW_BLOCKS = 2


def _dot(a, b):
    return jnp.dot(a.astype(BF16), b.astype(BF16), preferred_element_type=F32)


def _dot_nt(a, b):
    return lax.dot_general(a.astype(BF16), b.astype(BF16), (((1,), (1,)), ((), ())),
                           preferred_element_type=F32)


def _dot_tn(a, b):
    return lax.dot_general(a.astype(BF16), b.astype(BF16), (((0,), (0,)), ((), ())),
                           preferred_element_type=F32)


def _split3(x):
    h1 = x.astype(BF16)
    r1 = x - h1.astype(F32)
    h2 = r1.astype(BF16)
    h3 = (r1 - h2.astype(F32)).astype(BF16)
    return h1, h2, h3


def _dot_exact_lhs(a_bf16, x):
    h1, h2, h3 = _split3(x)
    d = lambda h: jnp.dot(a_bf16, h, preferred_element_type=F32)
    return d(h1) + d(h2) + d(h3)


def _sigmoid(x):
    return 1.0 / (1.0 + jnp.exp(-x))


def _gelu(x):
    c = math.sqrt(2.0 / math.pi)
    return 0.5 * x * (1.0 + jnp.tanh(c * (x + 0.044715 * (x * x * x))))


def _rmsnorm(x, g):
    return x * lax.rsqrt(jnp.mean(x * x, axis=-1, keepdims=True) + EPS) * g


def _full_spec(shape):
    nd = len(shape)
    return pl.BlockSpec(shape, lambda *_: (0,) * nd)


def _params(sem):
    return pltpu.CompilerParams(dimension_semantics=sem, vmem_limit_bytes=VMEM_LIMIT)


def _proj_kernel(x_ref, g_ref, w_ref, prw_ref, u_ref, gl_ref):
    h = _rmsnorm(x_ref[...], g_ref[...]).astype(BF16)
    prw_ref[...] = jnp.dot(h, w_ref[:, :SHIFT_WIDTH], preferred_element_type=F32)
    u_ref[...] = jnp.dot(h, w_ref[:, SHIFT_WIDTH:GATE_OFF], preferred_element_type=F32)
    gl_ref[...] = jnp.dot(h, w_ref[:, GATE_OFF:], preferred_element_type=F32)


def _proj(x2d, ln1_g, w_in_bf, tm):
    n = x2d.shape[0]
    row = lambda w: pl.BlockSpec((tm, w), lambda i: (i, 0))
    return pl.pallas_call(
        _proj_kernel,
        grid=(n // tm,),
        in_specs=[row(D_MODEL), _full_spec((1, D_MODEL)), _full_spec((D_MODEL, IN_WIDTH))],
        out_specs=[row(SHIFT_WIDTH), row(SSM_WIDTH), row(2 * D_MODEL)],
        out_shape=[jax.ShapeDtypeStruct((n, SHIFT_WIDTH), F32),
                   jax.ShapeDtypeStruct((n, SSM_WIDTH), F32),
                   jax.ShapeDtypeStruct((n, 2 * D_MODEL), F32)],
        compiler_params=_params(("arbitrary",)),
        name="proj",
    )(x2d, ln1_g.reshape(1, D_MODEL), w_in_bf)


def _rwkv_token_math_stages(p, prev, mu, w0, a0, k_k, k_a, r_k, wla, g2, eones):
    xs = p + (prev - p) * mu
    r = xs[:, 0:RW_WIDTH]
    k = xs[:, RW_WIDTH:2 * RW_WIDTH]
    v = xs[:, 2 * RW_WIDTH:3 * RW_WIDTH]
    xwa = xs[:, 3 * RW_WIDTH:3 * RW_WIDTH + LORA_W + LORA_A]
    xg = xs[:, 3 * RW_WIDTH + LORA_W + LORA_A:]
    lane = lax.broadcasted_iota(jnp.int32, xwa.shape, 1)
    la = _dot(jnp.where(lane < LORA_W, jnp.tanh(xwa), xwa), wla)
    g = _dot(_sigmoid(xg), g2)
    yield None
    wpre = w0 + la[:, :RW_WIDTH]
    softplus = jnp.maximum(-wpre, 0.0) + jnp.log(1.0 + jnp.exp(-jnp.abs(wpre)))
    lw = -jnp.exp(-softplus - 0.5)
    a = _sigmoid(a0 + la[:, RW_WIDTH:])
    kkr = k * k_k
    ss = _dot(kkr * kkr, eones)
    kk = kkr / jnp.maximum(jnp.sqrt(ss), 1e-12)
    kp = k * (1.0 + (a - 1.0) * k_a)
    bonus = _dot(r * kp * r_k, eones) * v
    yield r, v, lw, a, g, kk, kp, bonus


def _rwkv_token_math(*args):
    *_, out = _rwkv_token_math_stages(*args)
    return out


def _rwkv_output(y, bonus, g, lnx_g, lnx_b, eones, w_o):
    emean = eones * (1.0 / HEAD_DIM)
    mean = _dot(y, emean)
    d = y - mean
    var = _dot(d * d, emean)
    yn = d * lax.rsqrt(var + GN_EPS) * lnx_g + lnx_b
    return _dot((yn + bonus) * g, w_o)


def _head_masks(shape):
    lane = lax.broadcasted_iota(jnp.int32, shape, 1)
    return lane < HEAD_DIM, lane >= HEAD_DIM


def _stack_heads(x, m_a, m_b):
    return jnp.concatenate([jnp.where(m_a, x, 0.0), jnp.where(m_b, x, 0.0)], axis=0)


def _rwkv_prompt_kernel(p_ref, mu_ref, w0_ref, a0_ref, kk_ref, ka_ref, rk_ref, lg_ref, lb_ref,
                        wla_ref, g2_ref, e_ref, ltri_ref, wo_ref,
                        out_ref, sout_ref,
                        carry_ref, sp_ref, y_ref, pend_ref, gend_ref, new_ref, gnew_ref,
                        *, tt, nt, nh):
    s = pl.program_id(0)
    nch = tt // WKV_CHUNK
    eones = e_ref[...]
    halves = range(nh)
    pairs = range(N_PAIRS)
    rows_of = lambda ch: slice(ch * WKV_CHUNK, (ch + 1) * WKV_CHUNK)
    lanes_of = lambda j: slice(j * LANES, (j + 1) * LANES)

    @pl.when(s == 0)
    def _():
        pend_ref[...] = jnp.zeros_like(pend_ref)
        gend_ref[...] = jnp.zeros_like(gend_ref)
        sp_ref[...] = jnp.zeros_like(sp_ref)

    @pl.when(s % nt == 0)
    def _():
        carry_ref[...] = jnp.zeros_like(carry_ref)

    @pl.when(s % nt == 1 % nt)
    def _():
        sp_ref[...] = jnp.zeros_like(sp_ref)

    rows = lax.broadcasted_iota(jnp.int32, (tt, 1), 0)
    token_math = []
    for h in halves:
        p = p_ref[h]
        prev = jnp.where(rows == 0, carry_ref[h], pltpu.roll(p, 1, axis=0))
        carry_ref[h] = p[tt - 1:tt, :]
        token_math.append(_rwkv_token_math_stages(
            p, prev, mu_ref[...], w0_ref[...], a0_ref[...], kk_ref[...], ka_ref[...],
            rk_ref[...], wla_ref[...], g2_ref[...], eones))
        next(token_math[h])

    m_a, m_b = _head_masks((WKV_CHUNK, LANES))
    ri = lax.broadcasted_iota(jnp.int32, (2 * WKV_CHUNK, 2 * WKV_CHUNK), 0)
    ci = lax.broadcasted_iota(jnp.int32, (2 * WKV_CHUNK, 2 * WKV_CHUNK), 1)
    same_head = (ri >= WKV_CHUNK) == (ci >= WKV_CHUNK)
    strict = same_head & (ri > ci)
    incl = same_head & (ri >= ci)
    blocks = [(h, ch, j) for h in halves for ch in range(nch) for j in pairs]
    blk = lambda i, h, ch, j: pend_ref[h, i, rows_of(ch), lanes_of(j)]
    stack = lambda i, h, ch, j: _stack_heads(blk(i, h, ch, j), m_a, m_b)
    vss = {b: stack(4, *b) for b in blocks}
    gms = {b: _dot_nt(jnp.concatenate([stack(0, *b), stack(1, *b)], axis=0),
                      jnp.concatenate([stack(2, *b), stack(3, *b)], axis=0)) for b in blocks}
    n_abs = {b: jnp.where(strict, gms[b][:LANES, :LANES], 0.0) for b in blocks}
    wvs = {b: _dot(jnp.where(strict, gms[b][:LANES, LANES:], 0.0), vss[b]) for b in blocks}
    a_ys = {b: jnp.concatenate([jnp.where(incl, gms[b][LANES:, :LANES], 0.0),
                                jnp.where(incl, gms[b][LANES:, LANES:], 0.0)], axis=1)
            for b in blocks}

    tm = [next(token_math[h]) for h in halves]
    cs = [_dot_exact_lhs(ltri_ref[...], tm[h][2]) for h in halves]

    eye = (ri == ci).astype(F32)
    tinvs = {b: eye + n_abs[b] for b in blocks}
    pows = n_abs
    for _ in range(5):
        pows = {b: _dot(pows[b], pows[b]) for b in blocks}
        tinvs = {b: tinvs[b] + _dot(pows[b], tinvs[b]) for b in blocks}

    for h in halves:
        r, v_n, lw, a, g_n, kk, kp, bonus_n = tm[h]
        c = cs[h]
        einv = jnp.exp(-c)
        bvec = kk * a
        new_ref[h, 0] = -kk * jnp.exp(c - lw)
        new_ref[h, 1] = r * jnp.exp(c)
        new_ref[h, 2] = bvec * einv
        new_ref[h, 3] = kp * einv
        new_ref[h, 4] = v_n
        for ch in range(nch):
            c_end = c[(ch + 1) * WKV_CHUNK - 1:(ch + 1) * WKV_CHUNK, :]
            e_rest = jnp.exp(c_end - c[rows_of(ch), :])
            gnew_ref[h, ch] = jnp.exp(c_end)
            new_ref[h, 5, rows_of(ch), :] = bvec[rows_of(ch), :] * e_rest
            new_ref[h, 6, rows_of(ch), :] = kp[rows_of(ch), :] * e_rest
        new_ref[h, 7] = bonus_n
        new_ref[h, 8] = g_n

    sps = {(h, j): sp_ref[h, j] for h in halves for j in pairs}
    for ch in range(nch):
        hj = [(h, j) for h in halves for j in pairs]
        arss = {(h, j): _dot_nt(jnp.concatenate([blk(0, h, ch, j), blk(1, h, ch, j)], axis=0),
                                sps[h, j]) for h, j in hj}
        uvs = {(h, j): jnp.concatenate(
            [_dot(tinvs[h, ch, j],
                  _stack_heads(arss[h, j][:WKV_CHUNK], m_a, m_b) + wvs[h, ch, j]),
             vss[h, ch, j]], axis=0) for h, j in hj}
        yss = {(h, j): _dot(a_ys[h, ch, j], uvs[h, j]) for h, j in hj}
        incs = {(h, j): _dot_tn(uvs[h, j], jnp.concatenate(
            [stack(5, h, ch, j), stack(6, h, ch, j)], axis=0)) for h, j in hj}
        for h, j in hj:
            y_ref[h, rows_of(ch), lanes_of(j)] = (yss[h, j][:WKV_CHUNK] + yss[h, j][WKV_CHUNK:]
                                                  + arss[h, j][WKV_CHUNK:])
            sps[h, j] = sps[h, j] * gend_ref[h, ch][:, lanes_of(j)] + incs[h, j]
    for h in halves:
        for j in pairs:
            sp_ref[h, j] = sps[h, j]
        out_ref[h] = _rwkv_output(y_ref[h], pend_ref[h, 7], pend_ref[h, 8], lg_ref[...],
                                  lb_ref[...], eones, wo_ref[...])

    pend_ref[...] = new_ref[...]
    gend_ref[...] = gnew_ref[...]

    @pl.when((s % nt == 0) & (s > 0))
    def _():
        for h in halves:
            for j in pairs:
                sp = sp_ref[h, j]
                sout_ref[h, 0, 2 * j] = sp[:HEAD_DIM, :HEAD_DIM]
                sout_ref[h, 0, 2 * j + 1] = sp[HEAD_DIM:, HEAD_DIM:]


def _rwkv_consts():
    idx = jnp.arange(RW_WIDTH)
    eones = (idx[:, None] // HEAD_DIM == idx[None, :] // HEAD_DIM).astype(BF16)
    return eones


def _rwkv_weights(w2, a2, g2, w_rw_out):
    wla = jnp.zeros((LORA_W + LORA_A, 2 * RW_WIDTH), F32)
    wla = wla.at[:LORA_W, :RW_WIDTH].set(w2).at[LORA_W:, RW_WIDTH:].set(a2)
    return wla.astype(BF16), g2.astype(BF16), w_rw_out.astype(BF16)


def _row_params(mu_shift, w0, a0, k_k, k_a, r_k, lnx_g, lnx_b):
    return (mu_shift.reshape(1, SHIFT_WIDTH), w0.reshape(1, RW_WIDTH), a0.reshape(1, RW_WIDTH),
            k_k.reshape(1, RW_WIDTH), k_a.reshape(1, RW_WIDTH), r_k.reshape(1, RW_WIDTH),
            lnx_g.reshape(1, RW_WIDTH), lnx_b.reshape(1, RW_WIDTH))


def _rwkv_prompt(prw, rowp, wla, g2b, eones, wo, nb, t, tt):
    nt = t // tt
    idx = jnp.arange(tt)
    ltri = ((idx[:, None] // WKV_CHUNK == idx[None, :] // WKV_CHUNK)
            & (idx[:, None] >= idx[None, :])).astype(BF16)
    consts = list(rowp) + [wla, g2b, eones, ltri, wo]
    nh = RWKV_SEQ_GROUPS if nb % RWKV_SEQ_GROUPS == 0 else 1
    ntiles = (nb // nh) * nt
    nch = tt // WKV_CHUNK
    done = lambda s: jnp.maximum(s - 1, 0)
    out, s_new = pl.pallas_call(
        functools.partial(_rwkv_prompt_kernel, tt=tt, nt=nt, nh=nh),
        grid=(ntiles + 1,),
        in_specs=[pl.BlockSpec((nh, tt, SHIFT_WIDTH),
                               lambda s: (0, jnp.minimum(s, ntiles - 1), 0))]
        + [_full_spec(c.shape) for c in consts],
        out_specs=[pl.BlockSpec((nh, tt, D_MODEL), lambda s: (0, done(s), 0)),
                   pl.BlockSpec((nh, 1, RW_HEADS, HEAD_DIM, HEAD_DIM),
                                lambda s: (0, done(s) // nt, 0, 0, 0))],
        out_shape=[jax.ShapeDtypeStruct((nh, nb * t // nh, D_MODEL), F32),
                   jax.ShapeDtypeStruct((nh, nb // nh, RW_HEADS, HEAD_DIM, HEAD_DIM), F32)],
        scratch_shapes=[pltpu.VMEM((nh, 1, SHIFT_WIDTH), F32),
                        pltpu.VMEM((nh, N_PAIRS, LANES, LANES), F32),
                        pltpu.VMEM((nh, tt, RW_WIDTH), F32),
                        pltpu.VMEM((nh, 9, tt, RW_WIDTH), F32),
                        pltpu.VMEM((nh, nch, 1, RW_WIDTH), F32),
                        pltpu.VMEM((nh, 9, tt, RW_WIDTH), F32),
                        pltpu.VMEM((nh, nch, 1, RW_WIDTH), F32)],
        compiler_params=_params(("arbitrary",)),
        name="rwkv_prompt",
    )(prw.reshape(nh, nb * t // nh, SHIFT_WIDTH), *consts)
    return (out.reshape(nb * t, D_MODEL),
            s_new.reshape(nb, RW_HEADS, HEAD_DIM, HEAD_DIM))


def _rwkv_step_kernel(p_ref, prev_ref, s_ref, mu_ref, w0_ref, a0_ref, kk_ref, ka_ref, rk_ref,
                      lg_ref, lb_ref, wla_ref, g2_ref, e_ref, wo_ref,
                      out_ref, sout_ref,
                      qt_ref, wt_ref, rt_ref, bt_ref, kt_ref, vt_ref, yt_ref, g_ref, bonus_ref):
    hp = pl.program_id(0)

    @pl.when(hp == 0)
    def _():
        r, v, lw, a, g, kk, kp, bonus = _rwkv_token_math(
            p_ref[...], prev_ref[...], mu_ref[...], w0_ref[...], a0_ref[...], kk_ref[...],
            ka_ref[...], rk_ref[...], wla_ref[...], g2_ref[...], e_ref[...])
        qt_ref[...] = (-kk).T
        wt_ref[...] = jnp.exp(lw).T
        rt_ref[...] = r.T
        bt_ref[...] = (kk * a).T
        kt_ref[...] = kp.T
        vt_ref[...] = v.T
        g_ref[...] = g
        bonus_ref[...] = bonus

    def per_value_row(vi, carry):
        for hl in range(2):
            row0 = pl.multiple_of((2 * hp + hl) * HEAD_DIM, HEAD_DIM)
            ks = pl.ds(row0, HEAD_DIM)
            s = s_ref[hl, vi]
            sa = jnp.sum(s * qt_ref[ks, :], axis=0, keepdims=True)
            s_new = (s * wt_ref[ks, :] + sa * bt_ref[ks, :]
                     + vt_ref[pl.ds(row0 + vi, 1), :] * kt_ref[ks, :])
            yt_ref[pl.ds(row0 + vi, 1), :] = jnp.sum(s_new * rt_ref[ks, :], axis=0, keepdims=True)
            sout_ref[hl, vi] = s_new
        return carry

    lax.fori_loop(0, HEAD_DIM, per_value_row, 0, unroll=2)

    @pl.when(hp == pl.num_programs(0) - 1)
    def _():
        out_ref[...] = _rwkv_output(yt_ref[...].T, bonus_ref[...], g_ref[...], lg_ref[...],
                                    lb_ref[...], e_ref[...], wo_ref[...])


def _rwkv_step(prw, prev, state, rowp, wla, g2b, eones, wo):
    n = prw.shape[0]
    consts = list(rowp) + [wla, g2b, eones, wo]
    state_t = jnp.transpose(state, (1, 2, 3, 0))
    sspec = pl.BlockSpec((2, HEAD_DIM, HEAD_DIM, n), lambda hp: (hp, 0, 0, 0))
    out, s_new_t = pl.pallas_call(
        _rwkv_step_kernel,
        grid=(N_PAIRS,),
        in_specs=[_full_spec((n, SHIFT_WIDTH)), _full_spec((n, SHIFT_WIDTH)), sspec]
        + [_full_spec(c.shape) for c in consts],
        out_specs=[_full_spec((n, D_MODEL)), sspec],
        out_shape=[jax.ShapeDtypeStruct((n, D_MODEL), F32),
                   jax.ShapeDtypeStruct((RW_HEADS, HEAD_DIM, HEAD_DIM, n), F32)],
        scratch_shapes=[pltpu.VMEM((RW_WIDTH, n), F32) for _ in range(7)]
        + [pltpu.VMEM((n, RW_WIDTH), F32) for _ in range(2)],
        compiler_params=_params(("arbitrary",)),
        name="rwkv_step",
    )(prw, prev, state_t, *consts)
    return out, jnp.transpose(s_new_t, (3, 0, 1, 2))


def _zoh(lr, li, dt):
    mag = jnp.exp(lr * dt)
    ar, ai = mag * jnp.cos(li * dt), mag * jnp.sin(li * dt)
    den = lr * lr + li * li
    fr = ((ar - 1.0) * lr + ai * li) / den
    fi = (ai * lr - (ar - 1.0) * li) / den
    return ar, ai, fr, fi


def _dot_split(a, b):
    a1 = a.astype(BF16)
    a2 = (a - a1.astype(F32)).astype(BF16)
    b1 = b.astype(BF16)
    b2 = (b - b1.astype(F32)).astype(BF16)
    d = lambda x, y: jnp.dot(x, y, preferred_element_type=F32)
    return d(a1, b1) + (d(a1, b2) + d(a2, b1))


def _s5_prep_kernel(lrc_ref, lic_ref, lrr_ref, lir_ref, ldt_ref, bret_ref, bimt_ref,
                    brl_ref, bil_ref, crl_ref, cil_ref,
                    m_ref, pre_ref, pim_ref, cre_ref, cim_ref, a16_ref,
                    brt_ref, bit_ref, arr_ref, air_ref):
    L, C, N = S5_CHUNK, SSM_GROUP, SSM_STATE
    lag = lax.broadcasted_iota(jnp.int32, (1, LANES), 1).astype(F32)
    sel_row = lax.broadcasted_iota(jnp.int32, (LANES, L * C), 0)
    sel_lag = lax.shift_right_logical(lax.broadcasted_iota(jnp.int32, (LANES, L * C), 1), 4)
    pick = lambda cond: jnp.where(cond, 1.0, 0.0).astype(BF16)
    r_same, r_next, r_rev = (pick(sel_row == sel_lag), pick(sel_row == sel_lag + 1),
                             pick(sel_row == (L - 1) - sel_lag))

    def spread(e, r):
        h1, h2, h3 = _split3(e)
        d = lambda h: jnp.dot(h, r, preferred_element_type=F32)
        return d(h1) + d(h2) + d(h3)

    lane2 = lax.broadcasted_iota(jnp.int32, (C, L * C), 1)
    zero = jnp.zeros((N, L * C), F32)
    a16r, a16i = [], []
    for k in range(2):
        dt = jnp.exp(ldt_ref[k])
        lrc, lic = lrc_ref[k], lic_ref[k]
        mag = jnp.exp(lrc * dt * lag)
        er, ei = mag * jnp.cos(lic * dt * lag), mag * jnp.sin(lic * dt * lag)
        e0r, e0i = spread(er, r_same), spread(ei, r_same)
        e1r, e1i = spread(er, r_next), spread(ei, r_next)
        e2r, e2i = spread(er, r_rev), spread(ei, r_rev)
        _, _, frc, fic = _zoh(lrc, lic, dt)
        bbr = frc * brl_ref[k] - fic * bil_ref[k]
        bbi = frc * bil_ref[k] + fic * brl_ref[k]
        crl, cil = crl_ref[k], cil_ref[k]
        halves = lambda x: jnp.concatenate([x, zero] if k == 0 else [zero, x], axis=1)
        rows = slice(k * N, (k + 1) * N)
        pre_ref[0, rows, :] = halves(e2r * bbr - e2i * bbi).astype(BF16)
        pim_ref[0, rows, :] = halves(e2r * bbi + e2i * bbr).astype(BF16)
        cre_ref[0, rows, :] = halves(crl * e1r - cil * e1i).astype(BF16)
        cim_ref[0, rows, :] = halves(-(crl * e1i + cil * e1r)).astype(BF16)

        lrr, lir = lrr_ref[k], lir_ref[k]
        arr, air, frr, fir = _zoh(lrr, lir, dt)
        brt = frr * bret_ref[k] - fir * bimt_ref[k]
        bit = frr * bimt_ref[k] + fir * bret_ref[k]
        brt_ref[k], bit_ref[k], arr_ref[k], air_ref[k] = brt, bit, arr, air
        m16 = jnp.exp(lrr * dt * L)
        a16r.append(m16 * jnp.cos(lir * dt * L))
        a16i.append(m16 * jnp.sin(lir * dt * L))
        klt = (_dot_split(brt, crl * e0r - cil * e0i)
               - _dot_split(bit, crl * e0i + cil * e0r))
        for j in range(L):
            blk = klt if j == 0 else jnp.where(lane2 >= j * C, pltpu.roll(klt, j * C, axis=1), 0.0)
            m_ref[k, j * C:(j + 1) * C, :] = blk.astype(BF16)
    a16_ref[0, 0:1, :] = jnp.concatenate(a16r, axis=1)
    a16_ref[0, 1:2, :] = jnp.concatenate(a16i, axis=1)


def _s5_prep(A_re, A_im, log_dt, B_re, B_im, C_re, C_im):
    L, G, N, C = S5_CHUNK, SSM_GROUPS, SSM_STATE, SSM_GROUP
    tr = lambda x: x.transpose(0, 2, 1)
    lanes_l = lambda x: jnp.tile(x, (1, 1, L))
    ins = [A_re[:, :, None], A_im[:, :, None], A_re[:, None, :], A_im[:, None, :],
           log_dt[:, None, None], tr(B_re), tr(B_im),
           lanes_l(B_re), lanes_l(B_im), lanes_l(tr(C_re)), lanes_l(tr(C_im))]
    two = lambda a: pl.BlockSpec((2,) + a.shape[1:], lambda p: (p, 0, 0))
    one = lambda r, c: pl.BlockSpec((1, r, c), lambda p: (p, 0, 0))
    outs = [((G, L * C, L * C), BF16), ((G // 2, 2 * N, 2 * L * C), BF16),
            ((G // 2, 2 * N, 2 * L * C), BF16), ((G // 2, 2 * N, 2 * L * C), BF16),
            ((G // 2, 2 * N, 2 * L * C), BF16), ((G // 2, 2, 2 * N), F32),
            ((G, C, N), F32), ((G, C, N), F32), ((G, 1, N), F32), ((G, 1, N), F32)]
    out_specs = [pl.BlockSpec((2, L * C, L * C), lambda p: (p, 0, 0))] \
        + [one(2 * N, 2 * L * C)] * 4 + [one(2, 2 * N)] \
        + [pl.BlockSpec((2,) + s[1:], lambda p: (p, 0, 0)) for s, _ in outs[6:]]
    res = pl.pallas_call(
        _s5_prep_kernel,
        grid=(G // 2,),
        in_specs=[two(a) for a in ins],
        out_specs=out_specs,
        out_shape=[jax.ShapeDtypeStruct(s, d) for s, d in outs],
        compiler_params=_params(("arbitrary",)),
        name="s5_prep",
    )(*ins)
    names = ('m', 'pre', 'pim', 'cre', 'cim', 'a16', 'brt', 'bit', 'arr', 'air')
    return dict(zip(names, res))


S5_GQ = LANES // SSM_GROUP


def _s5_prompt_kernel(u_ref, m_ref, pre_ref, pim_ref, cre_ref, cim_ref, a16_ref,
                      y_ref, xr_ref, xi_ref,
                      u2_ref, bre_ref, bim_ref, xpr_ref, xpi_ref, *, nb, nch):
    L, C, GQ = S5_CHUNK, SSM_GROUP, S5_GQ
    npair = GQ // 2

    def relayout_in(b, carry):
        uts = [u_ref[b, pl.ds(j, nch, stride=L), :].T for j in range(L)]
        for g in range(GQ):
            for hf in range(2):
                xt = jnp.concatenate([ut[g * C:(g + 1) * C, :]
                                      for ut in uts[hf * L // 2:(hf + 1) * L // 2]], axis=0)
                u2_ref[g, hf, pl.ds(b, nch, stride=nb), :] = xt.T
        return carry

    u2 = lambda g: jnp.concatenate([u2_ref[g, 0], u2_ref[g, 1]], axis=1)

    lax.fori_loop(0, nb, relayout_in, 0)

    for m in range(npair):
        ub2 = jnp.concatenate([u2(2 * m), u2(2 * m + 1)], axis=1).astype(BF16)
        bre_ref[m] = _dot_nt(ub2, pre_ref[m])
        bim_ref[m] = _dot_nt(ub2, pim_ref[m])

    ars = [a16_ref[m, 0:1, :] for m in range(npair)]
    ais = [a16_ref[m, 1:2, :] for m in range(npair)]

    def step(cidx, carry):
        o = pl.multiple_of(cidx * nb, nb)
        out = []
        for m in range(npair):
            xr, xi = carry[2 * m], carry[2 * m + 1]
            xpr_ref[m, pl.ds(o, nb), :] = xr
            xpi_ref[m, pl.ds(o, nb), :] = xi
            out.append(ars[m] * xr - ais[m] * xi + bre_ref[m, pl.ds(o, nb), :])
            out.append(ars[m] * xi + ais[m] * xr + bim_ref[m, pl.ds(o, nb), :])
        return tuple(out)

    z = jnp.zeros((nb, LANES), F32)
    fin = lax.fori_loop(0, nch, step, (z,) * GQ)
    for m in range(npair):
        xr_ref[m] = fin[2 * m]
        xi_ref[m] = fin[2 * m + 1]

    for m in range(npair):
        ys = _dot(xpr_ref[m], cre_ref[m]) + _dot(xpi_ref[m], cim_ref[m])
        for k in range(2):
            g = 2 * m + k
            yg = _dot(u2(g), m_ref[g]) + ys[:, k * L * C:(k + 1) * L * C]
            u2_ref[g, 0] = yg[:, :LANES]
            u2_ref[g, 1] = yg[:, LANES:]

    def relayout_out(b, carry):
        for hf in range(2):
            yts = [u2_ref[g, hf, pl.ds(b, nch, stride=nb), :].T for g in range(GQ)]
            for i in range(L // 2):
                yt = jnp.concatenate([t_[i * C:(i + 1) * C, :] for t_ in yts], axis=0)
                y_ref[b, pl.ds(hf * L // 2 + i, nch, stride=L), :] = yt.T
        return carry

    lax.fori_loop(0, nb, relayout_out, 0)


def _s5_prompt(u, tabs, nb, t):
    L, G, N, C = S5_CHUNK, SSM_GROUPS, SSM_STATE, SSM_GROUP
    nch = t // L
    nc = nch * nb
    nq = SSM_WIDTH // LANES
    npair = S5_GQ // 2
    spec = lambda blk: pl.BlockSpec(blk, lambda q: (q, 0, 0))
    y, xr, xi = pl.pallas_call(
        functools.partial(_s5_prompt_kernel, nb=nb, nch=nch),
        grid=(nq,),
        in_specs=[pl.BlockSpec((nb, t, LANES), lambda q: (0, 0, q), pipeline_mode=pl.Buffered(1)),
                  spec((S5_GQ, L * C, L * C)),
                  spec((npair, 2 * N, 2 * L * C)), spec((npair, 2 * N, 2 * L * C)),
                  spec((npair, 2 * N, 2 * L * C)), spec((npair, 2 * N, 2 * L * C)),
                  spec((npair, 2, 2 * N))],
        out_specs=[pl.BlockSpec((nb, t, LANES), lambda q: (0, 0, q)),
                   spec((npair, nb, 2 * N)), spec((npair, nb, 2 * N))],
        out_shape=[jax.ShapeDtypeStruct((nb, t, SSM_WIDTH), F32),
                   jax.ShapeDtypeStruct((G // 2, nb, 2 * N), F32),
                   jax.ShapeDtypeStruct((G // 2, nb, 2 * N), F32)],
        scratch_shapes=[pltpu.VMEM((S5_GQ, 2, nc, LANES), F32)]
        + [pltpu.VMEM((npair, nc, 2 * N), F32) for _ in range(4)],
        compiler_params=_params(("arbitrary",)),
        name="s5_prompt",
    )(u.reshape(nb, t, SSM_WIDTH), tabs['m'], tabs['pre'], tabs['pim'], tabs['cre'], tabs['cim'],
      tabs['a16'])
    unpair = lambda x: x.reshape(G // 2, nb, 2, N).transpose(1, 0, 2, 3).reshape(nb, G, N)
    return y.reshape(nb * t, SSM_WIDTH), unpair(xr), unpair(xi)


def _s5_step_kernel(u_ref, xr_ref, xi_ref, ar_ref, ai_ref, brt_ref, bit_ref, cr_ref, ci_ref,
                    y_ref, nr_ref, ni_ref, bd_ref):
    G, N, C = SSM_GROUPS, SSM_STATE, SSM_GROUP

    def block_diag(slot, src_ref):
        bd_ref[slot] = jnp.zeros((G * C, G * N), BF16)
        for g in range(G):
            bd_ref[slot, g * C:(g + 1) * C, g * N:(g + 1) * N] = src_ref[g].astype(BF16)
        return bd_ref[slot]

    ub = u_ref[...].astype(BF16)
    xr, xi, ar, ai = xr_ref[...], xi_ref[...], ar_ref[...], ai_ref[...]
    nr = ar * xr - ai * xi + jnp.dot(ub, block_diag(0, brt_ref), preferred_element_type=F32)
    ni = ar * xi + ai * xr + jnp.dot(ub, block_diag(1, bit_ref), preferred_element_type=F32)
    nr_ref[...] = nr
    ni_ref[...] = ni
    y_ref[...] = _dot_nt(nr, block_diag(2, cr_ref)) - _dot_nt(ni, block_diag(3, ci_ref))


def _s5_step(u, st_re, st_im, tabs, C_re, C_im):
    G, N, C = SSM_GROUPS, SSM_STATE, SSM_GROUP
    n = u.shape[0]
    ops = (u, st_re.reshape(n, G * N), st_im.reshape(n, G * N),
           tabs['arr'].reshape(1, G * N), tabs['air'].reshape(1, G * N),
           tabs['brt'], tabs['bit'], C_re, C_im)
    y, nr, ni = pl.pallas_call(
        _s5_step_kernel,
        grid=(1,),
        in_specs=[_full_spec(o.shape) for o in ops],
        out_specs=[_full_spec((n, SSM_WIDTH)), _full_spec((n, G * N)), _full_spec((n, G * N))],
        out_shape=[jax.ShapeDtypeStruct((n, SSM_WIDTH), F32),
                   jax.ShapeDtypeStruct((n, G * N), F32),
                   jax.ShapeDtypeStruct((n, G * N), F32)],
        scratch_shapes=[pltpu.VMEM((4, G * C, G * N), BF16)],
        compiler_params=_params(("arbitrary",)),
        name="s5_step",
    )(*ops)
    return y, nr.reshape(n, G, N), ni.reshape(n, G, N)


def _tail_kernel(*refs, tm, nsub, tiles_per_seq, decode, final):
    if decode:
        (x_ref, rw_ref, ys_ref, u_ref, gl_ref, pe_ref, st2_ref, st1_ref,
         dsk_ref, wglu_ref, wout_ref, ln2_ref, wfi_ref, cw_ref, cb_ref, wfo_ref,
         ln3_ref, wpg_ref, wple_ref, fin_ref, y_ref, conv_ref) = refs
    else:
        (x_ref, rw_ref, ys_ref, u_ref, gl_ref, pe_ref,
         dsk_ref, wglu_ref, wout_ref, ln2_ref, wfi_ref, cw_ref, cb_ref, wfo_ref,
         ln3_ref, wpg_ref, wple_ref, fin_ref, y_ref, conv_ref, carry_ref) = refs

    sub = tm // nsub
    cw = cw_ref[...]
    last2 = {}
    if not decode:
        @pl.when(pl.program_id(0) % tiles_per_seq == 0)
        def _():
            carry_ref[...] = jnp.zeros_like(carry_ref)

        last2[-1] = carry_ref[...]

    def row_block(s):
        rs = slice(s * sub, (s + 1) * sub)
        x = x_ref[rs, :]
        z = _gelu(ys_ref[rs, :] + dsk_ref[...] * u_ref[rs, :])
        zz = _dot(z, wglu_ref[...])
        yield
        s5 = zz[:, :D_MODEL] * _sigmoid(zz[:, D_MODEL:])
        gl = gl_ref[rs, :]
        merged = _sigmoid(gl[:, :D_MODEL]) * rw_ref[rs, :] + _sigmoid(gl[:, D_MODEL:]) * s5
        x = x + _dot(merged, wout_ref[...])
        yield
        h2 = _rmsnorm(x, ln2_ref[...]).astype(BF16)
        a_up = jnp.dot(h2, wfi_ref[:, :D_FF], preferred_element_type=F32)
        b_up = jnp.dot(h2, wfi_ref[:, D_FF:], preferred_element_type=F32)
        if decode:
            prev2 = st2_ref[rs, :]
            prev1 = st1_ref[rs, :]
            conv_ref[rs, :] = a_up
        else:
            last2[s] = a_up[sub - 2:sub, :]
        yield
        if not decode:
            rows = lax.broadcasted_iota(jnp.int32, (sub, 1), 0)
            c0 = last2[s - 1][0:1, :]
            c1 = last2[s - 1][1:2, :]
            prev1 = jnp.where(rows == 0, c1, pltpu.roll(a_up, 1, axis=0))
            prev2 = jnp.where(rows == 0, c0,
                              jnp.where(rows == 1, c1, pltpu.roll(a_up, 2, axis=0)))
        a_conv = cw[0:1] * prev2 + cw[1:2] * prev1 + cw[2:3] * a_up + cb_ref[...]
        x = x + _dot(_gelu(a_conv) * b_up, wfo_ref[...])
        yield
        pg = _sigmoid(_dot(_rmsnorm(x, ln3_ref[...]), wpg_ref[...]))
        x = x + pg * _dot(pe_ref[rs, :], wple_ref[...])
        y_ref[rs, :] = _rmsnorm(x, fin_ref[...]) if final else x
        yield

    blocks = [row_block(s) for s in range(nsub)]
    for _ in range(5):
        for blk in blocks:
            next(blk)
    if not decode:
        carry_ref[...] = last2[nsub - 1]
        conv_ref[0] = last2[nsub - 1]


def _tail(x2d, rw, ys, u, gl, pe, st_conv, D_skip, wglu, wout, ln2_g, wfi, conv_w, conv_b, wfo,
          ln3_g, wpg, wple, final_g, *, tm, tiles_per_seq, decode, final):
    n = x2d.shape[0]
    row = lambda w: pl.BlockSpec((tm, w), lambda i: (i, 0))
    consts = [D_skip.reshape(1, SSM_WIDTH), wglu, wout, ln2_g.reshape(1, D_MODEL), wfi,
              conv_w, conv_b.reshape(1, D_FF), wfo, ln3_g.reshape(1, D_MODEL), wpg, wple,
              final_g.reshape(1, D_MODEL)]
    acts = [x2d, rw, ys, u, gl, pe]
    in_specs = [row(a.shape[1]) for a in acts]
    scratch = []
    if decode:
        acts += [st_conv[:, 0], st_conv[:, 1]]
        in_specs += [row(D_FF), row(D_FF)]
        conv_spec = row(D_FF)
        conv_shape = (n, D_FF)
    else:
        nseq = n // (tm * tiles_per_seq)
        conv_spec = pl.BlockSpec((1, 2, D_FF), lambda i: (i // tiles_per_seq, 0, 0))
        conv_shape = (nseq, 2, D_FF)
        scratch = [pltpu.VMEM((2, D_FF), F32)]
    in_specs += [pl.BlockSpec(c.shape, lambda i, nd=c.ndim: (0,) * nd,
                              pipeline_mode=pl.Buffered(1)) for c in consts]
    y, conv = pl.pallas_call(
        functools.partial(_tail_kernel, tm=tm, nsub=1 if decode else TAIL_ROW_BLOCKS,
                          tiles_per_seq=tiles_per_seq, decode=decode,
                          final=final),
        grid=(n // tm,),
        in_specs=in_specs,
        out_specs=[row(D_MODEL), conv_spec],
        out_shape=[jax.ShapeDtypeStruct((n, D_MODEL), F32),
                   jax.ShapeDtypeStruct(conv_shape, F32)],
        scratch_shapes=scratch,
        compiler_params=_params(("arbitrary",)),
        name="tail_decode" if decode else "tail_prompt",
    )(*acts, *consts)
    if decode:
        conv = jnp.stack([st_conv[:, 1], conv], axis=1)
    return y, conv


def _layer(x, pe, states, W, *, decode, final, tm_proj, tt, tm_tail):
    nb, t, _ = x.shape
    n = nb * t
    x2d = x.reshape(n, D_MODEL)
    prw, u, gl = _proj(x2d, W['ln1_g'], W['w_in'], tm_proj)
    if decode:
        st_shift, st_wkv, st_re, st_im, st_conv = states
        rw, s_new = _rwkv_step(prw, st_shift, st_wkv, W['rowp'], W['wla'], W['g2'], W['eones'],
                               W['w_rw_out'])
        ys, xr, xi = _s5_step(u, st_re, st_im, W['s5'], W['C_re'], W['C_im'])
        new_shift = prw
    else:
        st_conv = None
        rw, s_new = _rwkv_prompt(prw, W['rowp'], W['wla'], W['g2'], W['eones'], W['w_rw_out'],
                                 nb, t, tt)
        ys, xr, xi = _s5_prompt(u, W['s5'], nb, t)
        new_shift = prw.reshape(nb, t, SHIFT_WIDTH)[:, -1]
    y, new_conv = _tail(x2d, rw, ys, u, gl, pe.reshape(n, PLE_DIM), st_conv, W['D_skip'],
                        W['w_glu'], W['w_out'], W['ln2_g'], W['w_ffn_in'], W['conv_w'],
                        W['conv_b'], W['w_ffn_out'], W['ln3_g'], W['w_ple_gate'], W['w_ple'],
                        W['final_g'], tm=tm_tail, tiles_per_seq=max(t // tm_tail, 1),
                        decode=decode, final=final)
    return y.reshape(nb, t, D_MODEL), (new_shift, s_new, xr, xi, new_conv)


def kernel(x_prompt, x_sample, p_prompt, p_sample, state_shift, state_wkv, state_ssm_re, state_ssm_im, state_conv, ln1_g, w_in, mu_shift, w0, w2, a0, a2, g2, k_k, k_a, r_k, lnx_g, lnx_b, w_rw_out, A_re, A_im, log_dt, B_re, B_im, C_re, C_im, D_skip, w_glu, w_out, ln2_g, w_ffn_in, conv_w, conv_b, w_ffn_out, ln3_g, w_ple_gate, w_ple, final_g):
    depth = w_in.shape[0]
    xp, xs = x_prompt, x_sample
    pst = [[] for _ in range(5)]
    sst = [[] for _ in range(5)]
    for i in range(depth):
        wla, g2b, wo = _rwkv_weights(w2[i], a2[i], g2[i], w_rw_out[i])
        W = dict(
            ln1_g=ln1_g[i], w_in=w_in[i].astype(BF16),
            rowp=_row_params(mu_shift[i], w0[i], a0[i], k_k[i], k_a[i], r_k[i].reshape(-1),
                             lnx_g[i], lnx_b[i]),
            wla=wla, g2=g2b, eones=_rwkv_consts(), w_rw_out=wo,
            s5=_s5_prep(A_re[i], A_im[i], log_dt[i], B_re[i], B_im[i], C_re[i], C_im[i]),
            C_re=C_re[i], C_im=C_im[i], D_skip=D_skip[i],
            w_glu=w_glu[i].astype(BF16), w_out=w_out[i].astype(BF16), ln2_g=ln2_g[i],
            w_ffn_in=w_ffn_in[i].astype(BF16), conv_w=conv_w[i], conv_b=conv_b[i],
            w_ffn_out=w_ffn_out[i].astype(BF16), ln3_g=ln3_g[i],
            w_ple_gate=w_ple_gate[i].astype(BF16), w_ple=w_ple[i].astype(BF16), final_g=final_g)
        final = i == depth - 1
        tp = xp.shape[1]
        xp, sp = _layer(xp, p_prompt[i], None, W, decode=False, final=final,
                        tm_proj=min(TM_PROJ, tp), tt=min(TT_RWKV, tp), tm_tail=min(TM_TAIL, tp))
        ns = xs.shape[0]
        xs, ss = _layer(xs, p_sample[i],
                        (state_shift[i], state_wkv[i], state_ssm_re[i], state_ssm_im[i],
                         state_conv[i]),
                        W, decode=True, final=final, tm_proj=ns, tt=None, tm_tail=ns)
        for j in range(5):
            pst[j].append(sp[j])
            sst[j].append(ss[j])
    return (xp, xs,
            jnp.stack(pst[0]), jnp.stack(pst[1]), jnp.stack(pst[2]), jnp.stack(pst[3]),
            jnp.stack(pst[4]),
            jnp.stack(sst[0]), jnp.stack(sst[1]), jnp.stack(sst[2]), jnp.stack(sst[3]),
            jnp.stack(sst[4]))
```

```python
import functools
import math

import jax
import jax.numpy as jnp
import numpy as np
from jax import lax
from jax.experimental import pallas as pl
from jax.experimental.pallas import tpu as pltpu

F32 = jnp.float32
BF16 = jnp.bfloat16

D_MODEL = 1024
RW_HEADS = 8
HEAD_DIM = 64
RW_WIDTH = RW_HEADS * HEAD_DIM
LORA_W = 64
LORA_A = 64
LORA_G = 128
SHIFT_WIDTH = 3 * RW_WIDTH + LORA_W + LORA_A + LORA_G
SSM_WIDTH = 512
SSM_GROUP = 16
SSM_GROUPS = SSM_WIDTH // SSM_GROUP
SSM_STATE = 64
GATE_OFF = SHIFT_WIDTH + SSM_WIDTH
IN_WIDTH = SHIFT_WIDTH + SSM_WIDTH + 2 * D_MODEL
D_FF = 2816
PLE_DIM = 256
EPS = 1e-6
GN_EPS = 64e-5

LANES = 128
WKV_CHUNK = 64
S5_CHUNK = 16
N_PAIRS = RW_HEADS // 2
VMEM_LIMIT = 56 * 1024 * 1024
TM_PROJ = 512
TT_RWKV = 256
TM_TAIL = 256
RWKV_SEQ_GROUPS = 2
TAIL_ROW_BLOCKS = 2


def _dot(a, b):
    return jnp.dot(a.astype(BF16), b.astype(BF16), preferred_element_type=F32)


def _dot_nt(a, b):
    return lax.dot_general(a.astype(BF16), b.astype(BF16), (((1,), (1,)), ((), ())),
                           preferred_element_type=F32)


def _dot_tn(a, b):
    return lax.dot_general(a.astype(BF16), b.astype(BF16), (((0,), (0,)), ((), ())),
                           preferred_element_type=F32)


def _split3(x):
    h1 = x.astype(BF16)
    r1 = x - h1.astype(F32)
    h2 = r1.astype(BF16)
    h3 = (r1 - h2.astype(F32)).astype(BF16)
    return h1, h2, h3


def _dot_exact_lhs(a_bf16, x):
    h1, h2, h3 = _split3(x)
    d = lambda h: jnp.dot(a_bf16, h, preferred_element_type=F32)
    return d(h1) + d(h2) + d(h3)


def _sigmoid(x):
    return 1.0 / (1.0 + jnp.exp(-x))


def _gelu(x):
    c = math.sqrt(2.0 / math.pi)
    return 0.5 * x * (1.0 + jnp.tanh(c * (x + 0.044715 * (x * x * x))))


def _rmsnorm(x, g):
    return x * lax.rsqrt(jnp.mean(x * x, axis=-1, keepdims=True) + EPS) * g


def _full_spec(shape):
    nd = len(shape)
    return pl.BlockSpec(shape, lambda *_: (0,) * nd)


def _params(sem):
    return pltpu.CompilerParams(dimension_semantics=sem, vmem_limit_bytes=VMEM_LIMIT)


def _proj_kernel(x_ref, g_ref, w_ref, prw_ref, u_ref, gl_ref):
    h = _rmsnorm(x_ref[...], g_ref[...])
    prw_ref[...] = _dot(h, w_ref[:, :SHIFT_WIDTH])
    u_ref[...] = _dot(h, w_ref[:, SHIFT_WIDTH:GATE_OFF])
    gl_ref[...] = _dot(h, w_ref[:, GATE_OFF:])


def _proj(x2d, ln1_g, w_in_bf, tm):
    n = x2d.shape[0]
    row = lambda w: pl.BlockSpec((tm, w), lambda i: (i, 0))
    return pl.pallas_call(
        _proj_kernel,
        grid=(n // tm,),
        in_specs=[row(D_MODEL), _full_spec((1, D_MODEL)),
                  pl.BlockSpec((D_MODEL, IN_WIDTH), lambda i: (0, 0),
                               pipeline_mode=pl.Buffered(1))],
        out_specs=[row(SHIFT_WIDTH), row(SSM_WIDTH), row(2 * D_MODEL)],
        out_shape=[jax.ShapeDtypeStruct((n, SHIFT_WIDTH), F32),
                   jax.ShapeDtypeStruct((n, SSM_WIDTH), F32),
                   jax.ShapeDtypeStruct((n, 2 * D_MODEL), F32)],
        compiler_params=_params(("arbitrary",)),
        name="proj",
    )(x2d, ln1_g.reshape(1, D_MODEL), w_in_bf)


def _rwkv_token_math_stages(p, prev, mu, w0, a0, k_k, k_a, r_k, wla, g2, eones):
    xs = p + (prev - p) * mu
    r = xs[:, 0:RW_WIDTH]
    k = xs[:, RW_WIDTH:2 * RW_WIDTH]
    v = xs[:, 2 * RW_WIDTH:3 * RW_WIDTH]
    xwa = xs[:, 3 * RW_WIDTH:3 * RW_WIDTH + LORA_W + LORA_A]
    xg = xs[:, 3 * RW_WIDTH + LORA_W + LORA_A:]
    lane = lax.broadcasted_iota(jnp.int32, xwa.shape, 1)
    la = _dot(jnp.where(lane < LORA_W, jnp.tanh(xwa), xwa), wla)
    g = _dot(_sigmoid(xg), g2)
    yield None
    wpre = w0 + la[:, :RW_WIDTH]
    softplus = jnp.maximum(-wpre, 0.0) + jnp.log(1.0 + jnp.exp(-jnp.abs(wpre)))
    lw = -jnp.exp(-softplus - 0.5)
    a = _sigmoid(a0 + la[:, RW_WIDTH:])
    kkr = k * k_k
    ss = _dot(kkr * kkr, eones)
    kk = kkr / jnp.maximum(jnp.sqrt(ss), 1e-12)
    kp = k * (1.0 + (a - 1.0) * k_a)
    bonus = _dot(r * kp * r_k, eones) * v
    yield r, v, lw, a, g, kk, kp, bonus


def _rwkv_token_math(*args):
    *_, out = _rwkv_token_math_stages(*args)
    return out


def _rwkv_output(y, bonus, g, lnx_g, lnx_b, eones, w_o):
    emean = eones * (1.0 / HEAD_DIM)
    mean = _dot(y, emean)
    d = y - mean
    var = _dot(d * d, emean)
    yn = d * lax.rsqrt(var + GN_EPS) * lnx_g + lnx_b
    return _dot((yn + bonus) * g, w_o)


def _head_masks(shape):
    lane = lax.broadcasted_iota(jnp.int32, shape, 1)
    return lane < HEAD_DIM, lane >= HEAD_DIM


def _stack_heads(x, m_a, m_b):
    return jnp.concatenate([jnp.where(m_a, x, 0.0), jnp.where(m_b, x, 0.0)], axis=0)


def _rwkv_prompt_kernel(p_ref, mu_ref, w0_ref, a0_ref, kk_ref, ka_ref, rk_ref, lg_ref, lb_ref,
                        wla_ref, g2_ref, e_ref, ltri_ref, wo_ref,
                        out_ref, sout_ref,
                        carry_ref, sp_ref, y_ref, pend_ref, gend_ref, new_ref, gnew_ref,
                        *, tt, nt, nh):
    s = pl.program_id(0)
    nch = tt // WKV_CHUNK
    eones = e_ref[...]
    halves = range(nh)
    pairs = range(N_PAIRS)
    rows_of = lambda ch: slice(ch * WKV_CHUNK, (ch + 1) * WKV_CHUNK)
    lanes_of = lambda j: slice(j * LANES, (j + 1) * LANES)

    @pl.when(s == 0)
    def _():
        pend_ref[...] = jnp.zeros_like(pend_ref)
        gend_ref[...] = jnp.zeros_like(gend_ref)
        sp_ref[...] = jnp.zeros_like(sp_ref)

    @pl.when(s % nt == 0)
    def _():
        carry_ref[...] = jnp.zeros_like(carry_ref)

    @pl.when(s % nt == 1 % nt)
    def _():
        sp_ref[...] = jnp.zeros_like(sp_ref)

    rows = lax.broadcasted_iota(jnp.int32, (tt, 1), 0)
    token_math = []
    for h in halves:
        p = p_ref[h]
        prev = jnp.where(rows == 0, carry_ref[h], pltpu.roll(p, 1, axis=0))
        carry_ref[h] = p[tt - 1:tt, :]
        token_math.append(_rwkv_token_math_stages(
            p, prev, mu_ref[...], w0_ref[...], a0_ref[...], kk_ref[...], ka_ref[...],
            rk_ref[...], wla_ref[...], g2_ref[...], eones))
        next(token_math[h])

    m_a, m_b = _head_masks((WKV_CHUNK, LANES))
    ri = lax.broadcasted_iota(jnp.int32, (2 * WKV_CHUNK, 2 * WKV_CHUNK), 0)
    ci = lax.broadcasted_iota(jnp.int32, (2 * WKV_CHUNK, 2 * WKV_CHUNK), 1)
    same_head = (ri >= WKV_CHUNK) == (ci >= WKV_CHUNK)
    strict = same_head & (ri > ci)
    incl = same_head & (ri >= ci)
    blocks = [(h, ch, j) for h in halves for ch in range(nch) for j in pairs]
    blk = lambda i, h, ch, j: pend_ref[h, i, rows_of(ch), lanes_of(j)]
    stack = lambda i, h, ch, j: _stack_heads(blk(i, h, ch, j), m_a, m_b)
    vss = {b: stack(4, *b) for b in blocks}
    gms = {b: _dot_nt(jnp.concatenate([stack(0, *b), stack(1, *b)], axis=0),
                      jnp.concatenate([stack(2, *b), stack(3, *b)], axis=0)) for b in blocks}
    n_abs = {b: jnp.where(strict, gms[b][:LANES, :LANES], 0.0) for b in blocks}
    wvs = {b: _dot(jnp.where(strict, gms[b][:LANES, LANES:], 0.0), vss[b]) for b in blocks}
    a_ys = {b: jnp.concatenate([jnp.where(incl, gms[b][LANES:, :LANES], 0.0),
                                jnp.where(incl, gms[b][LANES:, LANES:], 0.0)], axis=1)
            for b in blocks}

    tm = [next(token_math[h]) for h in halves]
    cs = [_dot_exact_lhs(ltri_ref[...], tm[h][2]) for h in halves]

    eye = (ri == ci).astype(F32)
    tinvs = {b: eye + n_abs[b] for b in blocks}
    pows = n_abs
    for _ in range(5):
        pows = {b: _dot(pows[b], pows[b]).astype(BF16) for b in blocks}
        tinvs = {b: tinvs[b] + _dot(pows[b], tinvs[b]) for b in blocks}

    for h in halves:
        r, v_n, lw, a, g_n, kk, kp, bonus_n = tm[h]
        c = cs[h]
        einv = jnp.exp(-c)
        bvec = kk * a
        new_ref[h, 0] = -kk * jnp.exp(c - lw)
        new_ref[h, 1] = r * jnp.exp(c)
        new_ref[h, 2] = bvec * einv
        new_ref[h, 3] = kp * einv
        new_ref[h, 4] = v_n
        for ch in range(nch):
            c_end = c[(ch + 1) * WKV_CHUNK - 1:(ch + 1) * WKV_CHUNK, :]
            e_rest = jnp.exp(c_end - c[rows_of(ch), :])
            gnew_ref[h, ch] = jnp.exp(c_end)
            new_ref[h, 5, rows_of(ch), :] = bvec[rows_of(ch), :] * e_rest
            new_ref[h, 6, rows_of(ch), :] = kp[rows_of(ch), :] * e_rest
        new_ref[h, 7] = bonus_n
        new_ref[h, 8] = g_n

    sps = {(h, j): sp_ref[h, j] for h in halves for j in pairs}
    for ch in range(nch):
        hj = [(h, j) for h in halves for j in pairs]
        arss = {(h, j): _dot_nt(jnp.concatenate([blk(0, h, ch, j), blk(1, h, ch, j)], axis=0),
                                sps[h, j]) for h, j in hj}
        uvs = {(h, j): jnp.concatenate(
            [_dot(tinvs[h, ch, j],
                  _stack_heads(arss[h, j][:WKV_CHUNK], m_a, m_b) + wvs[h, ch, j]),
             vss[h, ch, j]], axis=0) for h, j in hj}
        yss = {(h, j): _dot(a_ys[h, ch, j], uvs[h, j]) for h, j in hj}
        incs = {(h, j): _dot_tn(uvs[h, j], jnp.concatenate(
            [stack(5, h, ch, j), stack(6, h, ch, j)], axis=0)) for h, j in hj}
        for h, j in hj:
            y_ref[h, rows_of(ch), lanes_of(j)] = (yss[h, j][:WKV_CHUNK] + yss[h, j][WKV_CHUNK:]
                                                  + arss[h, j][WKV_CHUNK:])
            sps[h, j] = sps[h, j] * gend_ref[h, ch][:, lanes_of(j)] + incs[h, j]
    for h in halves:
        for j in pairs:
            sp_ref[h, j] = sps[h, j]
        out_ref[h] = _rwkv_output(y_ref[h], pend_ref[h, 7], pend_ref[h, 8], lg_ref[...],
                                  lb_ref[...], eones, wo_ref[...])

    pend_ref[...] = new_ref[...]
    gend_ref[...] = gnew_ref[...]

    @pl.when((s % nt == 0) & (s > 0))
    def _():
        for h in halves:
            for j in pairs:
                sp = sp_ref[h, j]
                sout_ref[h, 0, 2 * j] = sp[:HEAD_DIM, :HEAD_DIM]
                sout_ref[h, 0, 2 * j + 1] = sp[HEAD_DIM:, HEAD_DIM:]


def _rwkv_consts():
    idx = np.arange(RW_WIDTH)
    return jnp.asarray(idx[:, None] // HEAD_DIM == idx[None, :] // HEAD_DIM, dtype=BF16)


def _rwkv_weights(w2, a2, g2, w_rw_out):
    wla = jnp.zeros((LORA_W + LORA_A, 2 * RW_WIDTH), F32)
    wla = wla.at[:LORA_W, :RW_WIDTH].set(w2).at[LORA_W:, RW_WIDTH:].set(a2)
    return wla.astype(BF16), g2.astype(BF16), w_rw_out.astype(BF16)


def _row_params(mu_shift, w0, a0, k_k, k_a, r_k, lnx_g, lnx_b):
    return (mu_shift.reshape(1, SHIFT_WIDTH), w0.reshape(1, RW_WIDTH), a0.reshape(1, RW_WIDTH),
            k_k.reshape(1, RW_WIDTH), k_a.reshape(1, RW_WIDTH), r_k.reshape(1, RW_WIDTH),
            lnx_g.reshape(1, RW_WIDTH), lnx_b.reshape(1, RW_WIDTH))


def _rwkv_prompt(prw, rowp, wla, g2b, eones, wo, nb, t, tt):
    nt = t // tt
    idx = np.arange(tt)
    ltri = jnp.asarray((idx[:, None] // WKV_CHUNK == idx[None, :] // WKV_CHUNK)
                       & (idx[:, None] >= idx[None, :]), dtype=BF16)
    consts = list(rowp) + [wla, g2b, eones, ltri, wo]
    nh = RWKV_SEQ_GROUPS if nb % RWKV_SEQ_GROUPS == 0 else 1
    ntiles = (nb // nh) * nt
    nch = tt // WKV_CHUNK
    done = lambda s: jnp.maximum(s - 1, 0)
    out, s_new = pl.pallas_call(
        functools.partial(_rwkv_prompt_kernel, tt=tt, nt=nt, nh=nh),
        grid=(ntiles + 1,),
        in_specs=[pl.BlockSpec((nh, tt, SHIFT_WIDTH),
                               lambda s: (0, jnp.minimum(s, ntiles - 1), 0))]
        + [_full_spec(c.shape) for c in consts],
        out_specs=[pl.BlockSpec((nh, tt, D_MODEL), lambda s: (0, done(s), 0)),
                   pl.BlockSpec((nh, 1, RW_HEADS, HEAD_DIM, HEAD_DIM),
                                lambda s: (0, done(s) // nt, 0, 0, 0))],
        out_shape=[jax.ShapeDtypeStruct((nh, nb * t // nh, D_MODEL), F32),
                   jax.ShapeDtypeStruct((nh, nb // nh, RW_HEADS, HEAD_DIM, HEAD_DIM), F32)],
        scratch_shapes=[pltpu.VMEM((nh, 1, SHIFT_WIDTH), F32),
                        pltpu.VMEM((nh, N_PAIRS, LANES, LANES), F32),
                        pltpu.VMEM((nh, tt, RW_WIDTH), F32),
                        pltpu.VMEM((nh, 9, tt, RW_WIDTH), F32),
                        pltpu.VMEM((nh, nch, 1, RW_WIDTH), F32),
                        pltpu.VMEM((nh, 9, tt, RW_WIDTH), F32),
                        pltpu.VMEM((nh, nch, 1, RW_WIDTH), F32)],
        compiler_params=_params(("arbitrary",)),
        name="rwkv_prompt",
    )(prw.reshape(nh, nb * t // nh, SHIFT_WIDTH), *consts)
    return (out.reshape(nb * t, D_MODEL),
            s_new.reshape(nb, RW_HEADS, HEAD_DIM, HEAD_DIM))


def _rwkv_step_kernel(p_ref, prev_ref, s_ref, mu_ref, w0_ref, a0_ref, kk_ref, ka_ref, rk_ref,
                      lg_ref, lb_ref, wla_ref, g2_ref, e_ref, wo_ref,
                      out_ref, sout_ref,
                      qt_ref, wt_ref, rt_ref, bt_ref, kt_ref, vt_ref, yt_ref, g_ref, bonus_ref):
    hp = pl.program_id(0)

    @pl.when(hp == 0)
    def _():
        r, v, lw, a, g, kk, kp, bonus = _rwkv_token_math(
            p_ref[...], prev_ref[...], mu_ref[...], w0_ref[...], a0_ref[...], kk_ref[...],
            ka_ref[...], rk_ref[...], wla_ref[...], g2_ref[...], e_ref[...])
        qt_ref[...] = (-kk).T
        wt_ref[...] = jnp.exp(lw).T
        rt_ref[...] = r.T
        bt_ref[...] = (kk * a).T
        kt_ref[...] = kp.T
        vt_ref[...] = v.T
        g_ref[...] = g
        bonus_ref[...] = bonus

    def per_value_row(vi, carry):
        for hl in range(2):
            row0 = pl.multiple_of((2 * hp + hl) * HEAD_DIM, HEAD_DIM)
            ks = pl.ds(row0, HEAD_DIM)
            s = s_ref[hl, vi]
            sa = jnp.sum(s * qt_ref[ks, :], axis=0, keepdims=True)
            s_new = (s * wt_ref[ks, :] + sa * bt_ref[ks, :]
                     + vt_ref[pl.ds(row0 + vi, 1), :] * kt_ref[ks, :])
            yt_ref[pl.ds(row0 + vi, 1), :] = jnp.sum(s_new * rt_ref[ks, :], axis=0, keepdims=True)
            sout_ref[hl, vi] = s_new
        return carry

    lax.fori_loop(0, HEAD_DIM, per_value_row, 0, unroll=2)

    @pl.when(hp == pl.num_programs(0) - 1)
    def _():
        out_ref[...] = _rwkv_output(yt_ref[...].T, bonus_ref[...], g_ref[...], lg_ref[...],
                                    lb_ref[...], e_ref[...], wo_ref[...])


def _rwkv_step(prw, prev, state, rowp, wla, g2b, eones, wo):
    n = prw.shape[0]
    consts = list(rowp) + [wla, g2b, eones, wo]
    state_t = jnp.transpose(state, (1, 2, 3, 0))
    sspec = pl.BlockSpec((2, HEAD_DIM, HEAD_DIM, n), lambda hp: (hp, 0, 0, 0))
    out, s_new_t = pl.pallas_call(
        _rwkv_step_kernel,
        grid=(N_PAIRS,),
        in_specs=[_full_spec((n, SHIFT_WIDTH)), _full_spec((n, SHIFT_WIDTH)), sspec]
        + [_full_spec(c.shape) for c in consts],
        out_specs=[_full_spec((n, D_MODEL)), sspec],
        out_shape=[jax.ShapeDtypeStruct((n, D_MODEL), F32),
                   jax.ShapeDtypeStruct((RW_HEADS, HEAD_DIM, HEAD_DIM, n), F32)],
        scratch_shapes=[pltpu.VMEM((RW_WIDTH, n), F32) for _ in range(7)]
        + [pltpu.VMEM((n, RW_WIDTH), F32) for _ in range(2)],
        compiler_params=_params(("arbitrary",)),
        name="rwkv_step",
    )(prw, prev, state_t, *consts)
    return out, jnp.transpose(s_new_t, (3, 0, 1, 2))


def _zoh(lr, li, dt):
    mag = jnp.exp(lr * dt)
    ar, ai = mag * jnp.cos(li * dt), mag * jnp.sin(li * dt)
    den = lr * lr + li * li
    fr = ((ar - 1.0) * lr + ai * li) / den
    fi = (ai * lr - (ar - 1.0) * li) / den
    return ar, ai, fr, fi


def _dot_split(a, b):
    a1 = a.astype(BF16)
    a2 = (a - a1.astype(F32)).astype(BF16)
    b1 = b.astype(BF16)
    b2 = (b - b1.astype(F32)).astype(BF16)
    d = lambda x, y: jnp.dot(x, y, preferred_element_type=F32)
    return d(a1, b1) + (d(a1, b2) + d(a2, b1))


def _s5_prep_kernel(lrc_ref, lic_ref, lrr_ref, lir_ref, ldt_ref, bt_ref, wide_ref,
                    m_ref, pre_ref, pim_ref, cre_ref, cim_ref, a16_ref,
                    brt_ref, bit_ref, arr_ref, air_ref):
    L, C, N = S5_CHUNK, SSM_GROUP, SSM_STATE
    lag = lax.broadcasted_iota(jnp.int32, (1, LANES), 1).astype(F32)
    sel_row = lax.broadcasted_iota(jnp.int32, (LANES, L * C), 0)
    sel_lag = lax.shift_right_logical(lax.broadcasted_iota(jnp.int32, (LANES, L * C), 1), 4)
    pick = lambda cond: jnp.where(cond, 1.0, 0.0).astype(BF16)
    r_same, r_next, r_rev = (pick(sel_row == sel_lag), pick(sel_row == sel_lag + 1),
                             pick(sel_row == (L - 1) - sel_lag))

    def spread(e, r):
        h1, h2, h3 = _split3(e)
        d = lambda h: jnp.dot(h, r, preferred_element_type=F32)
        return d(h1) + d(h2) + d(h3)

    lane2 = lax.broadcasted_iota(jnp.int32, (C, L * C), 1)
    zero = jnp.zeros((N, L * C), F32)
    a16r, a16i = [], []
    for k in range(2):
        dt = jnp.exp(ldt_ref[k])
        lrc, lic = lrc_ref[k], lic_ref[k]
        mag = jnp.exp(lrc * dt * lag)
        er, ei = mag * jnp.cos(lic * dt * lag), mag * jnp.sin(lic * dt * lag)
        e0r, e0i = spread(er, r_same), spread(ei, r_same)
        e1r, e1i = spread(er, r_next), spread(ei, r_next)
        e2r, e2i = spread(er, r_rev), spread(ei, r_rev)
        _, _, frc, fic = _zoh(lrc, lic, dt)
        brl, bil, crl, cil = (wide_ref[i, k] for i in range(4))
        bbr = frc * brl - fic * bil
        bbi = frc * bil + fic * brl
        halves = lambda x: jnp.concatenate([x, zero] if k == 0 else [zero, x], axis=1)
        rows = slice(k * N, (k + 1) * N)
        pre_ref[0, rows, :] = halves(e2r * bbr - e2i * bbi).astype(BF16)
        pim_ref[0, rows, :] = halves(e2r * bbi + e2i * bbr).astype(BF16)
        cre_ref[0, rows, :] = halves(crl * e1r - cil * e1i).astype(BF16)
        cim_ref[0, rows, :] = halves(-(crl * e1i + cil * e1r)).astype(BF16)

        lrr, lir = lrr_ref[k], lir_ref[k]
        arr, air, frr, fir = _zoh(lrr, lir, dt)
        brt = frr * bt_ref[0, k] - fir * bt_ref[1, k]
        bit = frr * bt_ref[1, k] + fir * bt_ref[0, k]
        brt_ref[k], bit_ref[k], arr_ref[k], air_ref[k] = brt, bit, arr, air
        m16 = jnp.exp(lrr * dt * L)
        a16r.append(m16 * jnp.cos(lir * dt * L))
        a16i.append(m16 * jnp.sin(lir * dt * L))
        klt = (_dot_split(brt, crl * e0r - cil * e0i)
               - _dot_split(bit, crl * e0i + cil * e0r))
        for j in range(L):
            blk = klt if j == 0 else jnp.where(lane2 >= j * C, pltpu.roll(klt, j * C, axis=1), 0.0)
            m_ref[k, j * C:(j + 1) * C, :] = blk.astype(BF16)
    a16_ref[0, 0:1, :] = jnp.concatenate(a16r, axis=1)
    a16_ref[0, 1:2, :] = jnp.concatenate(a16i, axis=1)


def _s5_prep(A_re, A_im, log_dt, B_re, B_im, C_re, C_im):
    L, G, N, C = S5_CHUNK, SSM_GROUPS, SSM_STATE, SSM_GROUP
    bb = jnp.stack([B_re, B_im])
    cct = jnp.stack([C_re, C_im]).transpose(0, 1, 3, 2)
    wide = jnp.tile(jnp.concatenate([bb, cct]), (1, 1, 1, L))
    ins = [A_re[:, :, None], A_im[:, :, None], A_re[:, None, :], A_im[:, None, :],
           log_dt[:, None, None], bb.transpose(0, 1, 3, 2), wide]

    def two(a):
        lead = a.shape[:a.ndim - 3]
        return pl.BlockSpec(lead + (2,) + a.shape[-2:], lambda p: (0,) * len(lead) + (p, 0, 0))

    one = lambda r, c: pl.BlockSpec((1, r, c), lambda p: (p, 0, 0))
    outs = [((G, L * C, L * C), BF16), ((G // 2, 2 * N, 2 * L * C), BF16),
            ((G // 2, 2 * N, 2 * L * C), BF16), ((G // 2, 2 * N, 2 * L * C), BF16),
            ((G // 2, 2 * N, 2 * L * C), BF16), ((G // 2, 2, 2 * N), F32),
            ((G, C, N), F32), ((G, C, N), F32), ((G, 1, N), F32), ((G, 1, N), F32)]
    out_specs = [pl.BlockSpec((2, L * C, L * C), lambda p: (p, 0, 0))] \
        + [one(2 * N, 2 * L * C)] * 4 + [one(2, 2 * N)] \
        + [pl.BlockSpec((2,) + s[1:], lambda p: (p, 0, 0)) for s, _ in outs[6:]]
    res = pl.pallas_call(
        _s5_prep_kernel,
        grid=(G // 2,),
        in_specs=[two(a) for a in ins],
        out_specs=out_specs,
        out_shape=[jax.ShapeDtypeStruct(s, d) for s, d in outs],
        compiler_params=_params(("arbitrary",)),
        name="s5_prep",
    )(*ins)
    names = ('m', 'pre', 'pim', 'cre', 'cim', 'a16', 'brt', 'bit', 'arr', 'air')
    return dict(zip(names, res))


S5_GQ = LANES // SSM_GROUP


def _s5_prompt_kernel(u_ref, m_ref, pre_ref, pim_ref, cre_ref, cim_ref, a16_ref,
                      y_ref, xr_ref, xi_ref,
                      u2_ref, bre_ref, bim_ref, xpr_ref, xpi_ref, *, nb, nch):
    L, C, GQ = S5_CHUNK, SSM_GROUP, S5_GQ
    npair = GQ // 2

    def relayout_in(b, carry):
        uts = [u_ref[b, pl.ds(j, nch, stride=L), :].T for j in range(L)]
        for g in range(GQ):
            for hf in range(2):
                xt = jnp.concatenate([ut[g * C:(g + 1) * C, :]
                                      for ut in uts[hf * L // 2:(hf + 1) * L // 2]], axis=0)
                u2_ref[g, hf, pl.ds(b, nch, stride=nb), :] = xt.T
        return carry

    u2 = lambda g: jnp.concatenate([u2_ref[g, 0], u2_ref[g, 1]], axis=1)

    lax.fori_loop(0, nb, relayout_in, 0)

    for m in range(npair):
        ub2 = jnp.concatenate([u2(2 * m), u2(2 * m + 1)], axis=1).astype(BF16)
        bre_ref[m] = _dot_nt(ub2, pre_ref[m])
        bim_ref[m] = _dot_nt(ub2, pim_ref[m])

    ars = [a16_ref[m, 0:1, :] for m in range(npair)]
    ais = [a16_ref[m, 1:2, :] for m in range(npair)]

    def step(cidx, carry):
        o = pl.multiple_of(cidx * nb, nb)
        out = []
        for m in range(npair):
            xr, xi = carry[2 * m], carry[2 * m + 1]
            xpr_ref[m, pl.ds(o, nb), :] = xr
            xpi_ref[m, pl.ds(o, nb), :] = xi
            out.append(ars[m] * xr - ais[m] * xi + bre_ref[m, pl.ds(o, nb), :])
            out.append(ars[m] * xi + ais[m] * xr + bim_ref[m, pl.ds(o, nb), :])
        return tuple(out)

    z = jnp.zeros((nb, LANES), F32)
    fin = lax.fori_loop(0, nch, step, (z,) * GQ)
    for m in range(npair):
        xr_ref[m] = fin[2 * m]
        xi_ref[m] = fin[2 * m + 1]

    for m in range(npair):
        ys = _dot(xpr_ref[m], cre_ref[m]) + _dot(xpi_ref[m], cim_ref[m])
        for k in range(2):
            g = 2 * m + k
            yg = _dot(u2(g), m_ref[g]) + ys[:, k * L * C:(k + 1) * L * C]
            u2_ref[g, 0] = yg[:, :LANES]
            u2_ref[g, 1] = yg[:, LANES:]

    def relayout_out(b, carry):
        for hf in range(2):
            yts = [u2_ref[g, hf, pl.ds(b, nch, stride=nb), :].T for g in range(GQ)]
            for i in range(L // 2):
                yt = jnp.concatenate([t_[i * C:(i + 1) * C, :] for t_ in yts], axis=0)
                y_ref[b, pl.ds(hf * L // 2 + i, nch, stride=L), :] = yt.T
        return carry

    lax.fori_loop(0, nb, relayout_out, 0)


def _s5_prompt(u, tabs, nb, t):
    L, G, N, C = S5_CHUNK, SSM_GROUPS, SSM_STATE, SSM_GROUP
    nch = t // L
    nc = nch * nb
    nq = SSM_WIDTH // LANES
    npair = S5_GQ // 2
    spec = lambda blk: pl.BlockSpec(blk, lambda q: (q, 0, 0))
    y, xr, xi = pl.pallas_call(
        functools.partial(_s5_prompt_kernel, nb=nb, nch=nch),
        grid=(nq,),
        in_specs=[pl.BlockSpec((nb, t, LANES), lambda q: (0, 0, q), pipeline_mode=pl.Buffered(1)),
                  spec((S5_GQ, L * C, L * C)),
                  spec((npair, 2 * N, 2 * L * C)), spec((npair, 2 * N, 2 * L * C)),
                  spec((npair, 2 * N, 2 * L * C)), spec((npair, 2 * N, 2 * L * C)),
                  spec((npair, 2, 2 * N))],
        out_specs=[pl.BlockSpec((nb, t, LANES), lambda q: (0, 0, q)),
                   spec((npair, nb, 2 * N)), spec((npair, nb, 2 * N))],
        out_shape=[jax.ShapeDtypeStruct((nb, t, SSM_WIDTH), F32),
                   jax.ShapeDtypeStruct((G // 2, nb, 2 * N), F32),
                   jax.ShapeDtypeStruct((G // 2, nb, 2 * N), F32)],
        scratch_shapes=[pltpu.VMEM((S5_GQ, 2, nc, LANES), F32)]
        + [pltpu.VMEM((npair, nc, 2 * N), F32) for _ in range(4)],
        compiler_params=_params(("arbitrary",)),
        name="s5_prompt",
    )(u.reshape(nb, t, SSM_WIDTH), tabs['m'], tabs['pre'], tabs['pim'], tabs['cre'], tabs['cim'],
      tabs['a16'])
    unpair = lambda x: x.reshape(G // 2, nb, 2, N).transpose(1, 0, 2, 3).reshape(nb, G, N)
    return y.reshape(nb * t, SSM_WIDTH), unpair(xr), unpair(xi)


def _s5_step_kernel(u_ref, xr_ref, xi_ref, ar_ref, ai_ref, brt_ref, bit_ref, cr_ref, ci_ref,
                    y_ref, nr_ref, ni_ref, bd_ref):
    G, N, C = SSM_GROUPS, SSM_STATE, SSM_GROUP

    def block_diag(slot, src_ref):
        bd_ref[slot] = jnp.zeros((G * C, G * N), BF16)
        for g in range(G):
            bd_ref[slot, g * C:(g + 1) * C, g * N:(g + 1) * N] = src_ref[g].astype(BF16)
        return bd_ref[slot]

    ub = u_ref[...].astype(BF16)
    xr, xi, ar, ai = xr_ref[...], xi_ref[...], ar_ref[...], ai_ref[...]
    nr = ar * xr - ai * xi + jnp.dot(ub, block_diag(0, brt_ref), preferred_element_type=F32)
    ni = ar * xi + ai * xr + jnp.dot(ub, block_diag(1, bit_ref), preferred_element_type=F32)
    nr_ref[...] = nr
    ni_ref[...] = ni
    y_ref[...] = _dot_nt(nr, block_diag(2, cr_ref)) - _dot_nt(ni, block_diag(3, ci_ref))


def _s5_step(u, st_re, st_im, tabs, C_re, C_im):
    G, N, C = SSM_GROUPS, SSM_STATE, SSM_GROUP
    n = u.shape[0]
    ops = (u, st_re.reshape(n, G * N), st_im.reshape(n, G * N),
           tabs['arr'].reshape(1, G * N), tabs['air'].reshape(1, G * N),
           tabs['brt'], tabs['bit'], C_re, C_im)
    y, nr, ni = pl.pallas_call(
        _s5_step_kernel,
        grid=(1,),
        in_specs=[_full_spec(o.shape) for o in ops],
        out_specs=[_full_spec((n, SSM_WIDTH)), _full_spec((n, G * N)), _full_spec((n, G * N))],
        out_shape=[jax.ShapeDtypeStruct((n, SSM_WIDTH), F32),
                   jax.ShapeDtypeStruct((n, G * N), F32),
                   jax.ShapeDtypeStruct((n, G * N), F32)],
        scratch_shapes=[pltpu.VMEM((4, G * C, G * N), BF16)],
        compiler_params=_params(("arbitrary",)),
        name="s5_step",
    )(*ops)
    return y, nr.reshape(n, G, N), ni.reshape(n, G, N)


def _tail_kernel(*refs, tm, nsub, tiles_per_seq, decode, final):
    if decode:
        (x_ref, rw_ref, ys_ref, u_ref, gl_ref, pe_ref, st2_ref, st1_ref,
         dsk_ref, wglu_ref, wout_ref, ln2_ref, wfi_ref, cw_ref, cb_ref, wfo_ref,
         ln3_ref, wpg_ref, wple_ref, fin_ref, y_ref, conv_ref) = refs
    else:
        (x_ref, rw_ref, ys_ref, u_ref, gl_ref, pe_ref,
         dsk_ref, wglu_ref, wout_ref, ln2_ref, wfi_ref, cw_ref, cb_ref, wfo_ref,
         ln3_ref, wpg_ref, wple_ref, fin_ref, y_ref, conv_ref, carry_ref) = refs

    sub = tm // nsub
    cw = cw_ref[...]
    last2 = {}
    if not decode:
        @pl.when(pl.program_id(0) % tiles_per_seq == 0)
        def _():
            carry_ref[...] = jnp.zeros_like(carry_ref)

        last2[-1] = carry_ref[...]

    def row_block(s):
        rs = slice(s * sub, (s + 1) * sub)
        x = x_ref[rs, :]
        z = _gelu(ys_ref[rs, :] + dsk_ref[...] * u_ref[rs, :])
        zz = _dot(z, wglu_ref[...])
        yield
        s5 = zz[:, :D_MODEL] * _sigmoid(zz[:, D_MODEL:])
        gl = gl_ref[rs, :]
        merged = _sigmoid(gl[:, :D_MODEL]) * rw_ref[rs, :] + _sigmoid(gl[:, D_MODEL:]) * s5
        x = x + _dot(merged, wout_ref[...])
        yield
        h2 = _rmsnorm(x, ln2_ref[...]).astype(BF16)
        a_up = jnp.dot(h2, wfi_ref[:, :D_FF], preferred_element_type=F32)
        b_up = jnp.dot(h2, wfi_ref[:, D_FF:], preferred_element_type=F32)
        if decode:
            prev2 = st2_ref[rs, :]
            prev1 = st1_ref[rs, :]
            conv_ref[rs, :] = a_up
        else:
            last2[s] = a_up[sub - 2:sub, :]
        yield
        if not decode:
            rows = lax.broadcasted_iota(jnp.int32, (sub, 1), 0)
            c0 = last2[s - 1][0:1, :]
            c1 = last2[s - 1][1:2, :]
            prev1 = jnp.where(rows == 0, c1, pltpu.roll(a_up, 1, axis=0))
            prev2 = jnp.where(rows == 0, c0,
                              jnp.where(rows == 1, c1, pltpu.roll(a_up, 2, axis=0)))
        a_conv = cw[0:1] * prev2 + cw[1:2] * prev1 + cw[2:3] * a_up + cb_ref[...]
        x = x + _dot(_gelu(a_conv) * b_up, wfo_ref[...])
        yield
        pg = _sigmoid(_dot(_rmsnorm(x, ln3_ref[...]), wpg_ref[...]))
        x = x + pg * _dot(pe_ref[rs, :], wple_ref[...])
        y_ref[rs, :] = _rmsnorm(x, fin_ref[...]) if final else x
        yield

    blocks = [row_block(s) for s in range(nsub)]
    for _ in range(5):
        for blk in blocks:
            next(blk)
    if not decode:
        carry_ref[...] = last2[nsub - 1]
        conv_ref[0] = last2[nsub - 1]


def _tail(x2d, rw, ys, u, gl, pe, st_conv, D_skip, wglu, wout, ln2_g, wfi, conv_w, conv_b, wfo,
          ln3_g, wpg, wple, final_g, *, tm, tiles_per_seq, decode, final):
    n = x2d.shape[0]
    row = lambda w: pl.BlockSpec((tm, w), lambda i: (i, 0))
    consts = [D_skip.reshape(1, SSM_WIDTH), wglu, wout, ln2_g.reshape(1, D_MODEL), wfi,
              conv_w, conv_b.reshape(1, D_FF), wfo, ln3_g.reshape(1, D_MODEL), wpg, wple,
              final_g.reshape(1, D_MODEL)]
    acts = [x2d, rw, ys, u, gl, pe]
    in_specs = [row(a.shape[1]) for a in acts]
    scratch = []
    if decode:
        acts += [st_conv[:, 0], st_conv[:, 1]]
        in_specs += [row(D_FF), row(D_FF)]
        conv_spec = row(D_FF)
        conv_shape = (n, D_FF)
    else:
        nseq = n // (tm * tiles_per_seq)
        conv_spec = pl.BlockSpec((1, 2, D_FF), lambda i: (i // tiles_per_seq, 0, 0))
        conv_shape = (nseq, 2, D_FF)
        scratch = [pltpu.VMEM((2, D_FF), F32)]
    in_specs += [pl.BlockSpec(c.shape, lambda i, nd=c.ndim: (0,) * nd,
                              pipeline_mode=pl.Buffered(1)) for c in consts]
    y, conv = pl.pallas_call(
        functools.partial(_tail_kernel, tm=tm, nsub=1 if decode else TAIL_ROW_BLOCKS,
                          tiles_per_seq=tiles_per_seq, decode=decode,
                          final=final),
        grid=(n // tm,),
        in_specs=in_specs,
        out_specs=[row(D_MODEL), conv_spec],
        out_shape=[jax.ShapeDtypeStruct((n, D_MODEL), F32),
                   jax.ShapeDtypeStruct(conv_shape, F32)],
        scratch_shapes=scratch,
        compiler_params=_params(("arbitrary",)),
        name="tail_decode" if decode else "tail_prompt",
    )(*acts, *consts)
    if decode:
        conv = jnp.stack([st_conv[:, 1], conv], axis=1)
    return y, conv


def _layer(x, pe, states, W, *, decode, final, tm_proj, tt, tm_tail):
    nb, t, _ = x.shape
    n = nb * t
    x2d = x.reshape(n, D_MODEL)
    prw, u, gl = _proj(x2d, W['ln1_g'], W['w_in'], tm_proj)
    if decode:
        st_shift, st_wkv, st_re, st_im, st_conv = states
        rw, s_new = _rwkv_step(prw, st_shift, st_wkv, W['rowp'], W['wla'], W['g2'], W['eones'],
                               W['w_rw_out'])
        ys, xr, xi = _s5_step(u, st_re, st_im, W['s5'], W['C_re'], W['C_im'])
        new_shift = prw
    else:
        st_conv = None
        rw, s_new = _rwkv_prompt(prw, W['rowp'], W['wla'], W['g2'], W['eones'], W['w_rw_out'],
                                 nb, t, tt)
        ys, xr, xi = _s5_prompt(u, W['s5'], nb, t)
        new_shift = prw.reshape(nb, t, SHIFT_WIDTH)[:, -1]
    y, new_conv = _tail(x2d, rw, ys, u, gl, pe.reshape(n, PLE_DIM), st_conv, W['D_skip'],
                        W['w_glu'], W['w_out'], W['ln2_g'], W['w_ffn_in'], W['conv_w'],
                        W['conv_b'], W['w_ffn_out'], W['ln3_g'], W['w_ple_gate'], W['w_ple'],
                        W['final_g'], tm=tm_tail, tiles_per_seq=max(t // tm_tail, 1),
                        decode=decode, final=final)
    return y.reshape(nb, t, D_MODEL), (new_shift, s_new, xr, xi, new_conv)


def kernel(x_prompt, x_sample, p_prompt, p_sample, state_shift, state_wkv, state_ssm_re, state_ssm_im, state_conv, ln1_g, w_in, mu_shift, w0, w2, a0, a2, g2, k_k, k_a, r_k, lnx_g, lnx_b, w_rw_out, A_re, A_im, log_dt, B_re, B_im, C_re, C_im, D_skip, w_glu, w_out, ln2_g, w_ffn_in, conv_w, conv_b, w_ffn_out, ln3_g, w_ple_gate, w_ple, final_g):
    depth = w_in.shape[0]
    xp, xs = x_prompt, x_sample
    pst = [[] for _ in range(5)]
    sst = [[] for _ in range(5)]
    for i in range(depth):
        wla, g2b, wo = _rwkv_weights(w2[i], a2[i], g2[i], w_rw_out[i])
        W = dict(
            ln1_g=ln1_g[i], w_in=w_in[i],
            rowp=_row_params(mu_shift[i], w0[i], a0[i], k_k[i], k_a[i], r_k[i].reshape(-1),
                             lnx_g[i], lnx_b[i]),
            wla=wla, g2=g2b, eones=_rwkv_consts(), w_rw_out=wo,
            s5=_s5_prep(A_re[i], A_im[i], log_dt[i], B_re[i], B_im[i], C_re[i], C_im[i]),
            C_re=C_re[i], C_im=C_im[i], D_skip=D_skip[i],
            w_glu=w_glu[i].astype(BF16), w_out=w_out[i].astype(BF16), ln2_g=ln2_g[i],
            w_ffn_in=w_ffn_in[i].astype(BF16), conv_w=conv_w[i], conv_b=conv_b[i],
            w_ffn_out=w_ffn_out[i].astype(BF16), ln3_g=ln3_g[i],
            w_ple_gate=w_ple_gate[i].astype(BF16), w_ple=w_ple[i].astype(BF16), final_g=final_g)
        final = i == depth - 1
        tp = xp.shape[1]
        xp, sp = _layer(xp, p_prompt[i], None, W, decode=False, final=final,
                        tm_proj=min(TM_PROJ, tp), tt=min(TT_RWKV, tp), tm_tail=min(TM_TAIL, tp))
        ns = xs.shape[0]
        xs, ss = _layer(xs, p_sample[i],
                        (state_shift[i], state_wkv[i], state_ssm_re[i], state_ssm_im[i],
                         state_conv[i]),
                        W, decode=True, final=final, tm_proj=ns, tt=None, tm_tail=ns)
        for j in range(5):
            pst[j].append(sp[j])
            sst[j].append(ss[j])
    return (xp, xs,
            jnp.stack(pst[0]), jnp.stack(pst[1]), jnp.stack(pst[2]), jnp.stack(pst[3]),
            jnp.stack(pst[4]),
            jnp.stack(sst[0]), jnp.stack(sst[1]), jnp.stack(sst[2]), jnp.stack(sst[3]),
            jnp.stack(sst[4]))
```

```python
import functools
import math

import jax
import jax.numpy as jnp
import numpy as np
from jax import lax
from jax.experimental import pallas as pl
from jax.experimental.pallas import tpu as pltpu

F32 = jnp.float32
BF16 = jnp.bfloat16

D_MODEL = 1024
RW_HEADS = 8
HEAD_DIM = 64
RW_WIDTH = RW_HEADS * HEAD_DIM
LORA_W = 64
LORA_A = 64
LORA_G = 128
SHIFT_WIDTH = 3 * RW_WIDTH + LORA_W + LORA_A + LORA_G
SSM_WIDTH = 512
SSM_GROUP = 16
SSM_GROUPS = SSM_WIDTH // SSM_GROUP
SSM_STATE = 64
GATE_OFF = SHIFT_WIDTH + SSM_WIDTH
IN_WIDTH = SHIFT_WIDTH + SSM_WIDTH + 2 * D_MODEL
D_FF = 2816
PLE_DIM = 256
EPS = 1e-6
GN_EPS = 64e-5

LANES = 128
WKV_CHUNK = 64
S5_CHUNK = 16
N_PAIRS = RW_HEADS // 2
VMEM_LIMIT = 56 * 1024 * 1024
TM_PROJ = 512
TT_RWKV = 256
TM_TAIL = 256
RWKV_SEQ_GROUPS = 2
TAIL_ROW_BLOCKS = 2


def _dot(a, b):
    return jnp.dot(a.astype(BF16), b.astype(BF16), preferred_element_type=F32)


def _dot_nt(a, b):
    return lax.dot_general(a.astype(BF16), b.astype(BF16), (((1,), (1,)), ((), ())),
                           preferred_element_type=F32)


def _dot_tn(a, b):
    return lax.dot_general(a.astype(BF16), b.astype(BF16), (((0,), (0,)), ((), ())),
                           preferred_element_type=F32)


def _split3(x):
    h1 = x.astype(BF16)
    r1 = x - h1.astype(F32)
    h2 = r1.astype(BF16)
    h3 = (r1 - h2.astype(F32)).astype(BF16)
    return h1, h2, h3


def _dot_exact_lhs(a_bf16, x):
    h1, h2, h3 = _split3(x)
    d = lambda h: jnp.dot(a_bf16, h, preferred_element_type=F32)
    return d(h1) + d(h2) + d(h3)


def _sigmoid(x):
    return 1.0 / (1.0 + jnp.exp(-x))


def _gelu(x):
    c = math.sqrt(2.0 / math.pi)
    return 0.5 * x * (1.0 + jnp.tanh(c * (x + 0.044715 * (x * x * x))))


def _rmsnorm(x, g):
    return x * lax.rsqrt(jnp.mean(x * x, axis=-1, keepdims=True) + EPS) * g


def _full_spec(shape):
    nd = len(shape)
    return pl.BlockSpec(shape, lambda *_: (0,) * nd)


def _params(sem):
    return pltpu.CompilerParams(dimension_semantics=sem, vmem_limit_bytes=VMEM_LIMIT)


BF16_ROWS = 16
TAIL_WEIGHTS = ('w_glu', 'w_out', 'w_ffn_in', 'w_ffn_out', 'w_ple_gate', 'w_ple')


def _proj_kernel(x_ref, g_ref, w_ref, *refs, n_cast):
    cast_in, (prw_ref, u_ref, gl_ref), cast_out = refs[:n_cast], refs[n_cast:n_cast + 3], \
        refs[n_cast + 3:]
    h = _rmsnorm(x_ref[...], g_ref[...])
    prw_ref[...] = _dot(h, w_ref[:, :SHIFT_WIDTH])
    u_ref[...] = _dot(h, w_ref[:, SHIFT_WIDTH:GATE_OFF])
    gl_ref[...] = _dot(h, w_ref[:, GATE_OFF:])
    for src, dst in zip(cast_in, cast_out):
        dst[...] = src[...].astype(BF16)


def _proj(x2d, ln1_g, w_in, tm, cast_ws=()):
    n = x2d.shape[0]
    steps = n // tm
    row = lambda w: pl.BlockSpec((tm, w), lambda i: (i, 0))

    def slab(w):
        rows = w.shape[0]
        span = 1
        while rows % (steps // span) or (rows // (steps // span)) % BF16_ROWS:
            span *= 2
        return pl.BlockSpec((rows // (steps // span), w.shape[1]), lambda i: (i // span, 0))

    slabs = [slab(w) for w in cast_ws]
    res = pl.pallas_call(
        functools.partial(_proj_kernel, n_cast=len(cast_ws)),
        grid=(steps,),
        in_specs=[row(D_MODEL), _full_spec((1, D_MODEL)),
                  pl.BlockSpec((D_MODEL, IN_WIDTH), lambda i: (0, 0),
                               pipeline_mode=pl.Buffered(1))] + slabs,
        out_specs=[row(SHIFT_WIDTH), row(SSM_WIDTH), row(2 * D_MODEL)] + slabs,
        out_shape=[jax.ShapeDtypeStruct((n, SHIFT_WIDTH), F32),
                   jax.ShapeDtypeStruct((n, SSM_WIDTH), F32),
                   jax.ShapeDtypeStruct((n, 2 * D_MODEL), F32)]
        + [jax.ShapeDtypeStruct(w.shape, BF16) for w in cast_ws],
        compiler_params=_params(("arbitrary",)),
        name="proj",
    )(x2d, ln1_g.reshape(1, D_MODEL), w_in, *cast_ws)
    return res[:3], res[3:]


def _rwkv_token_math_stages(p, prev, mu, w0, a0, k_k, k_a, r_k, wla, g2, eones):
    xs = p + (prev - p) * mu
    r = xs[:, 0:RW_WIDTH]
    k = xs[:, RW_WIDTH:2 * RW_WIDTH]
    v = xs[:, 2 * RW_WIDTH:3 * RW_WIDTH]
    xwa = xs[:, 3 * RW_WIDTH:3 * RW_WIDTH + LORA_W + LORA_A]
    xg = xs[:, 3 * RW_WIDTH + LORA_W + LORA_A:]
    lane = lax.broadcasted_iota(jnp.int32, xwa.shape, 1)
    la = _dot(jnp.where(lane < LORA_W, jnp.tanh(xwa), xwa), wla)
    g = _dot(_sigmoid(xg), g2)
    yield None
    wpre = w0 + la[:, :RW_WIDTH]
    softplus = jnp.maximum(-wpre, 0.0) + jnp.log(1.0 + jnp.exp(-jnp.abs(wpre)))
    lw = -jnp.exp(-softplus - 0.5)
    a = _sigmoid(a0 + la[:, RW_WIDTH:])
    kkr = k * k_k
    ss = _dot(kkr * kkr, eones)
    kk = kkr / jnp.maximum(jnp.sqrt(ss), 1e-12)
    kp = k * (1.0 + (a - 1.0) * k_a)
    bonus = _dot(r * kp * r_k, eones) * v
    yield r, v, lw, a, g, kk, kp, bonus


def _rwkv_token_math(*args):
    *_, out = _rwkv_token_math_stages(*args)
    return out


def _rwkv_output(y, bonus, g, lnx_g, lnx_b, eones, w_o):
    emean = eones * (1.0 / HEAD_DIM)
    mean = _dot(y, emean)
    d = y - mean
    var = _dot(d * d, emean)
    yn = d * lax.rsqrt(var + GN_EPS) * lnx_g + lnx_b
    return _dot((yn + bonus) * g, w_o)


def _head_masks(shape):
    lane = lax.broadcasted_iota(jnp.int32, shape, 1)
    return lane < HEAD_DIM, lane >= HEAD_DIM


def _stack_heads(x, m_a, m_b):
    return jnp.concatenate([jnp.where(m_a, x, 0.0), jnp.where(m_b, x, 0.0)], axis=0)


def _rwkv_prompt_kernel(p_ref, mu_ref, w0_ref, a0_ref, kk_ref, ka_ref, rk_ref, lg_ref, lb_ref,
                        wla_ref, g2_ref, e_ref, ltri_ref, wo_ref,
                        out_ref, sout_ref,
                        carry_ref, sp_ref, y_ref, pend_ref, gend_ref, new_ref, gnew_ref,
                        *, tt, nt, nh):
    s = pl.program_id(0)
    nch = tt // WKV_CHUNK
    eones = e_ref[...]
    halves = range(nh)
    pairs = range(N_PAIRS)
    rows_of = lambda ch: slice(ch * WKV_CHUNK, (ch + 1) * WKV_CHUNK)
    lanes_of = lambda j: slice(j * LANES, (j + 1) * LANES)

    @pl.when(s == 0)
    def _():
        pend_ref[...] = jnp.zeros_like(pend_ref)
        gend_ref[...] = jnp.zeros_like(gend_ref)
        sp_ref[...] = jnp.zeros_like(sp_ref)

    @pl.when(s % nt == 0)
    def _():
        carry_ref[...] = jnp.zeros_like(carry_ref)

    @pl.when(s % nt == 1 % nt)
    def _():
        sp_ref[...] = jnp.zeros_like(sp_ref)

    rows = lax.broadcasted_iota(jnp.int32, (tt, 1), 0)
    token_math = []
    for h in halves:
        p = p_ref[h]
        prev = jnp.where(rows == 0, carry_ref[h], pltpu.roll(p, 1, axis=0))
        carry_ref[h] = p[tt - 1:tt, :]
        token_math.append(_rwkv_token_math_stages(
            p, prev, mu_ref[...], w0_ref[...], a0_ref[...], kk_ref[...], ka_ref[...],
            rk_ref[...], wla_ref[...], g2_ref[...], eones))
        next(token_math[h])

    m_a, m_b = _head_masks((WKV_CHUNK, LANES))
    ri = lax.broadcasted_iota(jnp.int32, (2 * WKV_CHUNK, 2 * WKV_CHUNK), 0)
    ci = lax.broadcasted_iota(jnp.int32, (2 * WKV_CHUNK, 2 * WKV_CHUNK), 1)
    same_head = (ri >= WKV_CHUNK) == (ci >= WKV_CHUNK)
    strict = same_head & (ri > ci)
    incl = same_head & (ri >= ci)
    blocks = [(h, ch, j) for h in halves for ch in range(nch) for j in pairs]
    blk = lambda i, h, ch, j: pend_ref[h, i, rows_of(ch), lanes_of(j)]
    stack = lambda i, h, ch, j: _stack_heads(blk(i, h, ch, j), m_a, m_b)
    vss = {b: stack(4, *b) for b in blocks}
    gms = {b: _dot_nt(jnp.concatenate([stack(0, *b), stack(1, *b)], axis=0),
                      jnp.concatenate([stack(2, *b), stack(3, *b)], axis=0)) for b in blocks}
    n_abs = {b: jnp.where(strict, gms[b][:LANES, :LANES], 0.0) for b in blocks}
    wvs = {b: _dot(jnp.where(strict, gms[b][:LANES, LANES:], 0.0), vss[b]) for b in blocks}
    a_ys = {b: jnp.concatenate([jnp.where(incl, gms[b][LANES:, :LANES], 0.0),
                                jnp.where(incl, gms[b][LANES:, LANES:], 0.0)], axis=1)
            for b in blocks}

    tm = [next(token_math[h]) for h in halves]
    cs = [_dot_exact_lhs(ltri_ref[...], tm[h][2]) for h in halves]

    eye = (ri == ci).astype(F32)
    tinvs = {b: eye + n_abs[b] for b in blocks}
    pows = n_abs
    for _ in range(5):
        pows = {b: _dot(pows[b], pows[b]).astype(BF16) for b in blocks}
        tinvs = {b: tinvs[b] + _dot(pows[b], tinvs[b]) for b in blocks}

    for h in halves:
        r, v_n, lw, a, g_n, kk, kp, bonus_n = tm[h]
        c = cs[h]
        einv = jnp.exp(-c)
        bvec = kk * a
        new_ref[h, 0] = -kk * jnp.exp(c - lw)
        new_ref[h, 1] = r * jnp.exp(c)
        new_ref[h, 2] = bvec * einv
        new_ref[h, 3] = kp * einv
        new_ref[h, 4] = v_n
        for ch in range(nch):
            c_end = c[(ch + 1) * WKV_CHUNK - 1:(ch + 1) * WKV_CHUNK, :]
            e_rest = jnp.exp(c_end - c[rows_of(ch), :])
            gnew_ref[h, ch] = jnp.exp(c_end)
            new_ref[h, 5, rows_of(ch), :] = bvec[rows_of(ch), :] * e_rest
            new_ref[h, 6, rows_of(ch), :] = kp[rows_of(ch), :] * e_rest
        new_ref[h, 7] = bonus_n
        new_ref[h, 8] = g_n

    sps = {(h, j): sp_ref[h, j] for h in halves for j in pairs}
    for ch in range(nch):
        hj = [(h, j) for h in halves for j in pairs]
        arss = {(h, j): _dot_nt(jnp.concatenate([blk(0, h, ch, j), blk(1, h, ch, j)], axis=0),
                                sps[h, j]) for h, j in hj}
        uvs = {(h, j): jnp.concatenate(
            [_dot(tinvs[h, ch, j],
                  _stack_heads(arss[h, j][:WKV_CHUNK], m_a, m_b) + wvs[h, ch, j]),
             vss[h, ch, j]], axis=0) for h, j in hj}
        yss = {(h, j): _dot(a_ys[h, ch, j], uvs[h, j]) for h, j in hj}
        incs = {(h, j): _dot_tn(uvs[h, j], jnp.concatenate(
            [stack(5, h, ch, j), stack(6, h, ch, j)], axis=0)) for h, j in hj}
        for h, j in hj:
            y_ref[h, rows_of(ch), lanes_of(j)] = (yss[h, j][:WKV_CHUNK] + yss[h, j][WKV_CHUNK:]
                                                  + arss[h, j][WKV_CHUNK:])
            sps[h, j] = sps[h, j] * gend_ref[h, ch][:, lanes_of(j)] + incs[h, j]
    for h in halves:
        for j in pairs:
            sp_ref[h, j] = sps[h, j]
        out_ref[h] = _rwkv_output(y_ref[h], pend_ref[h, 7], pend_ref[h, 8], lg_ref[...],
                                  lb_ref[...], eones, wo_ref[...])

    pend_ref[...] = new_ref[...]
    gend_ref[...] = gnew_ref[...]

    @pl.when((s % nt == 0) & (s > 0))
    def _():
        for h in halves:
            for j in pairs:
                sp = sp_ref[h, j]
                sout_ref[h, 0, 2 * j] = sp[:HEAD_DIM, :HEAD_DIM]
                sout_ref[h, 0, 2 * j + 1] = sp[HEAD_DIM:, HEAD_DIM:]


def _rwkv_consts():
    idx = np.arange(RW_WIDTH)
    return jnp.asarray(idx[:, None] // HEAD_DIM == idx[None, :] // HEAD_DIM, dtype=BF16)


def _rwkv_weights(w2, a2, g2, w_rw_out):
    wla = jnp.zeros((LORA_W + LORA_A, 2 * RW_WIDTH), F32)
    wla = wla.at[:LORA_W, :RW_WIDTH].set(w2).at[LORA_W:, RW_WIDTH:].set(a2)
    return wla.astype(BF16), g2.astype(BF16), w_rw_out.astype(BF16)


def _row_params(mu_shift, w0, a0, k_k, k_a, r_k, lnx_g, lnx_b):
    return (mu_shift.reshape(1, SHIFT_WIDTH), w0.reshape(1, RW_WIDTH), a0.reshape(1, RW_WIDTH),
            k_k.reshape(1, RW_WIDTH), k_a.reshape(1, RW_WIDTH), r_k.reshape(1, RW_WIDTH),
            lnx_g.reshape(1, RW_WIDTH), lnx_b.reshape(1, RW_WIDTH))


def _rwkv_prompt(prw, rowp, wla, g2b, eones, wo, nb, t, tt):
    nt = t // tt
    idx = np.arange(tt)
    ltri = jnp.asarray((idx[:, None] // WKV_CHUNK == idx[None, :] // WKV_CHUNK)
                       & (idx[:, None] >= idx[None, :]), dtype=BF16)
    consts = list(rowp) + [wla, g2b, eones, ltri, wo]
    nh = RWKV_SEQ_GROUPS if nb % RWKV_SEQ_GROUPS == 0 else 1
    ntiles = (nb // nh) * nt
    nch = tt // WKV_CHUNK
    done = lambda s: jnp.maximum(s - 1, 0)
    out, s_new = pl.pallas_call(
        functools.partial(_rwkv_prompt_kernel, tt=tt, nt=nt, nh=nh),
        grid=(ntiles + 1,),
        in_specs=[pl.BlockSpec((nh, tt, SHIFT_WIDTH),
                               lambda s: (0, jnp.minimum(s, ntiles - 1), 0))]
        + [_full_spec(c.shape) for c in consts],
        out_specs=[pl.BlockSpec((nh, tt, D_MODEL), lambda s: (0, done(s), 0)),
                   pl.BlockSpec((nh, 1, RW_HEADS, HEAD_DIM, HEAD_DIM),
                                lambda s: (0, done(s) // nt, 0, 0, 0))],
        out_shape=[jax.ShapeDtypeStruct((nh, nb * t // nh, D_MODEL), F32),
                   jax.ShapeDtypeStruct((nh, nb // nh, RW_HEADS, HEAD_DIM, HEAD_DIM), F32)],
        scratch_shapes=[pltpu.VMEM((nh, 1, SHIFT_WIDTH), F32),
                        pltpu.VMEM((nh, N_PAIRS, LANES, LANES), F32),
                        pltpu.VMEM((nh, tt, RW_WIDTH), F32),
                        pltpu.VMEM((nh, 9, tt, RW_WIDTH), F32),
                        pltpu.VMEM((nh, nch, 1, RW_WIDTH), F32),
                        pltpu.VMEM((nh, 9, tt, RW_WIDTH), F32),
                        pltpu.VMEM((nh, nch, 1, RW_WIDTH), F32)],
        compiler_params=_params(("arbitrary",)),
        name="rwkv_prompt",
    )(prw.reshape(nh, nb * t // nh, SHIFT_WIDTH), *consts)
    return (out.reshape(nb * t, D_MODEL),
            s_new.reshape(nb, RW_HEADS, HEAD_DIM, HEAD_DIM))


def _rwkv_step_kernel(p_ref, prev_ref, s_ref, mu_ref, w0_ref, a0_ref, kk_ref, ka_ref, rk_ref,
                      lg_ref, lb_ref, wla_ref, g2_ref, e_ref, wo_ref,
                      out_ref, sout_ref,
                      qt_ref, wt_ref, rt_ref, bt_ref, kt_ref, vt_ref, yt_ref, g_ref, bonus_ref):
    hp = pl.program_id(0)

    @pl.when(hp == 0)
    def _():
        r, v, lw, a, g, kk, kp, bonus = _rwkv_token_math(
            p_ref[...], prev_ref[...], mu_ref[...], w0_ref[...], a0_ref[...], kk_ref[...],
            ka_ref[...], rk_ref[...], wla_ref[...], g2_ref[...], e_ref[...])
        qt_ref[...] = (-kk).T
        wt_ref[...] = jnp.exp(lw).T
        rt_ref[...] = r.T
        bt_ref[...] = (kk * a).T
        kt_ref[...] = kp.T
        vt_ref[...] = v.T
        g_ref[...] = g
        bonus_ref[...] = bonus

    def per_value_row(vi, carry):
        for hl in range(2):
            row0 = pl.multiple_of((2 * hp + hl) * HEAD_DIM, HEAD_DIM)
            ks = pl.ds(row0, HEAD_DIM)
            s = s_ref[hl, vi]
            sa = jnp.sum(s * qt_ref[ks, :], axis=0, keepdims=True)
            s_new = (s * wt_ref[ks, :] + sa * bt_ref[ks, :]
                     + vt_ref[pl.ds(row0 + vi, 1), :] * kt_ref[ks, :])
            yt_ref[pl.ds(row0 + vi, 1), :] = jnp.sum(s_new * rt_ref[ks, :], axis=0, keepdims=True)
            sout_ref[hl, vi] = s_new
        return carry

    lax.fori_loop(0, HEAD_DIM, per_value_row, 0, unroll=2)

    @pl.when(hp == pl.num_programs(0) - 1)
    def _():
        out_ref[...] = _rwkv_output(yt_ref[...].T, bonus_ref[...], g_ref[...], lg_ref[...],
                                    lb_ref[...], e_ref[...], wo_ref[...])


def _rwkv_step(prw, prev, state, rowp, wla, g2b, eones, wo):
    n = prw.shape[0]
    consts = list(rowp) + [wla, g2b, eones, wo]
    state_t = jnp.transpose(state, (1, 2, 3, 0))
    sspec = pl.BlockSpec((2, HEAD_DIM, HEAD_DIM, n), lambda hp: (hp, 0, 0, 0))
    out, s_new_t = pl.pallas_call(
        _rwkv_step_kernel,
        grid=(N_PAIRS,),
        in_specs=[_full_spec((n, SHIFT_WIDTH)), _full_spec((n, SHIFT_WIDTH)), sspec]
        + [_full_spec(c.shape) for c in consts],
        out_specs=[_full_spec((n, D_MODEL)), sspec],
        out_shape=[jax.ShapeDtypeStruct((n, D_MODEL), F32),
                   jax.ShapeDtypeStruct((RW_HEADS, HEAD_DIM, HEAD_DIM, n), F32)],
        scratch_shapes=[pltpu.VMEM((RW_WIDTH, n), F32) for _ in range(7)]
        + [pltpu.VMEM((n, RW_WIDTH), F32) for _ in range(2)],
        compiler_params=_params(("arbitrary",)),
        name="rwkv_step",
    )(prw, prev, state_t, *consts)
    return out, jnp.transpose(s_new_t, (3, 0, 1, 2))


def _zoh(lr, li, dt):
    mag = jnp.exp(lr * dt)
    ar, ai = mag * jnp.cos(li * dt), mag * jnp.sin(li * dt)
    den = lr * lr + li * li
    fr = ((ar - 1.0) * lr + ai * li) / den
    fi = (ai * lr - (ar - 1.0) * li) / den
    return ar, ai, fr, fi


def _dot_split(a, b):
    a1 = a.astype(BF16)
    a2 = (a - a1.astype(F32)).astype(BF16)
    b1 = b.astype(BF16)
    b2 = (b - b1.astype(F32)).astype(BF16)
    d = lambda x, y: jnp.dot(x, y, preferred_element_type=F32)
    return d(a1, b1) + (d(a1, b2) + d(a2, b1))


def _s5_prep_kernel(lrc_ref, lic_ref, lrr_ref, lir_ref, ldt_ref, bre_ref, bim_ref, cr_ref, ci_ref,
                    m_ref, pre_ref, pim_ref, cre_ref, cim_ref, a16_ref,
                    brt_ref, bit_ref, arr_ref, air_ref):
    L, C, N = S5_CHUNK, SSM_GROUP, SSM_STATE
    lag = lax.broadcasted_iota(jnp.int32, (1, LANES), 1).astype(F32)
    sel_row = lax.broadcasted_iota(jnp.int32, (LANES, L * C), 0)
    sel_lag = lax.shift_right_logical(lax.broadcasted_iota(jnp.int32, (LANES, L * C), 1), 4)
    pick = lambda cond: jnp.where(cond, 1.0, 0.0).astype(BF16)
    r_same, r_next, r_rev = (pick(sel_row == sel_lag), pick(sel_row == sel_lag + 1),
                             pick(sel_row == (L - 1) - sel_lag))

    def select(e, r, dims=(((1,), (0,)), ((), ()))):
        d = lambda h: lax.dot_general(h, r, dims, preferred_element_type=F32)
        h1, h2, h3 = _split3(e)
        return d(h1) + d(h2) + d(h3)

    spread = select
    on_rows = (((0,), (0,)), ((), ()))
    r_tile = pick(lax.broadcasted_iota(jnp.int32, (C, L * C), 0)
                  == (lax.broadcasted_iota(jnp.int32, (C, L * C), 1) & (C - 1)))
    ident = pick(lax.broadcasted_iota(jnp.int32, (N, N), 0)
                 == lax.broadcasted_iota(jnp.int32, (N, N), 1))

    lane2 = lax.broadcasted_iota(jnp.int32, (C, L * C), 1)
    zero = jnp.zeros((N, L * C), F32)
    a16r, a16i = [], []
    for k in range(2):
        dt = jnp.exp(ldt_ref[k])
        lrc, lic = lrc_ref[k], lic_ref[k]
        mag = jnp.exp(lrc * dt * lag)
        er, ei = mag * jnp.cos(lic * dt * lag), mag * jnp.sin(lic * dt * lag)
        e0r, e0i = spread(er, r_same), spread(ei, r_same)
        e1r, e1i = spread(er, r_next), spread(ei, r_next)
        e2r, e2i = spread(er, r_rev), spread(ei, r_rev)
        _, _, frc, fic = _zoh(lrc, lic, dt)
        brl, bil = select(bre_ref[k], r_tile), select(bim_ref[k], r_tile)
        crl = select(cr_ref[k], r_tile, on_rows)
        cil = select(ci_ref[k], r_tile, on_rows)
        bbr = frc * brl - fic * bil
        bbi = frc * bil + fic * brl
        halves = lambda x: jnp.concatenate([x, zero] if k == 0 else [zero, x], axis=1)
        rows = slice(k * N, (k + 1) * N)
        pre_ref[0, rows, :] = halves(e2r * bbr - e2i * bbi).astype(BF16)
        pim_ref[0, rows, :] = halves(e2r * bbi + e2i * bbr).astype(BF16)
        cre_ref[0, rows, :] = halves(crl * e1r - cil * e1i).astype(BF16)
        cim_ref[0, rows, :] = halves(-(crl * e1i + cil * e1r)).astype(BF16)

        lrr, lir = lrr_ref[k], lir_ref[k]
        arr, air, frr, fir = _zoh(lrr, lir, dt)
        bret, bimt = select(bre_ref[k], ident, on_rows), select(bim_ref[k], ident, on_rows)
        brt = frr * bret - fir * bimt
        bit = frr * bimt + fir * bret
        brt_ref[k], bit_ref[k], arr_ref[k], air_ref[k] = brt, bit, arr, air
        m16 = jnp.exp(lrr * dt * L)
        a16r.append(m16 * jnp.cos(lir * dt * L))
        a16i.append(m16 * jnp.sin(lir * dt * L))
        klt = (_dot_split(brt, crl * e0r - cil * e0i)
               - _dot_split(bit, crl * e0i + cil * e0r))
        for j in range(L):
            blk = klt if j == 0 else jnp.where(lane2 >= j * C, pltpu.roll(klt, j * C, axis=1), 0.0)
            m_ref[k, j * C:(j + 1) * C, :] = blk.astype(BF16)
    a16_ref[0, 0:1, :] = jnp.concatenate(a16r, axis=1)
    a16_ref[0, 1:2, :] = jnp.concatenate(a16i, axis=1)


def _s5_prep(A_re, A_im, log_dt, B_re, B_im, C_re, C_im):
    L, G, N, C = S5_CHUNK, SSM_GROUPS, SSM_STATE, SSM_GROUP
    ins = [A_re[:, :, None], A_im[:, :, None], A_re[:, None, :], A_im[:, None, :],
           log_dt[:, None, None], B_re, B_im, C_re, C_im]
    two = lambda a: pl.BlockSpec((2,) + a.shape[1:], lambda p: (p, 0, 0))
    one = lambda r, c: pl.BlockSpec((1, r, c), lambda p: (p, 0, 0))
    outs = [((G, L * C, L * C), BF16), ((G // 2, 2 * N, 2 * L * C), BF16),
            ((G // 2, 2 * N, 2 * L * C), BF16), ((G // 2, 2 * N, 2 * L * C), BF16),
            ((G // 2, 2 * N, 2 * L * C), BF16), ((G // 2, 2, 2 * N), F32),
            ((G, C, N), F32), ((G, C, N), F32), ((G, 1, N), F32), ((G, 1, N), F32)]
    out_specs = [pl.BlockSpec((2, L * C, L * C), lambda p: (p, 0, 0))] \
        + [one(2 * N, 2 * L * C)] * 4 + [one(2, 2 * N)] \
        + [pl.BlockSpec((2,) + s[1:], lambda p: (p, 0, 0)) for s, _ in outs[6:]]
    res = pl.pallas_call(
        _s5_prep_kernel,
        grid=(G // 2,),
        in_specs=[two(a) for a in ins],
        out_specs=out_specs,
        out_shape=[jax.ShapeDtypeStruct(s, d) for s, d in outs],
        compiler_params=_params(("arbitrary",)),
        name="s5_prep",
    )(*ins)
    names = ('m', 'pre', 'pim', 'cre', 'cim', 'a16', 'brt', 'bit', 'arr', 'air')
    return dict(zip(names, res))


S5_GQ = LANES // SSM_GROUP


def _s5_prompt_kernel(u_ref, m_ref, pre_ref, pim_ref, cre_ref, cim_ref, a16_ref,
                      y_ref, xr_ref, xi_ref,
                      u2_ref, bre_ref, bim_ref, xpr_ref, xpi_ref, *, nb, nch):
    L, C, GQ = S5_CHUNK, SSM_GROUP, S5_GQ
    npair = GQ // 2

    def relayout_in(b, carry):
        uts = [u_ref[b, pl.ds(j, nch, stride=L), :].T for j in range(L)]
        for g in range(GQ):
            for hf in range(2):
                xt = jnp.concatenate([ut[g * C:(g + 1) * C, :]
                                      for ut in uts[hf * L // 2:(hf + 1) * L // 2]], axis=0)
                u2_ref[g, hf, pl.ds(b, nch, stride=nb), :] = xt.T
        return carry

    u2 = lambda g: jnp.concatenate([u2_ref[g, 0], u2_ref[g, 1]], axis=1)

    lax.fori_loop(0, nb, relayout_in, 0)

    for m in range(npair):
        ub2 = jnp.concatenate([u2(2 * m), u2(2 * m + 1)], axis=1).astype(BF16)
        bre_ref[m] = _dot_nt(ub2, pre_ref[m])
        bim_ref[m] = _dot_nt(ub2, pim_ref[m])

    ars = [a16_ref[m, 0:1, :] for m in range(npair)]
    ais = [a16_ref[m, 1:2, :] for m in range(npair)]

    def step(cidx, carry):
        o = pl.multiple_of(cidx * nb, nb)
        out = []
        for m in range(npair):
            xr, xi = carry[2 * m], carry[2 * m + 1]
            xpr_ref[m, pl.ds(o, nb), :] = xr
            xpi_ref[m, pl.ds(o, nb), :] = xi
            out.append(ars[m] * xr - ais[m] * xi + bre_ref[m, pl.ds(o, nb), :])
            out.append(ars[m] * xi + ais[m] * xr + bim_ref[m, pl.ds(o, nb), :])
        return tuple(out)

    z = jnp.zeros((nb, LANES), F32)
    fin = lax.fori_loop(0, nch, step, (z,) * GQ)
    for m in range(npair):
        xr_ref[m] = fin[2 * m]
        xi_ref[m] = fin[2 * m + 1]

    for m in range(npair):
        ys = _dot(xpr_ref[m], cre_ref[m]) + _dot(xpi_ref[m], cim_ref[m])
        for k in range(2):
            g = 2 * m + k
            yg = _dot(u2(g), m_ref[g]) + ys[:, k * L * C:(k + 1) * L * C]
            u2_ref[g, 0] = yg[:, :LANES]
            u2_ref[g, 1] = yg[:, LANES:]

    def relayout_out(b, carry):
        for hf in range(2):
            yts = [u2_ref[g, hf, pl.ds(b, nch, stride=nb), :].T for g in range(GQ)]
            for i in range(L // 2):
                yt = jnp.concatenate([t_[i * C:(i + 1) * C, :] for t_ in yts], axis=0)
                y_ref[b, pl.ds(hf * L // 2 + i, nch, stride=L), :] = yt.T
        return carry

    lax.fori_loop(0, nb, relayout_out, 0)


def _s5_prompt(u, tabs, nb, t):
    L, G, N, C = S5_CHUNK, SSM_GROUPS, SSM_STATE, SSM_GROUP
    nch = t // L
    nc = nch * nb
    nq = SSM_WIDTH // LANES
    npair = S5_GQ // 2
    spec = lambda blk: pl.BlockSpec(blk, lambda q: (q, 0, 0))
    y, xr, xi = pl.pallas_call(
        functools.partial(_s5_prompt_kernel, nb=nb, nch=nch),
        grid=(nq,),
        in_specs=[pl.BlockSpec((nb, t, LANES), lambda q: (0, 0, q), pipeline_mode=pl.Buffered(1)),
                  spec((S5_GQ, L * C, L * C)),
                  spec((npair, 2 * N, 2 * L * C)), spec((npair, 2 * N, 2 * L * C)),
                  spec((npair, 2 * N, 2 * L * C)), spec((npair, 2 * N, 2 * L * C)),
                  spec((npair, 2, 2 * N))],
        out_specs=[pl.BlockSpec((nb, t, LANES), lambda q: (0, 0, q)),
                   spec((npair, nb, 2 * N)), spec((npair, nb, 2 * N))],
        out_shape=[jax.ShapeDtypeStruct((nb, t, SSM_WIDTH), F32),
                   jax.ShapeDtypeStruct((G // 2, nb, 2 * N), F32),
                   jax.ShapeDtypeStruct((G // 2, nb, 2 * N), F32)],
        scratch_shapes=[pltpu.VMEM((S5_GQ, 2, nc, LANES), F32)]
        + [pltpu.VMEM((npair, nc, 2 * N), F32) for _ in range(4)],
        compiler_params=_params(("arbitrary",)),
        name="s5_prompt",
    )(u.reshape(nb, t, SSM_WIDTH), tabs['m'], tabs['pre'], tabs['pim'], tabs['cre'], tabs['cim'],
      tabs['a16'])
    unpair = lambda x: x.reshape(G // 2, nb, 2, N).transpose(1, 0, 2, 3).reshape(nb, G, N)
    return y.reshape(nb * t, SSM_WIDTH), unpair(xr), unpair(xi)


def _s5_step_kernel(u_ref, xr_ref, xi_ref, ar_ref, ai_ref, brt_ref, bit_ref, cr_ref, ci_ref,
                    y_ref, nr_ref, ni_ref, bd_ref):
    G, N, C = SSM_GROUPS, SSM_STATE, SSM_GROUP

    def block_diag(slot, src_ref):
        bd_ref[slot] = jnp.zeros((G * C, G * N), BF16)
        for g in range(G):
            bd_ref[slot, g * C:(g + 1) * C, g * N:(g + 1) * N] = src_ref[g].astype(BF16)
        return bd_ref[slot]

    ub = u_ref[...].astype(BF16)
    xr, xi, ar, ai = xr_ref[...], xi_ref[...], ar_ref[...], ai_ref[...]
    nr = ar * xr - ai * xi + jnp.dot(ub, block_diag(0, brt_ref), preferred_element_type=F32)
    ni = ar * xi + ai * xr + jnp.dot(ub, block_diag(1, bit_ref), preferred_element_type=F32)
    nr_ref[...] = nr
    ni_ref[...] = ni
    y_ref[...] = _dot_nt(nr, block_diag(2, cr_ref)) - _dot_nt(ni, block_diag(3, ci_ref))


def _s5_step(u, st_re, st_im, tabs, C_re, C_im):
    G, N, C = SSM_GROUPS, SSM_STATE, SSM_GROUP
    n = u.shape[0]
    ops = (u, st_re.reshape(n, G * N), st_im.reshape(n, G * N),
           tabs['arr'].reshape(1, G * N), tabs['air'].reshape(1, G * N),
           tabs['brt'], tabs['bit'], C_re, C_im)
    y, nr, ni = pl.pallas_call(
        _s5_step_kernel,
        grid=(1,),
        in_specs=[_full_spec(o.shape) for o in ops],
        out_specs=[_full_spec((n, SSM_WIDTH)), _full_spec((n, G * N)), _full_spec((n, G * N))],
        out_shape=[jax.ShapeDtypeStruct((n, SSM_WIDTH), F32),
                   jax.ShapeDtypeStruct((n, G * N), F32),
                   jax.ShapeDtypeStruct((n, G * N), F32)],
        scratch_shapes=[pltpu.VMEM((4, G * C, G * N), BF16)],
        compiler_params=_params(("arbitrary",)),
        name="s5_step",
    )(*ops)
    return y, nr.reshape(n, G, N), ni.reshape(n, G, N)


def _tail_kernel(*refs, tm, nsub, tiles_per_seq, decode, final):
    if decode:
        (x_ref, rw_ref, ys_ref, u_ref, gl_ref, pe_ref, st2_ref, st1_ref,
         dsk_ref, wglu_ref, wout_ref, ln2_ref, wfi_ref, cw_ref, cb_ref, wfo_ref,
         ln3_ref, wpg_ref, wple_ref, fin_ref, y_ref, conv_ref) = refs
    else:
        (x_ref, rw_ref, ys_ref, u_ref, gl_ref, pe_ref,
         dsk_ref, wglu_ref, wout_ref, ln2_ref, wfi_ref, cw_ref, cb_ref, wfo_ref,
         ln3_ref, wpg_ref, wple_ref, fin_ref, y_ref, conv_ref, carry_ref) = refs

    sub = tm // nsub
    cw = cw_ref[...]
    last2 = {}
    if not decode:
        @pl.when(pl.program_id(0) % tiles_per_seq == 0)
        def _():
            carry_ref[...] = jnp.zeros_like(carry_ref)

        last2[-1] = carry_ref[...]

    def row_block(s):
        rs = slice(s * sub, (s + 1) * sub)
        x = x_ref[rs, :]
        z = _gelu(ys_ref[rs, :] + dsk_ref[...] * u_ref[rs, :])
        zz = _dot(z, wglu_ref[...])
        yield
        s5 = zz[:, :D_MODEL] * _sigmoid(zz[:, D_MODEL:])
        gl = gl_ref[rs, :]
        merged = _sigmoid(gl[:, :D_MODEL]) * rw_ref[rs, :] + _sigmoid(gl[:, D_MODEL:]) * s5
        x = x + _dot(merged, wout_ref[...])
        yield
        h2 = _rmsnorm(x, ln2_ref[...]).astype(BF16)
        a_up = jnp.dot(h2, wfi_ref[:, :D_FF], preferred_element_type=F32)
        b_up = jnp.dot(h2, wfi_ref[:, D_FF:], preferred_element_type=F32)
        if decode:
            prev2 = st2_ref[rs, :]
            prev1 = st1_ref[rs, :]
            conv_ref[rs, :] = a_up
        else:
            last2[s] = a_up[sub - 2:sub, :]
        yield
        if not decode:
            rows = lax.broadcasted_iota(jnp.int32, (sub, 1), 0)
            c0 = last2[s - 1][0:1, :]
            c1 = last2[s - 1][1:2, :]
            prev1 = jnp.where(rows == 0, c1, pltpu.roll(a_up, 1, axis=0))
            prev2 = jnp.where(rows == 0, c0,
                              jnp.where(rows == 1, c1, pltpu.roll(a_up, 2, axis=0)))
        a_conv = cw[0:1] * prev2 + cw[1:2] * prev1 + cw[2:3] * a_up + cb_ref[...]
        x = x + _dot(_gelu(a_conv) * b_up, wfo_ref[...])
        yield
        pg = _sigmoid(_dot(_rmsnorm(x, ln3_ref[...]), wpg_ref[...]))
        x = x + pg * _dot(pe_ref[rs, :], wple_ref[...])
        y_ref[rs, :] = _rmsnorm(x, fin_ref[...]) if final else x
        yield

    blocks = [row_block(s) for s in range(nsub)]
    for _ in range(5):
        for blk in blocks:
            next(blk)
    if not decode:
        carry_ref[...] = last2[nsub - 1]
        conv_ref[0] = last2[nsub - 1]


def _tail(x2d, rw, ys, u, gl, pe, st_conv, D_skip, wglu, wout, ln2_g, wfi, conv_w, conv_b, wfo,
          ln3_g, wpg, wple, final_g, *, tm, tiles_per_seq, decode, final):
    n = x2d.shape[0]
    row = lambda w: pl.BlockSpec((tm, w), lambda i: (i, 0))
    consts = [D_skip.reshape(1, SSM_WIDTH), wglu, wout, ln2_g.reshape(1, D_MODEL), wfi,
              conv_w, conv_b.reshape(1, D_FF), wfo, ln3_g.reshape(1, D_MODEL), wpg, wple,
              final_g.reshape(1, D_MODEL)]
    acts = [x2d, rw, ys, u, gl, pe]
    in_specs = [row(a.shape[1]) for a in acts]
    scratch = []
    if decode:
        acts += [st_conv[:, 0], st_conv[:, 1]]
        in_specs += [row(D_FF), row(D_FF)]
        conv_spec = row(D_FF)
        conv_shape = (n, D_FF)
    else:
        nseq = n // (tm * tiles_per_seq)
        conv_spec = pl.BlockSpec((1, 2, D_FF), lambda i: (i // tiles_per_seq, 0, 0))
        conv_shape = (nseq, 2, D_FF)
        scratch = [pltpu.VMEM((2, D_FF), F32)]
    in_specs += [pl.BlockSpec(c.shape, lambda i, nd=c.ndim: (0,) * nd,
                              pipeline_mode=pl.Buffered(1)) for c in consts]
    y, conv = pl.pallas_call(
        functools.partial(_tail_kernel, tm=tm, nsub=1 if decode else TAIL_ROW_BLOCKS,
                          tiles_per_seq=tiles_per_seq, decode=decode,
                          final=final),
        grid=(n // tm,),
        in_specs=in_specs,
        out_specs=[row(D_MODEL), conv_spec],
        out_shape=[jax.ShapeDtypeStruct((n, D_MODEL), F32),
                   jax.ShapeDtypeStruct(conv_shape, F32)],
        scratch_shapes=scratch,
        compiler_params=_params(("arbitrary",)),
        name="tail_decode" if decode else "tail_prompt",
    )(*acts, *consts)
    if decode:
        conv = jnp.stack([st_conv[:, 1], conv], axis=1)
    return y, conv


def _layer(x, pe, states, W, *, decode, final, tm_proj, tt, tm_tail):
    nb, t, _ = x.shape
    n = nb * t
    x2d = x.reshape(n, D_MODEL)
    to_cast = [k for k in TAIL_WEIGHTS if W[k].dtype != BF16]
    (prw, u, gl), casted = _proj(x2d, W['ln1_g'], W['w_in'], tm_proj, [W[k] for k in to_cast])
    W.update(zip(to_cast, casted))
    if decode:
        st_shift, st_wkv, st_re, st_im, st_conv = states
        rw, s_new = _rwkv_step(prw, st_shift, st_wkv, W['rowp'], W['wla'], W['g2'], W['eones'],
                               W['w_rw_out'])
        ys, xr, xi = _s5_step(u, st_re, st_im, W['s5'], W['C_re'], W['C_im'])
        new_shift = prw
    else:
        st_conv = None
        rw, s_new = _rwkv_prompt(prw, W['rowp'], W['wla'], W['g2'], W['eones'], W['w_rw_out'],
                                 nb, t, tt)
        ys, xr, xi = _s5_prompt(u, W['s5'], nb, t)
        new_shift = prw.reshape(nb, t, SHIFT_WIDTH)[:, -1]
    y, new_conv = _tail(x2d, rw, ys, u, gl, pe.reshape(n, PLE_DIM), st_conv, W['D_skip'],
                        W['w_glu'], W['w_out'], W['ln2_g'], W['w_ffn_in'], W['conv_w'],
                        W['conv_b'], W['w_ffn_out'], W['ln3_g'], W['w_ple_gate'], W['w_ple'],
                        W['final_g'], tm=tm_tail, tiles_per_seq=max(t // tm_tail, 1),
                        decode=decode, final=final)
    return y.reshape(nb, t, D_MODEL), (new_shift, s_new, xr, xi, new_conv)


def kernel(x_prompt, x_sample, p_prompt, p_sample, state_shift, state_wkv, state_ssm_re, state_ssm_im, state_conv, ln1_g, w_in, mu_shift, w0, w2, a0, a2, g2, k_k, k_a, r_k, lnx_g, lnx_b, w_rw_out, A_re, A_im, log_dt, B_re, B_im, C_re, C_im, D_skip, w_glu, w_out, ln2_g, w_ffn_in, conv_w, conv_b, w_ffn_out, ln3_g, w_ple_gate, w_ple, final_g):
    depth = w_in.shape[0]
    xp, xs = x_prompt, x_sample
    pst = [[] for _ in range(5)]
    sst = [[] for _ in range(5)]
    for i in range(depth):
        wla, g2b, wo = _rwkv_weights(w2[i], a2[i], g2[i], w_rw_out[i])
        W = dict(
            ln1_g=ln1_g[i], w_in=w_in[i],
            rowp=_row_params(mu_shift[i], w0[i], a0[i], k_k[i], k_a[i], r_k[i].reshape(-1),
                             lnx_g[i], lnx_b[i]),
            wla=wla, g2=g2b, eones=_rwkv_consts(), w_rw_out=wo,
            s5=_s5_prep(A_re[i], A_im[i], log_dt[i], B_re[i], B_im[i], C_re[i], C_im[i]),
            C_re=C_re[i], C_im=C_im[i], D_skip=D_skip[i],
            w_glu=w_glu[i], w_out=w_out[i], ln2_g=ln2_g[i],
            w_ffn_in=w_ffn_in[i], conv_w=conv_w[i], conv_b=conv_b[i],
            w_ffn_out=w_ffn_out[i], ln3_g=ln3_g[i],
            w_ple_gate=w_ple_gate[i], w_ple=w_ple[i], final_g=final_g)
        final = i == depth - 1
        tp = xp.shape[1]
        xp, sp = _layer(xp, p_prompt[i], None, W, decode=False, final=final,
                        tm_proj=min(TM_PROJ, tp), tt=min(TT_RWKV, tp), tm_tail=min(TM_TAIL, tp))
        ns = xs.shape[0]
        xs, ss = _layer(xs, p_sample[i],
                        (state_shift[i], state_wkv[i], state_ssm_re[i], state_ssm_im[i],
                         state_conv[i]),
                        W, decode=True, final=final, tm_proj=ns, tt=None, tm_tail=ns)
        for j in range(5):
            pst[j].append(sp[j])
            sst[j].append(ss[j])
    return (xp, xs,
            jnp.stack(pst[0]), jnp.stack(pst[1]), jnp.stack(pst[2]), jnp.stack(pst[3]),
            jnp.stack(pst[4]),
            jnp.stack(sst[0]), jnp.stack(sst[1]), jnp.stack(sst[2]), jnp.stack(sst[3]),
            jnp.stack(sst[4]))
```

```python
import functools
import math

import jax
import jax.numpy as jnp
import numpy as np
from jax import lax
from jax.experimental import pallas as pl
from jax.experimental.pallas import tpu as pltpu

F32 = jnp.float32
BF16 = jnp.bfloat16

D_MODEL = 1024
RW_HEADS = 8
HEAD_DIM = 64
RW_WIDTH = RW_HEADS * HEAD_DIM
LORA_W = 64
LORA_A = 64
LORA_G = 128
SHIFT_WIDTH = 3 * RW_WIDTH + LORA_W + LORA_A + LORA_G
SSM_WIDTH = 512
SSM_GROUP = 16
SSM_GROUPS = SSM_WIDTH // SSM_GROUP
SSM_STATE = 64
GATE_OFF = SHIFT_WIDTH + SSM_WIDTH
IN_WIDTH = SHIFT_WIDTH + SSM_WIDTH + 2 * D_MODEL
D_FF = 2816
PLE_DIM = 256
EPS = 1e-6
GN_EPS = 64e-5

LANES = 128
WKV_CHUNK = 64
S5_CHUNK = 16
N_PAIRS = RW_HEADS // 2
VMEM_LIMIT = 60 * 1024 * 1024
TM_PROJ = 512
TT_RWKV = 256
TM_TAIL = 512
RWKV_SEQ_GROUPS = 2
TAIL_ROW_BLOCKS = 2


def _dot(a, b):
    return jnp.dot(a.astype(BF16), b.astype(BF16), preferred_element_type=F32)


def _dot_nt(a, b):
    return lax.dot_general(a.astype(BF16), b.astype(BF16), (((1,), (1,)), ((), ())),
                           preferred_element_type=F32)


def _dot_tn(a, b):
    return lax.dot_general(a.astype(BF16), b.astype(BF16), (((0,), (0,)), ((), ())),
                           preferred_element_type=F32)


def _split3(x):
    h1 = x.astype(BF16)
    r1 = x - h1.astype(F32)
    h2 = r1.astype(BF16)
    h3 = (r1 - h2.astype(F32)).astype(BF16)
    return h1, h2, h3


def _dot_exact_lhs(a_bf16, x):
    h1, h2, h3 = _split3(x)
    d = lambda h: jnp.dot(a_bf16, h, preferred_element_type=F32)
    return d(h1) + d(h2) + d(h3)


def _sigmoid(x):
    return 1.0 / (1.0 + jnp.exp(-x))


def _gelu(x):
    c = math.sqrt(2.0 / math.pi)
    return 0.5 * x * (1.0 + jnp.tanh(c * (x + 0.044715 * (x * x * x))))


def _rmsnorm(x, g):
    return x * lax.rsqrt(jnp.mean(x * x, axis=-1, keepdims=True) + EPS) * g


def _full_spec(shape):
    nd = len(shape)
    return pl.BlockSpec(shape, lambda *_: (0,) * nd)


def _params(sem):
    return pltpu.CompilerParams(dimension_semantics=sem, vmem_limit_bytes=VMEM_LIMIT)


BF16_ROWS = 16
FFN_CHUNKS = (slice(0, 768), slice(768, 1536), slice(1536, 2304), slice(2304, D_FF))
TAIL_WEIGHTS = ('w_glu', 'w_out', 'w_ffn_in', 'w_ffn_out', 'w_ple_gate', 'w_ple')


def _proj_kernel(x_ref, g_ref, w_ref, *refs, n_cast):
    cast_in, (prw_ref, u_ref, gl_ref), cast_out = refs[:n_cast], refs[n_cast:n_cast + 3], \
        refs[n_cast + 3:]
    h = _rmsnorm(x_ref[...], g_ref[...])
    prw_ref[...] = _dot(h, w_ref[:, :SHIFT_WIDTH])
    u_ref[...] = _dot(h, w_ref[:, SHIFT_WIDTH:GATE_OFF])
    gl_ref[...] = _dot(h, w_ref[:, GATE_OFF:])
    for src, dst in zip(cast_in, cast_out):
        dst[...] = src[...].astype(BF16)


def _proj(x2d, ln1_g, w_in, tm, cast_ws=()):
    n = x2d.shape[0]
    steps = n // tm
    row = lambda w: pl.BlockSpec((tm, w), lambda i: (i, 0))

    def slab(w):
        rows = w.shape[0]
        span = 1
        while rows % (steps // span) or (rows // (steps // span)) % BF16_ROWS:
            span *= 2
        return pl.BlockSpec((rows // (steps // span), w.shape[1]), lambda i: (i // span, 0))

    slabs = [slab(w) for w in cast_ws]
    res = pl.pallas_call(
        functools.partial(_proj_kernel, n_cast=len(cast_ws)),
        grid=(steps,),
        in_specs=[row(D_MODEL), _full_spec((1, D_MODEL)),
                  pl.BlockSpec((D_MODEL, IN_WIDTH), lambda i: (0, 0),
                               pipeline_mode=pl.Buffered(1))] + slabs,
        out_specs=[row(SHIFT_WIDTH), row(SSM_WIDTH), row(2 * D_MODEL)] + slabs,
        out_shape=[jax.ShapeDtypeStruct((n, SHIFT_WIDTH), F32),
                   jax.ShapeDtypeStruct((n, SSM_WIDTH), F32),
                   jax.ShapeDtypeStruct((n, 2 * D_MODEL), F32)]
        + [jax.ShapeDtypeStruct(w.shape, BF16) for w in cast_ws],
        compiler_params=_params(("arbitrary",)),
        name="proj",
    )(x2d, ln1_g.reshape(1, D_MODEL), w_in, *cast_ws)
    return res[:3], res[3:]


def _rwkv_token_math_stages(p, prev, mu, w0, a0, k_k, k_a, r_k, wla, g2, eones):
    xs = p + (prev - p) * mu
    r = xs[:, 0:RW_WIDTH]
    k = xs[:, RW_WIDTH:2 * RW_WIDTH]
    v = xs[:, 2 * RW_WIDTH:3 * RW_WIDTH]
    xwa = xs[:, 3 * RW_WIDTH:3 * RW_WIDTH + LORA_W + LORA_A]
    xg = xs[:, 3 * RW_WIDTH + LORA_W + LORA_A:]
    lane = lax.broadcasted_iota(jnp.int32, xwa.shape, 1)
    la = _dot(jnp.where(lane < LORA_W, jnp.tanh(xwa), xwa), wla)
    g = _dot(_sigmoid(xg), g2)
    yield None
    wpre = w0 + la[:, :RW_WIDTH]
    softplus = jnp.maximum(-wpre, 0.0) + jnp.log(1.0 + jnp.exp(-jnp.abs(wpre)))
    lw = -jnp.exp(-softplus - 0.5)
    a = _sigmoid(a0 + la[:, RW_WIDTH:])
    kkr = k * k_k
    ss = _dot(kkr * kkr, eones)
    kk = kkr / jnp.maximum(jnp.sqrt(ss), 1e-12)
    kp = k * (1.0 + (a - 1.0) * k_a)
    bonus = _dot(r * kp * r_k, eones) * v
    yield r, v, lw, a, g, kk, kp, bonus


def _rwkv_token_math(*args):
    *_, out = _rwkv_token_math_stages(*args)
    return out


def _rwkv_output(y, bonus, g, lnx_g, lnx_b, eones, w_o):
    emean = eones * (1.0 / HEAD_DIM)
    mean = _dot(y, emean)
    d = y - mean
    var = _dot(d * d, emean)
    yn = d * lax.rsqrt(var + GN_EPS) * lnx_g + lnx_b
    return _dot((yn + bonus) * g, w_o)


def _head_masks(shape):
    lane = lax.broadcasted_iota(jnp.int32, shape, 1)
    return lane < HEAD_DIM, lane >= HEAD_DIM


def _stack_heads(x, m_a, m_b):
    return jnp.concatenate([jnp.where(m_a, x, 0.0), jnp.where(m_b, x, 0.0)], axis=0)


def _rwkv_prompt_kernel(p_ref, mu_ref, w0_ref, a0_ref, kk_ref, ka_ref, rk_ref, lg_ref, lb_ref,
                        wla_ref, g2_ref, e_ref, ltri_ref, wo_ref,
                        out_ref, sout_ref,
                        carry_ref, sp_ref, y_ref, pend_ref, gend_ref, new_ref, gnew_ref,
                        *, tt, nt, nh):
    s = pl.program_id(0)
    nch = tt // WKV_CHUNK
    eones = e_ref[...]
    halves = range(nh)
    pairs = range(N_PAIRS)
    rows_of = lambda ch: slice(ch * WKV_CHUNK, (ch + 1) * WKV_CHUNK)
    lanes_of = lambda j: slice(j * LANES, (j + 1) * LANES)

    @pl.when(s == 0)
    def _():
        pend_ref[...] = jnp.zeros_like(pend_ref)
        gend_ref[...] = jnp.zeros_like(gend_ref)
        sp_ref[...] = jnp.zeros_like(sp_ref)

    @pl.when(s % nt == 0)
    def _():
        carry_ref[...] = jnp.zeros_like(carry_ref)

    @pl.when(s % nt == 1 % nt)
    def _():
        sp_ref[...] = jnp.zeros_like(sp_ref)

    rows = lax.broadcasted_iota(jnp.int32, (tt, 1), 0)
    token_math = []
    for h in halves:
        p = p_ref[h]
        prev = jnp.where(rows == 0, carry_ref[h], pltpu.roll(p, 1, axis=0))
        carry_ref[h] = p[tt - 1:tt, :]
        token_math.append(_rwkv_token_math_stages(
            p, prev, mu_ref[...], w0_ref[...], a0_ref[...], kk_ref[...], ka_ref[...],
            rk_ref[...], wla_ref[...], g2_ref[...], eones))
        next(token_math[h])

    m_a, m_b = _head_masks((WKV_CHUNK, LANES))
    ri = lax.broadcasted_iota(jnp.int32, (2 * WKV_CHUNK, 2 * WKV_CHUNK), 0)
    ci = lax.broadcasted_iota(jnp.int32, (2 * WKV_CHUNK, 2 * WKV_CHUNK), 1)
    same_head = (ri >= WKV_CHUNK) == (ci >= WKV_CHUNK)
    strict = same_head & (ri > ci)
    incl = same_head & (ri >= ci)
    blocks = [(h, ch, j) for h in halves for ch in range(nch) for j in pairs]
    blk = lambda i, h, ch, j: pend_ref[h, i, rows_of(ch), lanes_of(j)]
    stack = lambda i, h, ch, j: _stack_heads(blk(i, h, ch, j), m_a, m_b)
    vss = {b: stack(4, *b) for b in blocks}
    gms = {b: _dot_nt(jnp.concatenate([stack(0, *b), stack(1, *b)], axis=0),
                      jnp.concatenate([stack(2, *b), stack(3, *b)], axis=0)) for b in blocks}
    n_abs = {b: jnp.where(strict, gms[b][:LANES, :LANES], 0.0) for b in blocks}
    wvs = {b: _dot(jnp.where(strict, gms[b][:LANES, LANES:], 0.0), vss[b]) for b in blocks}
    a_ys = {b: jnp.concatenate([jnp.where(incl, gms[b][LANES:, :LANES], 0.0),
                                jnp.where(incl, gms[b][LANES:, LANES:], 0.0)], axis=1)
            for b in blocks}

    tm = [next(token_math[h]) for h in halves]
    cs = [_dot_exact_lhs(ltri_ref[...], tm[h][2]) for h in halves]

    eye = (ri == ci).astype(F32)
    tinvs = {b: eye + n_abs[b] for b in blocks}
    pows = n_abs
    for _ in range(5):
        pows = {b: _dot(pows[b], pows[b]).astype(BF16) for b in blocks}
        tinvs = {b: tinvs[b] + _dot(pows[b], tinvs[b]) for b in blocks}

    for h in halves:
        r, v_n, lw, a, g_n, kk, kp, bonus_n = tm[h]
        c = cs[h]
        einv = jnp.exp(-c)
        bvec = kk * a
        new_ref[h, 0] = -kk * jnp.exp(c - lw)
        new_ref[h, 1] = r * jnp.exp(c)
        new_ref[h, 2] = bvec * einv
        new_ref[h, 3] = kp * einv
        new_ref[h, 4] = v_n
        for ch in range(nch):
            c_end = c[(ch + 1) * WKV_CHUNK - 1:(ch + 1) * WKV_CHUNK, :]
            e_rest = jnp.exp(c_end - c[rows_of(ch), :])
            gnew_ref[h, ch] = jnp.exp(c_end)
            new_ref[h, 5, rows_of(ch), :] = bvec[rows_of(ch), :] * e_rest
            new_ref[h, 6, rows_of(ch), :] = kp[rows_of(ch), :] * e_rest
        new_ref[h, 7] = bonus_n
        new_ref[h, 8] = g_n

    sps = {(h, j): sp_ref[h, j] for h in halves for j in pairs}
    for ch in range(nch):
        hj = [(h, j) for h in halves for j in pairs]
        arss = {(h, j): _dot_nt(jnp.concatenate([blk(0, h, ch, j), blk(1, h, ch, j)], axis=0),
                                sps[h, j]) for h, j in hj}
        uvs = {(h, j): jnp.concatenate(
            [_dot(tinvs[h, ch, j],
                  _stack_heads(arss[h, j][:WKV_CHUNK], m_a, m_b) + wvs[h, ch, j]),
             vss[h, ch, j]], axis=0) for h, j in hj}
        yss = {(h, j): _dot(a_ys[h, ch, j], uvs[h, j]) for h, j in hj}
        incs = {(h, j): _dot_tn(uvs[h, j], jnp.concatenate(
            [stack(5, h, ch, j), stack(6, h, ch, j)], axis=0)) for h, j in hj}
        for h, j in hj:
            y_ref[h, rows_of(ch), lanes_of(j)] = (yss[h, j][:WKV_CHUNK] + yss[h, j][WKV_CHUNK:]
                                                  + arss[h, j][WKV_CHUNK:])
            sps[h, j] = sps[h, j] * gend_ref[h, ch][:, lanes_of(j)] + incs[h, j]
    for h in halves:
        for j in pairs:
            sp_ref[h, j] = sps[h, j]
        out_ref[h] = _rwkv_output(y_ref[h], pend_ref[h, 7], pend_ref[h, 8], lg_ref[...],
                                  lb_ref[...], eones, wo_ref[...])

    pend_ref[...] = new_ref[...]
    gend_ref[...] = gnew_ref[...]

    @pl.when((s % nt == 0) & (s > 0))
    def _():
        for h in halves:
            for j in pairs:
                sp = sp_ref[h, j]
                sout_ref[h, 0, 2 * j] = sp[:HEAD_DIM, :HEAD_DIM]
                sout_ref[h, 0, 2 * j + 1] = sp[HEAD_DIM:, HEAD_DIM:]


def _rwkv_consts():
    idx = np.arange(RW_WIDTH)
    return jnp.asarray(idx[:, None] // HEAD_DIM == idx[None, :] // HEAD_DIM, dtype=BF16)


def _rwkv_weights(w2, a2, g2, w_rw_out):
    wla = jnp.zeros((LORA_W + LORA_A, 2 * RW_WIDTH), F32)
    wla = wla.at[:LORA_W, :RW_WIDTH].set(w2).at[LORA_W:, RW_WIDTH:].set(a2)
    return wla.astype(BF16), g2.astype(BF16), w_rw_out.astype(BF16)


def _row_params(mu_shift, w0, a0, k_k, k_a, r_k, lnx_g, lnx_b):
    return (mu_shift.reshape(1, SHIFT_WIDTH), w0.reshape(1, RW_WIDTH), a0.reshape(1, RW_WIDTH),
            k_k.reshape(1, RW_WIDTH), k_a.reshape(1, RW_WIDTH), r_k.reshape(1, RW_WIDTH),
            lnx_g.reshape(1, RW_WIDTH), lnx_b.reshape(1, RW_WIDTH))


def _rwkv_prompt(prw, rowp, wla, g2b, eones, wo, nb, t, tt):
    nt = t // tt
    idx = np.arange(tt)
    ltri = jnp.asarray((idx[:, None] // WKV_CHUNK == idx[None, :] // WKV_CHUNK)
                       & (idx[:, None] >= idx[None, :]), dtype=BF16)
    consts = list(rowp) + [wla, g2b, eones, ltri, wo]
    nh = RWKV_SEQ_GROUPS if nb % RWKV_SEQ_GROUPS == 0 else 1
    ntiles = (nb // nh) * nt
    nch = tt // WKV_CHUNK
    done = lambda s: jnp.maximum(s - 1, 0)
    out, s_new = pl.pallas_call(
        functools.partial(_rwkv_prompt_kernel, tt=tt, nt=nt, nh=nh),
        grid=(ntiles + 1,),
        in_specs=[pl.BlockSpec((nh, tt, SHIFT_WIDTH),
                               lambda s: (0, jnp.minimum(s, ntiles - 1), 0))]
        + [_full_spec(c.shape) for c in consts],
        out_specs=[pl.BlockSpec((nh, tt, D_MODEL), lambda s: (0, done(s), 0)),
                   pl.BlockSpec((nh, 1, RW_HEADS, HEAD_DIM, HEAD_DIM),
                                lambda s: (0, done(s) // nt, 0, 0, 0))],
        out_shape=[jax.ShapeDtypeStruct((nh, nb * t // nh, D_MODEL), F32),
                   jax.ShapeDtypeStruct((nh, nb // nh, RW_HEADS, HEAD_DIM, HEAD_DIM), F32)],
        scratch_shapes=[pltpu.VMEM((nh, 1, SHIFT_WIDTH), F32),
                        pltpu.VMEM((nh, N_PAIRS, LANES, LANES), F32),
                        pltpu.VMEM((nh, tt, RW_WIDTH), F32),
                        pltpu.VMEM((nh, 9, tt, RW_WIDTH), F32),
                        pltpu.VMEM((nh, nch, 1, RW_WIDTH), F32),
                        pltpu.VMEM((nh, 9, tt, RW_WIDTH), F32),
                        pltpu.VMEM((nh, nch, 1, RW_WIDTH), F32)],
        compiler_params=_params(("arbitrary",)),
        name="rwkv_prompt",
    )(prw.reshape(nh, nb * t // nh, SHIFT_WIDTH), *consts)
    return (out.reshape(nb * t, D_MODEL),
            s_new.reshape(nb, RW_HEADS, HEAD_DIM, HEAD_DIM))


def _rwkv_step_kernel(p_ref, prev_ref, s_ref, mu_ref, w0_ref, a0_ref, kk_ref, ka_ref, rk_ref,
                      lg_ref, lb_ref, wla_ref, g2_ref, e_ref, wo_ref,
                      out_ref, sout_ref,
                      qt_ref, wt_ref, rt_ref, bt_ref, kt_ref, vt_ref, yt_ref, g_ref, bonus_ref):
    hp = pl.program_id(0)

    @pl.when(hp == 0)
    def _():
        r, v, lw, a, g, kk, kp, bonus = _rwkv_token_math(
            p_ref[...], prev_ref[...], mu_ref[...], w0_ref[...], a0_ref[...], kk_ref[...],
            ka_ref[...], rk_ref[...], wla_ref[...], g2_ref[...], e_ref[...])
        qt_ref[...] = (-kk).T
        wt_ref[...] = jnp.exp(lw).T
        rt_ref[...] = r.T
        bt_ref[...] = (kk * a).T
        kt_ref[...] = kp.T
        vt_ref[...] = v.T
        g_ref[...] = g
        bonus_ref[...] = bonus

    def per_value_row(vi, carry):
        for hl in range(2):
            row0 = pl.multiple_of((2 * hp + hl) * HEAD_DIM, HEAD_DIM)
            ks = pl.ds(row0, HEAD_DIM)
            s = s_ref[hl, vi]
            sa = jnp.sum(s * qt_ref[ks, :], axis=0, keepdims=True)
            s_new = (s * wt_ref[ks, :] + sa * bt_ref[ks, :]
                     + vt_ref[pl.ds(row0 + vi, 1), :] * kt_ref[ks, :])
            yt_ref[pl.ds(row0 + vi, 1), :] = jnp.sum(s_new * rt_ref[ks, :], axis=0, keepdims=True)
            sout_ref[hl, vi] = s_new
        return carry

    lax.fori_loop(0, HEAD_DIM, per_value_row, 0, unroll=2)

    @pl.when(hp == pl.num_programs(0) - 1)
    def _():
        out_ref[...] = _rwkv_output(yt_ref[...].T, bonus_ref[...], g_ref[...], lg_ref[...],
                                    lb_ref[...], e_ref[...], wo_ref[...])


def _rwkv_step(prw, prev, state, rowp, wla, g2b, eones, wo):
    n = prw.shape[0]
    consts = list(rowp) + [wla, g2b, eones, wo]
    state_t = jnp.transpose(state, (1, 2, 3, 0))
    sspec = pl.BlockSpec((2, HEAD_DIM, HEAD_DIM, n), lambda hp: (hp, 0, 0, 0))
    out, s_new_t = pl.pallas_call(
        _rwkv_step_kernel,
        grid=(N_PAIRS,),
        in_specs=[_full_spec((n, SHIFT_WIDTH)), _full_spec((n, SHIFT_WIDTH)), sspec]
        + [_full_spec(c.shape) for c in consts],
        out_specs=[_full_spec((n, D_MODEL)), sspec],
        out_shape=[jax.ShapeDtypeStruct((n, D_MODEL), F32),
                   jax.ShapeDtypeStruct((RW_HEADS, HEAD_DIM, HEAD_DIM, n), F32)],
        scratch_shapes=[pltpu.VMEM((RW_WIDTH, n), F32) for _ in range(7)]
        + [pltpu.VMEM((n, RW_WIDTH), F32) for _ in range(2)],
        compiler_params=_params(("arbitrary",)),
        name="rwkv_step",
    )(prw, prev, state_t, *consts)
    return out, jnp.transpose(s_new_t, (3, 0, 1, 2))


def _zoh(lr, li, dt):
    mag = jnp.exp(lr * dt)
    ar, ai = mag * jnp.cos(li * dt), mag * jnp.sin(li * dt)
    den = lr * lr + li * li
    fr = ((ar - 1.0) * lr + ai * li) / den
    fi = (ai * lr - (ar - 1.0) * li) / den
    return ar, ai, fr, fi


def _dot_split(a, b):
    a1 = a.astype(BF16)
    a2 = (a - a1.astype(F32)).astype(BF16)
    b1 = b.astype(BF16)
    b2 = (b - b1.astype(F32)).astype(BF16)
    d = lambda x, y: jnp.dot(x, y, preferred_element_type=F32)
    return d(a1, b1) + (d(a1, b2) + d(a2, b1))


def _s5_prep_kernel(lrc_ref, lic_ref, lrr_ref, lir_ref, ldt_ref, bre_ref, bim_ref, cr_ref, ci_ref,
                    m_ref, pre_ref, pim_ref, cre_ref, cim_ref, a16_ref,
                    brt_ref, bit_ref, arr_ref, air_ref):
    L, C, N = S5_CHUNK, SSM_GROUP, SSM_STATE
    lag = lax.broadcasted_iota(jnp.int32, (1, LANES), 1).astype(F32)
    sel_row = lax.broadcasted_iota(jnp.int32, (LANES, L * C), 0)
    sel_lag = lax.shift_right_logical(lax.broadcasted_iota(jnp.int32, (LANES, L * C), 1), 4)
    pick = lambda cond: jnp.where(cond, 1.0, 0.0).astype(BF16)
    r_same, r_next, r_rev = (pick(sel_row == sel_lag), pick(sel_row == sel_lag + 1),
                             pick(sel_row == (L - 1) - sel_lag))

    def select(e, r, dims=(((1,), (0,)), ((), ()))):
        d = lambda h: lax.dot_general(h, r, dims, preferred_element_type=F32)
        h1, h2, h3 = _split3(e)
        return d(h1) + d(h2) + d(h3)

    spread = select
    on_rows = (((0,), (0,)), ((), ()))
    r_tile = pick(lax.broadcasted_iota(jnp.int32, (C, L * C), 0)
                  == (lax.broadcasted_iota(jnp.int32, (C, L * C), 1) & (C - 1)))
    ident = pick(lax.broadcasted_iota(jnp.int32, (N, N), 0)
                 == lax.broadcasted_iota(jnp.int32, (N, N), 1))

    lane2 = lax.broadcasted_iota(jnp.int32, (C, L * C), 1)
    zero = jnp.zeros((N, L * C), F32)
    a16r, a16i = [], []
    for k in range(2):
        dt = jnp.exp(ldt_ref[k])
        lrc, lic = lrc_ref[k], lic_ref[k]
        mag = jnp.exp(lrc * dt * lag)
        er, ei = mag * jnp.cos(lic * dt * lag), mag * jnp.sin(lic * dt * lag)
        e0r, e0i = spread(er, r_same), spread(ei, r_same)
        e1r, e1i = spread(er, r_next), spread(ei, r_next)
        e2r, e2i = spread(er, r_rev), spread(ei, r_rev)
        _, _, frc, fic = _zoh(lrc, lic, dt)
        brl, bil = select(bre_ref[k], r_tile), select(bim_ref[k], r_tile)
        crl = select(cr_ref[k], r_tile, on_rows)
        cil = select(ci_ref[k], r_tile, on_rows)
        bbr = frc * brl - fic * bil
        bbi = frc * bil + fic * brl
        halves = lambda x: jnp.concatenate([x, zero] if k == 0 else [zero, x], axis=1)
        rows = slice(k * N, (k + 1) * N)
        pre_ref[0, rows, :] = halves(e2r * bbr - e2i * bbi).astype(BF16)
        pim_ref[0, rows, :] = halves(e2r * bbi + e2i * bbr).astype(BF16)
        cre_ref[0, rows, :] = halves(crl * e1r - cil * e1i).astype(BF16)
        cim_ref[0, rows, :] = halves(-(crl * e1i + cil * e1r)).astype(BF16)

        lrr, lir = lrr_ref[k], lir_ref[k]
        arr, air, frr, fir = _zoh(lrr, lir, dt)
        bret, bimt = select(bre_ref[k], ident, on_rows), select(bim_ref[k], ident, on_rows)
        brt = frr * bret - fir * bimt
        bit = frr * bimt + fir * bret
        brt_ref[k], bit_ref[k], arr_ref[k], air_ref[k] = brt, bit, arr, air
        m16 = jnp.exp(lrr * dt * L)
        a16r.append(m16 * jnp.cos(lir * dt * L))
        a16i.append(m16 * jnp.sin(lir * dt * L))
        klt = (_dot_split(brt, crl * e0r - cil * e0i)
               - _dot_split(bit, crl * e0i + cil * e0r))
        for j in range(L):
            blk = klt if j == 0 else jnp.where(lane2 >= j * C, pltpu.roll(klt, j * C, axis=1), 0.0)
            m_ref[k, j * C:(j + 1) * C, :] = blk.astype(BF16)
    a16_ref[0, 0:1, :] = jnp.concatenate(a16r, axis=1)
    a16_ref[0, 1:2, :] = jnp.concatenate(a16i, axis=1)


def _s5_prep(A_re, A_im, log_dt, B_re, B_im, C_re, C_im):
    L, G, N, C = S5_CHUNK, SSM_GROUPS, SSM_STATE, SSM_GROUP
    ins = [A_re[:, :, None], A_im[:, :, None], A_re[:, None, :], A_im[:, None, :],
           log_dt[:, None, None], B_re, B_im, C_re, C_im]
    two = lambda a: pl.BlockSpec((2,) + a.shape[1:], lambda p: (p, 0, 0))
    one = lambda r, c: pl.BlockSpec((1, r, c), lambda p: (p, 0, 0))
    outs = [((G, L * C, L * C), BF16), ((G // 2, 2 * N, 2 * L * C), BF16),
            ((G // 2, 2 * N, 2 * L * C), BF16), ((G // 2, 2 * N, 2 * L * C), BF16),
            ((G // 2, 2 * N, 2 * L * C), BF16), ((G // 2, 2, 2 * N), F32),
            ((G, C, N), F32), ((G, C, N), F32), ((G, 1, N), F32), ((G, 1, N), F32)]
    out_specs = [pl.BlockSpec((2, L * C, L * C), lambda p: (p, 0, 0))] \
        + [one(2 * N, 2 * L * C)] * 4 + [one(2, 2 * N)] \
        + [pl.BlockSpec((2,) + s[1:], lambda p: (p, 0, 0)) for s, _ in outs[6:]]
    res = pl.pallas_call(
        _s5_prep_kernel,
        grid=(G // 2,),
        in_specs=[two(a) for a in ins],
        out_specs=out_specs,
        out_shape=[jax.ShapeDtypeStruct(s, d) for s, d in outs],
        compiler_params=_params(("arbitrary",)),
        name="s5_prep",
    )(*ins)
    names = ('m', 'pre', 'pim', 'cre', 'cim', 'a16', 'brt', 'bit', 'arr', 'air')
    return dict(zip(names, res))


S5_GQ = LANES // SSM_GROUP


def _s5_prompt_kernel(u_ref, m_ref, pre_ref, pim_ref, cre_ref, cim_ref, a16_ref,
                      y_ref, xr_ref, xi_ref,
                      u2_ref, bre_ref, bim_ref, xpr_ref, xpi_ref, *, nb, nch):
    L, C, GQ = S5_CHUNK, SSM_GROUP, S5_GQ
    npair = GQ // 2

    def relayout_in(b, carry):
        uts = [u_ref[b, pl.ds(j, nch, stride=L), :].T for j in range(L)]
        for g in range(GQ):
            for hf in range(2):
                xt = jnp.concatenate([ut[g * C:(g + 1) * C, :]
                                      for ut in uts[hf * L // 2:(hf + 1) * L // 2]], axis=0)
                u2_ref[g, hf, pl.ds(b, nch, stride=nb), :] = xt.T
        return carry

    u2 = lambda g: jnp.concatenate([u2_ref[g, 0], u2_ref[g, 1]], axis=1)

    lax.fori_loop(0, nb, relayout_in, 0)

    for m in range(npair):
        ub2 = jnp.concatenate([u2(2 * m), u2(2 * m + 1)], axis=1).astype(BF16)
        bre_ref[m] = _dot_nt(ub2, pre_ref[m])
        bim_ref[m] = _dot_nt(ub2, pim_ref[m])

    ars = [a16_ref[m, 0:1, :] for m in range(npair)]
    ais = [a16_ref[m, 1:2, :] for m in range(npair)]

    def step(cidx, carry):
        o = pl.multiple_of(cidx * nb, nb)
        out = []
        for m in range(npair):
            xr, xi = carry[2 * m], carry[2 * m + 1]
            xpr_ref[m, pl.ds(o, nb), :] = xr
            xpi_ref[m, pl.ds(o, nb), :] = xi
            out.append(ars[m] * xr - ais[m] * xi + bre_ref[m, pl.ds(o, nb), :])
            out.append(ars[m] * xi + ais[m] * xr + bim_ref[m, pl.ds(o, nb), :])
        return tuple(out)

    z = jnp.zeros((nb, LANES), F32)
    fin = lax.fori_loop(0, nch, step, (z,) * GQ)
    for m in range(npair):
        xr_ref[m] = fin[2 * m]
        xi_ref[m] = fin[2 * m + 1]

    for m in range(npair):
        ys = _dot(xpr_ref[m], cre_ref[m]) + _dot(xpi_ref[m], cim_ref[m])
        for k in range(2):
            g = 2 * m + k
            yg = _dot(u2(g), m_ref[g]) + ys[:, k * L * C:(k + 1) * L * C]
            u2_ref[g, 0] = yg[:, :LANES]
            u2_ref[g, 1] = yg[:, LANES:]

    def relayout_out(b, carry):
        for hf in range(2):
            yts = [u2_ref[g, hf, pl.ds(b, nch, stride=nb), :].T for g in range(GQ)]
            for i in range(L // 2):
                yt = jnp.concatenate([t_[i * C:(i + 1) * C, :] for t_ in yts], axis=0)
                y_ref[b, pl.ds(hf * L // 2 + i, nch, stride=L), :] = yt.T
        return carry

    lax.fori_loop(0, nb, relayout_out, 0)


def _s5_prompt(u, tabs, nb, t):
    L, G, N, C = S5_CHUNK, SSM_GROUPS, SSM_STATE, SSM_GROUP
    nch = t // L
    nc = nch * nb
    nq = SSM_WIDTH // LANES
    npair = S5_GQ // 2
    spec = lambda blk: pl.BlockSpec(blk, lambda q: (q, 0, 0))
    y, xr, xi = pl.pallas_call(
        functools.partial(_s5_prompt_kernel, nb=nb, nch=nch),
        grid=(nq,),
        in_specs=[pl.BlockSpec((nb, t, LANES), lambda q: (0, 0, q), pipeline_mode=pl.Buffered(1)),
                  spec((S5_GQ, L * C, L * C)),
                  spec((npair, 2 * N, 2 * L * C)), spec((npair, 2 * N, 2 * L * C)),
                  spec((npair, 2 * N, 2 * L * C)), spec((npair, 2 * N, 2 * L * C)),
                  spec((npair, 2, 2 * N))],
        out_specs=[pl.BlockSpec((nb, t, LANES), lambda q: (0, 0, q)),
                   spec((npair, nb, 2 * N)), spec((npair, nb, 2 * N))],
        out_shape=[jax.ShapeDtypeStruct((nb, t, SSM_WIDTH), F32),
                   jax.ShapeDtypeStruct((G // 2, nb, 2 * N), F32),
                   jax.ShapeDtypeStruct((G // 2, nb, 2 * N), F32)],
        scratch_shapes=[pltpu.VMEM((S5_GQ, 2, nc, LANES), F32)]
        + [pltpu.VMEM((npair, nc, 2 * N), F32) for _ in range(4)],
        compiler_params=_params(("arbitrary",)),
        name="s5_prompt",
    )(u.reshape(nb, t, SSM_WIDTH), tabs['m'], tabs['pre'], tabs['pim'], tabs['cre'], tabs['cim'],
      tabs['a16'])
    unpair = lambda x: x.reshape(G // 2, nb, 2, N).transpose(1, 0, 2, 3).reshape(nb, G, N)
    return y.reshape(nb * t, SSM_WIDTH), unpair(xr), unpair(xi)


def _s5_step_kernel(u_ref, xr_ref, xi_ref, ar_ref, ai_ref, brt_ref, bit_ref, cr_ref, ci_ref,
                    y_ref, nr_ref, ni_ref, bd_ref):
    G, N, C = SSM_GROUPS, SSM_STATE, SSM_GROUP

    def block_diag(slot, src_ref):
        bd_ref[slot] = jnp.zeros((G * C, G * N), BF16)
        for g in range(G):
            bd_ref[slot, g * C:(g + 1) * C, g * N:(g + 1) * N] = src_ref[g].astype(BF16)
        return bd_ref[slot]

    ub = u_ref[...].astype(BF16)
    xr, xi, ar, ai = xr_ref[...], xi_ref[...], ar_ref[...], ai_ref[...]
    nr = ar * xr - ai * xi + jnp.dot(ub, block_diag(0, brt_ref), preferred_element_type=F32)
    ni = ar * xi + ai * xr + jnp.dot(ub, block_diag(1, bit_ref), preferred_element_type=F32)
    nr_ref[...] = nr
    ni_ref[...] = ni
    y_ref[...] = _dot_nt(nr, block_diag(2, cr_ref)) - _dot_nt(ni, block_diag(3, ci_ref))


def _s5_step(u, st_re, st_im, tabs, C_re, C_im):
    G, N, C = SSM_GROUPS, SSM_STATE, SSM_GROUP
    n = u.shape[0]
    ops = (u, st_re.reshape(n, G * N), st_im.reshape(n, G * N),
           tabs['arr'].reshape(1, G * N), tabs['air'].reshape(1, G * N),
           tabs['brt'], tabs['bit'], C_re, C_im)
    y, nr, ni = pl.pallas_call(
        _s5_step_kernel,
        grid=(1,),
        in_specs=[_full_spec(o.shape) for o in ops],
        out_specs=[_full_spec((n, SSM_WIDTH)), _full_spec((n, G * N)), _full_spec((n, G * N))],
        out_shape=[jax.ShapeDtypeStruct((n, SSM_WIDTH), F32),
                   jax.ShapeDtypeStruct((n, G * N), F32),
                   jax.ShapeDtypeStruct((n, G * N), F32)],
        scratch_shapes=[pltpu.VMEM((4, G * C, G * N), BF16)],
        compiler_params=_params(("arbitrary",)),
        name="s5_step",
    )(*ops)
    return y, nr.reshape(n, G, N), ni.reshape(n, G, N)


def _tail_kernel(*refs, tm, nsub, tiles_per_seq, decode, final):
    if decode:
        (x_ref, rw_ref, ys_ref, u_ref, gl_ref, pe_ref, st2_ref, st1_ref,
         dsk_ref, wglu_ref, wout_ref, ln2_ref, wfi_ref, cw_ref, cb_ref, wfo_ref,
         ln3_ref, wpg_ref, wple_ref, fin_ref, y_ref, conv_ref) = refs
    else:
        (x_ref, rw_ref, ys_ref, u_ref, gl_ref, pe_ref,
         dsk_ref, wglu_ref, wout_ref, ln2_ref, wfi_ref, cw_ref, cb_ref, wfo_ref,
         ln3_ref, wpg_ref, wple_ref, fin_ref, y_ref, conv_ref, carry_ref) = refs

    sub = tm // nsub
    cw = cw_ref[...]
    last2 = {}
    if not decode:
        @pl.when(pl.program_id(0) % tiles_per_seq == 0)
        def _():
            carry_ref[...] = jnp.zeros_like(carry_ref)

        for cols in FFN_CHUNKS:
            last2[-1, cols.start] = carry_ref[:, cols]

    def row_block(s):
        rs = slice(s * sub, (s + 1) * sub)
        x = x_ref[rs, :]
        z = _gelu(ys_ref[rs, :] + dsk_ref[...] * u_ref[rs, :])
        zz = _dot(z, wglu_ref[...])
        yield
        s5 = zz[:, :D_MODEL] * _sigmoid(zz[:, D_MODEL:])
        gl = gl_ref[rs, :]
        merged = _sigmoid(gl[:, :D_MODEL]) * rw_ref[rs, :] + _sigmoid(gl[:, D_MODEL:]) * s5
        x = x + _dot(merged, wout_ref[...])
        yield
        h2 = _rmsnorm(x, ln2_ref[...]).astype(BF16)
        rows = lax.broadcasted_iota(jnp.int32, (sub, 1), 0)

        def up(cols):
            a_c = jnp.dot(h2, wfi_ref[:, cols], preferred_element_type=F32)
            b_c = jnp.dot(h2, wfi_ref[:, slice(D_FF + cols.start, D_FF + cols.stop)],
                          preferred_element_type=F32)
            if decode:
                conv_ref[rs, cols] = a_c
            else:
                last2[s, cols.start] = a_c[sub - 2:sub, :]
            return a_c, b_c

        def down(a_c, b_c, cols):
            if decode:
                prev2, prev1 = st2_ref[rs, cols], st1_ref[rs, cols]
            else:
                before = last2[s - 1, cols.start]
                c0, c1 = before[0:1, :], before[1:2, :]
                prev1 = jnp.where(rows == 0, c1, pltpu.roll(a_c, 1, axis=0))
                prev2 = jnp.where(rows == 0, c0,
                                  jnp.where(rows == 1, c1, pltpu.roll(a_c, 2, axis=0)))
            a_conv = (cw[0:1, cols] * prev2 + cw[1:2, cols] * prev1 + cw[2:3, cols] * a_c
                      + cb_ref[:, cols])
            return _dot(_gelu(a_conv) * b_c, wfo_ref[cols, :])

        ups = {0: up(FFN_CHUNKS[0])}
        for c, cols in enumerate(FFN_CHUNKS):
            if c + 1 < len(FFN_CHUNKS):
                ups[c + 1] = up(FFN_CHUNKS[c + 1])
            yield
            x = x + down(*ups.pop(c), cols)
        yield
        pg = _sigmoid(_dot(_rmsnorm(x, ln3_ref[...]), wpg_ref[...]))
        x = x + pg * _dot(pe_ref[rs, :], wple_ref[...])
        y_ref[rs, :] = _rmsnorm(x, fin_ref[...]) if final else x
        yield

    blocks = [row_block(s) for s in range(nsub)]
    for _ in range(4 + len(FFN_CHUNKS)):
        for blk in blocks:
            next(blk)
    if not decode:
        for cols in FFN_CHUNKS:
            carry_ref[:, cols] = last2[nsub - 1, cols.start]
            conv_ref[0, :, cols] = last2[nsub - 1, cols.start]


def _tail(x2d, rw, ys, u, gl, pe, st_conv, D_skip, wglu, wout, ln2_g, wfi, conv_w, conv_b, wfo,
          ln3_g, wpg, wple, final_g, *, tm, tiles_per_seq, decode, final):
    n = x2d.shape[0]
    row = lambda w: pl.BlockSpec((tm, w), lambda i: (i, 0))
    consts = [D_skip.reshape(1, SSM_WIDTH), wglu, wout, ln2_g.reshape(1, D_MODEL), wfi,
              conv_w, conv_b.reshape(1, D_FF), wfo, ln3_g.reshape(1, D_MODEL), wpg, wple,
              final_g.reshape(1, D_MODEL)]
    acts = [x2d, rw, ys, u, gl, pe]
    in_specs = [row(a.shape[1]) for a in acts]
    scratch = []
    if decode:
        acts += [st_conv[:, 0], st_conv[:, 1]]
        in_specs += [row(D_FF), row(D_FF)]
        conv_spec = row(D_FF)
        conv_shape = (n, D_FF)
    else:
        nseq = n // (tm * tiles_per_seq)
        conv_spec = pl.BlockSpec((1, 2, D_FF), lambda i: (i // tiles_per_seq, 0, 0))
        conv_shape = (nseq, 2, D_FF)
        scratch = [pltpu.VMEM((2, D_FF), F32)]
    in_specs += [pl.BlockSpec(c.shape, lambda i, nd=c.ndim: (0,) * nd,
                              pipeline_mode=pl.Buffered(1)) for c in consts]
    y, conv = pl.pallas_call(
        functools.partial(_tail_kernel, tm=tm, nsub=1 if decode else TAIL_ROW_BLOCKS,
                          tiles_per_seq=tiles_per_seq, decode=decode,
                          final=final),
        grid=(n // tm,),
        in_specs=in_specs,
        out_specs=[row(D_MODEL), conv_spec],
        out_shape=[jax.ShapeDtypeStruct((n, D_MODEL), F32),
                   jax.ShapeDtypeStruct(conv_shape, F32)],
        scratch_shapes=scratch,
        compiler_params=_params(("arbitrary",)),
        name="tail_decode" if decode else "tail_prompt",
    )(*acts, *consts)
    if decode:
        conv = jnp.stack([st_conv[:, 1], conv], axis=1)
    return y, conv


def _layer(x, pe, states, W, *, decode, final, tm_proj, tt, tm_tail):
    nb, t, _ = x.shape
    n = nb * t
    x2d = x.reshape(n, D_MODEL)
    to_cast = [k for k in TAIL_WEIGHTS if W[k].dtype != BF16]
    (prw, u, gl), casted = _proj(x2d, W['ln1_g'], W['w_in'], tm_proj, [W[k] for k in to_cast])
    W.update(zip(to_cast, casted))
    if decode:
        st_shift, st_wkv, st_re, st_im, st_conv = states
        rw, s_new = _rwkv_step(prw, st_shift, st_wkv, W['rowp'], W['wla'], W['g2'], W['eones'],
                               W['w_rw_out'])
        ys, xr, xi = _s5_step(u, st_re, st_im, W['s5'], W['C_re'], W['C_im'])
        new_shift = prw
    else:
        st_conv = None
        rw, s_new = _rwkv_prompt(prw, W['rowp'], W['wla'], W['g2'], W['eones'], W['w_rw_out'],
                                 nb, t, tt)
        ys, xr, xi = _s5_prompt(u, W['s5'], nb, t)
        new_shift = prw.reshape(nb, t, SHIFT_WIDTH)[:, -1]
    y, new_conv = _tail(x2d, rw, ys, u, gl, pe.reshape(n, PLE_DIM), st_conv, W['D_skip'],
                        W['w_glu'], W['w_out'], W['ln2_g'], W['w_ffn_in'], W['conv_w'],
                        W['conv_b'], W['w_ffn_out'], W['ln3_g'], W['w_ple_gate'], W['w_ple'],
                        W['final_g'], tm=tm_tail, tiles_per_seq=max(t // tm_tail, 1),
                        decode=decode, final=final)
    return y.reshape(nb, t, D_MODEL), (new_shift, s_new, xr, xi, new_conv)


def kernel(x_prompt, x_sample, p_prompt, p_sample, state_shift, state_wkv, state_ssm_re, state_ssm_im, state_conv, ln1_g, w_in, mu_shift, w0, w2, a0, a2, g2, k_k, k_a, r_k, lnx_g, lnx_b, w_rw_out, A_re, A_im, log_dt, B_re, B_im, C_re, C_im, D_skip, w_glu, w_out, ln2_g, w_ffn_in, conv_w, conv_b, w_ffn_out, ln3_g, w_ple_gate, w_ple, final_g):
    depth = w_in.shape[0]
    xp, xs = x_prompt, x_sample
    pst = [[] for _ in range(5)]
    sst = [[] for _ in range(5)]
    for i in range(depth):
        wla, g2b, wo = _rwkv_weights(w2[i], a2[i], g2[i], w_rw_out[i])
        W = dict(
            ln1_g=ln1_g[i], w_in=w_in[i],
            rowp=_row_params(mu_shift[i], w0[i], a0[i], k_k[i], k_a[i], r_k[i].reshape(-1),
                             lnx_g[i], lnx_b[i]),
            wla=wla, g2=g2b, eones=_rwkv_consts(), w_rw_out=wo,
            s5=_s5_prep(A_re[i], A_im[i], log_dt[i], B_re[i], B_im[i], C_re[i], C_im[i]),
            C_re=C_re[i], C_im=C_im[i], D_skip=D_skip[i],
            w_glu=w_glu[i], w_out=w_out[i], ln2_g=ln2_g[i],
            w_ffn_in=w_ffn_in[i], conv_w=conv_w[i], conv_b=conv_b[i],
            w_ffn_out=w_ffn_out[i], ln3_g=ln3_g[i],
            w_ple_gate=w_ple_gate[i], w_ple=w_ple[i], final_g=final_g)
        final = i == depth - 1
        tp = xp.shape[1]
        xp, sp = _layer(xp, p_prompt[i], None, W, decode=False, final=final,
                        tm_proj=min(TM_PROJ, tp), tt=min(TT_RWKV, tp), tm_tail=min(TM_TAIL, tp))
        ns = xs.shape[0]
        xs, ss = _layer(xs, p_sample[i],
                        (state_shift[i], state_wkv[i], state_ssm_re[i], state_ssm_im[i],
                         state_conv[i]),
                        W, decode=True, final=final, tm_proj=ns, tt=None, tm_tail=ns)
        for j in range(5):
            pst[j].append(sp[j])
            sst[j].append(ss[j])
    return (xp, xs,
            jnp.stack(pst[0]), jnp.stack(pst[1]), jnp.stack(pst[2]), jnp.stack(pst[3]),
            jnp.stack(pst[4]),
            jnp.stack(sst[0]), jnp.stack(sst[1]), jnp.stack(sst[2]), jnp.stack(sst[3]),
            jnp.stack(sst[4]))
```

```python
import functools
import math

import jax
import jax.numpy as jnp
import numpy as np
from jax import lax
from jax.experimental import pallas as pl
from jax.experimental.pallas import tpu as pltpu

F32 = jnp.float32
BF16 = jnp.bfloat16

D_MODEL = 1024
RW_HEADS = 8
HEAD_DIM = 64
RW_WIDTH = RW_HEADS * HEAD_DIM
LORA_W = 64
LORA_A = 64
LORA_G = 128
SHIFT_WIDTH = 3 * RW_WIDTH + LORA_W + LORA_A + LORA_G
SSM_WIDTH = 512
SSM_GROUP = 16
SSM_GROUPS = SSM_WIDTH // SSM_GROUP
SSM_STATE = 64
GATE_OFF = SHIFT_WIDTH + SSM_WIDTH
IN_WIDTH = SHIFT_WIDTH + SSM_WIDTH + 2 * D_MODEL
D_FF = 2816
PLE_DIM = 256
EPS = 1e-6
GN_EPS = 64e-5

LANES = 128
WKV_CHUNK = 64
S5_CHUNK = 16
N_PAIRS = RW_HEADS // 2
VMEM_LIMIT = 60 * 1024 * 1024
TM_PROJ = 512
TT_RWKV = 256
TM_TAIL = 512
RWKV_SEQ_GROUPS = 2
TAIL_ROW_BLOCKS = 2


def _dot(a, b):
    return jnp.dot(a.astype(BF16), b.astype(BF16), preferred_element_type=F32)


def _dot_nt(a, b):
    return lax.dot_general(a.astype(BF16), b.astype(BF16), (((1,), (1,)), ((), ())),
                           preferred_element_type=F32)


def _dot_tn(a, b):
    return lax.dot_general(a.astype(BF16), b.astype(BF16), (((0,), (0,)), ((), ())),
                           preferred_element_type=F32)


def _split3(x):
    h1 = x.astype(BF16)
    r1 = x - h1.astype(F32)
    h2 = r1.astype(BF16)
    h3 = (r1 - h2.astype(F32)).astype(BF16)
    return h1, h2, h3


def _dot_exact_lhs(a_bf16, x):
    h1, h2, h3 = _split3(x)
    d = lambda h: jnp.dot(a_bf16, h, preferred_element_type=F32)
    return d(h1) + d(h2) + d(h3)


def _sigmoid(x):
    return 1.0 / (1.0 + jnp.exp(-x))


def _gelu(x):
    c = math.sqrt(2.0 / math.pi)
    return 0.5 * x * (1.0 + jnp.tanh(c * (x + 0.044715 * (x * x * x))))


def _rmsnorm(x, g):
    return x * lax.rsqrt(jnp.mean(x * x, axis=-1, keepdims=True) + EPS) * g


def _full_spec(shape):
    nd = len(shape)
    return pl.BlockSpec(shape, lambda *_: (0,) * nd)


def _params(sem):
    return pltpu.CompilerParams(dimension_semantics=sem, vmem_limit_bytes=VMEM_LIMIT)


BF16_ROWS = 16
FFN_CHUNKS = (slice(0, 768), slice(768, 1536), slice(1536, 2304), slice(2304, D_FF))
TAIL_WEIGHTS = ('w_glu', 'w_out', 'w_ffn_in', 'w_ffn_out', 'w_ple_gate', 'w_ple')


def _proj_kernel(x_ref, g_ref, w_ref, *refs, n_cast):
    cast_in, (prw_ref, u_ref, gl_ref), cast_out = refs[:n_cast], refs[n_cast:n_cast + 3], \
        refs[n_cast + 3:]
    h = _rmsnorm(x_ref[...], g_ref[...])
    prw_ref[...] = _dot(h, w_ref[:, :SHIFT_WIDTH])
    u_ref[...] = _dot(h, w_ref[:, SHIFT_WIDTH:GATE_OFF])
    gl_ref[...] = _dot(h, w_ref[:, GATE_OFF:])
    for src, dst in zip(cast_in, cast_out):
        dst[...] = src[...].astype(BF16)


def _proj(x2d, ln1_g, w_in, tm, cast_ws=()):
    n = x2d.shape[0]
    steps = n // tm
    row = lambda w: pl.BlockSpec((tm, w), lambda i: (i, 0))

    def slab(w):
        rows = w.shape[0]
        span = 1
        while rows % (steps // span) or (rows // (steps // span)) % BF16_ROWS:
            span *= 2
        return pl.BlockSpec((rows // (steps // span), w.shape[1]), lambda i: (i // span, 0))

    slabs = [slab(w) for w in cast_ws]
    res = pl.pallas_call(
        functools.partial(_proj_kernel, n_cast=len(cast_ws)),
        grid=(steps,),
        in_specs=[row(D_MODEL), _full_spec((1, D_MODEL)),
                  pl.BlockSpec((D_MODEL, IN_WIDTH), lambda i: (0, 0),
                               pipeline_mode=pl.Buffered(1))] + slabs,
        out_specs=[row(SHIFT_WIDTH), row(SSM_WIDTH), row(2 * D_MODEL)] + slabs,
        out_shape=[jax.ShapeDtypeStruct((n, SHIFT_WIDTH), F32),
                   jax.ShapeDtypeStruct((n, SSM_WIDTH), F32),
                   jax.ShapeDtypeStruct((n, 2 * D_MODEL), F32)]
        + [jax.ShapeDtypeStruct(w.shape, BF16) for w in cast_ws],
        compiler_params=_params(("arbitrary",)),
        name="proj",
    )(x2d, ln1_g.reshape(1, D_MODEL), w_in, *cast_ws)
    return res[:3], res[3:]


def _rwkv_token_math_stages(p, prev, mu, w0, a0, k_k, k_a, r_k, wla, g2, eones):
    xs = p + (prev - p) * mu
    r = xs[:, 0:RW_WIDTH]
    k = xs[:, RW_WIDTH:2 * RW_WIDTH]
    v = xs[:, 2 * RW_WIDTH:3 * RW_WIDTH]
    xwa = xs[:, 3 * RW_WIDTH:3 * RW_WIDTH + LORA_W + LORA_A]
    xg = xs[:, 3 * RW_WIDTH + LORA_W + LORA_A:]
    lane = lax.broadcasted_iota(jnp.int32, xwa.shape, 1)
    la = _dot(jnp.where(lane < LORA_W, jnp.tanh(xwa), xwa), wla)
    g = _dot(_sigmoid(xg), g2)
    yield None
    wpre = w0 + la[:, :RW_WIDTH]
    softplus = jnp.maximum(-wpre, 0.0) + jnp.log(1.0 + jnp.exp(-jnp.abs(wpre)))
    lw = -jnp.exp(-softplus - 0.5)
    a = _sigmoid(a0 + la[:, RW_WIDTH:])
    kkr = k * k_k
    ss = _dot(kkr * kkr, eones)
    kk = kkr / jnp.maximum(jnp.sqrt(ss), 1e-12)
    kp = k * (1.0 + (a - 1.0) * k_a)
    bonus = _dot(r * kp * r_k, eones) * v
    yield r, v, lw, a, g, kk, kp, bonus


def _rwkv_token_math(*args):
    *_, out = _rwkv_token_math_stages(*args)
    return out


def _rwkv_output(y, bonus, g, lnx_g, lnx_b, eones, w_o):
    emean = eones * (1.0 / HEAD_DIM)
    mean = _dot(y, emean)
    d = y - mean
    var = _dot(d * d, emean)
    yn = d * lax.rsqrt(var + GN_EPS) * lnx_g + lnx_b
    return _dot((yn + bonus) * g, w_o)


def _head_masks(shape):
    lane = lax.broadcasted_iota(jnp.int32, shape, 1)
    return lane < HEAD_DIM, lane >= HEAD_DIM


def _stack_heads(x, m_a, m_b):
    return jnp.concatenate([jnp.where(m_a, x, 0.0), jnp.where(m_b, x, 0.0)], axis=0)


def _rwkv_prompt_kernel(p_ref, mu_ref, w0_ref, a0_ref, kk_ref, ka_ref, rk_ref, lg_ref, lb_ref,
                        wla_ref, g2_ref, e_ref, ltri_ref, wo_ref,
                        out_ref, sout_ref,
                        carry_ref, sp_ref, y_ref, pend_ref, pendf_ref, gend_ref,
                        new_ref, newf_ref, gnew_ref, *, tt, nt, nh):
    s = pl.program_id(0)
    nch = tt // WKV_CHUNK
    eones = e_ref[...]
    halves = range(nh)
    pairs = range(N_PAIRS)
    rows_of = lambda ch: slice(ch * WKV_CHUNK, (ch + 1) * WKV_CHUNK)
    lanes_of = lambda j: slice(j * LANES, (j + 1) * LANES)

    @pl.when(s == 0)
    def _():
        pend_ref[...] = jnp.zeros_like(pend_ref)
        pendf_ref[...] = jnp.zeros_like(pendf_ref)
        gend_ref[...] = jnp.zeros_like(gend_ref)
        sp_ref[...] = jnp.zeros_like(sp_ref)

    @pl.when(s % nt == 0)
    def _():
        carry_ref[...] = jnp.zeros_like(carry_ref)

    @pl.when(s % nt == 1 % nt)
    def _():
        sp_ref[...] = jnp.zeros_like(sp_ref)

    rows = lax.broadcasted_iota(jnp.int32, (tt, 1), 0)
    token_math = []
    for h in halves:
        p = p_ref[h]
        prev = jnp.where(rows == 0, carry_ref[h], pltpu.roll(p, 1, axis=0))
        carry_ref[h] = p[tt - 1:tt, :]
        token_math.append(_rwkv_token_math_stages(
            p, prev, mu_ref[...], w0_ref[...], a0_ref[...], kk_ref[...], ka_ref[...],
            rk_ref[...], wla_ref[...], g2_ref[...], eones))
        next(token_math[h])

    m_a, m_b = _head_masks((WKV_CHUNK, LANES))
    ri = lax.broadcasted_iota(jnp.int32, (2 * WKV_CHUNK, 2 * WKV_CHUNK), 0)
    ci = lax.broadcasted_iota(jnp.int32, (2 * WKV_CHUNK, 2 * WKV_CHUNK), 1)
    same_head = (ri >= WKV_CHUNK) == (ci >= WKV_CHUNK)
    strict = same_head & (ri > ci)
    incl = same_head & (ri >= ci)
    blocks = [(h, ch, j) for h in halves for ch in range(nch) for j in pairs]
    blk = lambda i, h, ch, j: pend_ref[h, i, rows_of(ch), lanes_of(j)]
    stack = lambda i, h, ch, j: _stack_heads(blk(i, h, ch, j), m_a, m_b)
    vss = {b: stack(4, *b) for b in blocks}
    gms = {b: _dot_nt(jnp.concatenate([stack(0, *b), stack(1, *b)], axis=0),
                      jnp.concatenate([stack(2, *b), stack(3, *b)], axis=0)) for b in blocks}
    n_abs = {b: jnp.where(strict, gms[b][:LANES, :LANES], 0.0) for b in blocks}
    wvs = {b: _dot(jnp.where(strict, gms[b][:LANES, LANES:], 0.0), vss[b]) for b in blocks}
    a_ys = {b: jnp.concatenate([jnp.where(incl, gms[b][LANES:, :LANES], 0.0),
                                jnp.where(incl, gms[b][LANES:, LANES:], 0.0)],
                               axis=1).astype(BF16) for b in blocks}

    tm = [next(token_math[h]) for h in halves]
    cs = [_dot_exact_lhs(ltri_ref[...], tm[h][2]) for h in halves]

    eye = (ri == ci).astype(F32)
    tinvs = {b: eye + n_abs[b] for b in blocks}
    pows = n_abs
    for _ in range(5):
        pows = {b: _dot(pows[b], pows[b]).astype(BF16) for b in blocks}
        tinvs = {b: tinvs[b] + _dot(pows[b], tinvs[b]) for b in blocks}

    for h in halves:
        r, v_n, lw, a, g_n, kk, kp, bonus_n = tm[h]
        c = cs[h]
        einv = jnp.exp(-c)
        bvec = kk * a
        new_ref[h, 0] = (-kk * jnp.exp(c - lw)).astype(BF16)
        new_ref[h, 1] = (r * jnp.exp(c)).astype(BF16)
        new_ref[h, 2] = (bvec * einv).astype(BF16)
        new_ref[h, 3] = (kp * einv).astype(BF16)
        new_ref[h, 4] = v_n.astype(BF16)
        for ch in range(nch):
            c_end = c[(ch + 1) * WKV_CHUNK - 1:(ch + 1) * WKV_CHUNK, :]
            e_rest = jnp.exp(c_end - c[rows_of(ch), :])
            gnew_ref[h, ch] = jnp.exp(c_end)
            new_ref[h, 5, rows_of(ch), :] = (bvec[rows_of(ch), :] * e_rest).astype(BF16)
            new_ref[h, 6, rows_of(ch), :] = (kp[rows_of(ch), :] * e_rest).astype(BF16)
        newf_ref[h, 0] = bonus_n
        newf_ref[h, 1] = g_n

    sps = {(h, j): sp_ref[h, j] for h in halves for j in pairs}
    for ch in range(nch):
        hj = [(h, j) for h in halves for j in pairs]
        arss = {(h, j): _dot_nt(jnp.concatenate([blk(0, h, ch, j), blk(1, h, ch, j)], axis=0),
                                sps[h, j]) for h, j in hj}
        uvs = {(h, j): jnp.concatenate(
            [_dot(tinvs[h, ch, j],
                  _stack_heads(arss[h, j][:WKV_CHUNK], m_a, m_b) + wvs[h, ch, j]).astype(BF16),
             vss[h, ch, j]], axis=0) for h, j in hj}
        yss = {(h, j): _dot(a_ys[h, ch, j], uvs[h, j]) for h, j in hj}
        incs = {(h, j): _dot_tn(uvs[h, j], jnp.concatenate(
            [stack(5, h, ch, j), stack(6, h, ch, j)], axis=0)) for h, j in hj}
        for h, j in hj:
            y_ref[h, rows_of(ch), lanes_of(j)] = (yss[h, j][:WKV_CHUNK] + yss[h, j][WKV_CHUNK:]
                                                  + arss[h, j][WKV_CHUNK:])
            sps[h, j] = sps[h, j] * gend_ref[h, ch][:, lanes_of(j)] + incs[h, j]
    for h in halves:
        for j in pairs:
            sp_ref[h, j] = sps[h, j]
        out_ref[h] = _rwkv_output(y_ref[h], pendf_ref[h, 0], pendf_ref[h, 1], lg_ref[...],
                                  lb_ref[...], eones, wo_ref[...])

    pend_ref[...] = new_ref[...]
    pendf_ref[...] = newf_ref[...]
    gend_ref[...] = gnew_ref[...]

    @pl.when((s % nt == 0) & (s > 0))
    def _():
        for h in halves:
            for j in pairs:
                sp = sp_ref[h, j]
                sout_ref[h, 0, 2 * j] = sp[:HEAD_DIM, :HEAD_DIM]
                sout_ref[h, 0, 2 * j + 1] = sp[HEAD_DIM:, HEAD_DIM:]


def _rwkv_consts():
    idx = np.arange(RW_WIDTH)
    return jnp.asarray(idx[:, None] // HEAD_DIM == idx[None, :] // HEAD_DIM, dtype=BF16)


def _rwkv_weights(w2, a2, g2, w_rw_out):
    wla = jnp.zeros((LORA_W + LORA_A, 2 * RW_WIDTH), F32)
    wla = wla.at[:LORA_W, :RW_WIDTH].set(w2).at[LORA_W:, RW_WIDTH:].set(a2)
    return wla.astype(BF16), g2.astype(BF16), w_rw_out.astype(BF16)


def _row_params(mu_shift, w0, a0, k_k, k_a, r_k, lnx_g, lnx_b):
    return (mu_shift.reshape(1, SHIFT_WIDTH), w0.reshape(1, RW_WIDTH), a0.reshape(1, RW_WIDTH),
            k_k.reshape(1, RW_WIDTH), k_a.reshape(1, RW_WIDTH), r_k.reshape(1, RW_WIDTH),
            lnx_g.reshape(1, RW_WIDTH), lnx_b.reshape(1, RW_WIDTH))


def _rwkv_prompt(prw, rowp, wla, g2b, eones, wo, nb, t, tt):
    nt = t // tt
    idx = np.arange(tt)
    ltri = jnp.asarray((idx[:, None] // WKV_CHUNK == idx[None, :] // WKV_CHUNK)
                       & (idx[:, None] >= idx[None, :]), dtype=BF16)
    consts = list(rowp) + [wla, g2b, eones, ltri, wo]
    nh = RWKV_SEQ_GROUPS if nb % RWKV_SEQ_GROUPS == 0 else 1
    ntiles = (nb // nh) * nt
    nch = tt // WKV_CHUNK
    done = lambda s: jnp.maximum(s - 1, 0)
    out, s_new = pl.pallas_call(
        functools.partial(_rwkv_prompt_kernel, tt=tt, nt=nt, nh=nh),
        grid=(ntiles + 1,),
        in_specs=[pl.BlockSpec((nh, tt, SHIFT_WIDTH),
                               lambda s: (0, jnp.minimum(s, ntiles - 1), 0))]
        + [_full_spec(c.shape) for c in consts],
        out_specs=[pl.BlockSpec((nh, tt, D_MODEL), lambda s: (0, done(s), 0)),
                   pl.BlockSpec((nh, 1, RW_HEADS, HEAD_DIM, HEAD_DIM),
                                lambda s: (0, done(s) // nt, 0, 0, 0))],
        out_shape=[jax.ShapeDtypeStruct((nh, nb * t // nh, D_MODEL), F32),
                   jax.ShapeDtypeStruct((nh, nb // nh, RW_HEADS, HEAD_DIM, HEAD_DIM), F32)],
        scratch_shapes=[pltpu.VMEM((nh, 1, SHIFT_WIDTH), F32),
                        pltpu.VMEM((nh, N_PAIRS, LANES, LANES), F32),
                        pltpu.VMEM((nh, tt, RW_WIDTH), F32),
                        pltpu.VMEM((nh, 7, tt, RW_WIDTH), BF16),
                        pltpu.VMEM((nh, 2, tt, RW_WIDTH), F32),
                        pltpu.VMEM((nh, nch, 1, RW_WIDTH), F32),
                        pltpu.VMEM((nh, 7, tt, RW_WIDTH), BF16),
                        pltpu.VMEM((nh, 2, tt, RW_WIDTH), F32),
                        pltpu.VMEM((nh, nch, 1, RW_WIDTH), F32)],
        compiler_params=_params(("arbitrary",)),
        name="rwkv_prompt",
    )(prw.reshape(nh, nb * t // nh, SHIFT_WIDTH), *consts)
    return (out.reshape(nb * t, D_MODEL),
            s_new.reshape(nb, RW_HEADS, HEAD_DIM, HEAD_DIM))


def _rwkv_step_kernel(p_ref, prev_ref, s_ref, mu_ref, w0_ref, a0_ref, kk_ref, ka_ref, rk_ref,
                      lg_ref, lb_ref, wla_ref, g2_ref, e_ref, wo_ref,
                      out_ref, sout_ref,
                      qt_ref, wt_ref, rt_ref, bt_ref, kt_ref, vt_ref, yt_ref, g_ref, bonus_ref):
    hp = pl.program_id(0)

    @pl.when(hp == 0)
    def _():
        r, v, lw, a, g, kk, kp, bonus = _rwkv_token_math(
            p_ref[...], prev_ref[...], mu_ref[...], w0_ref[...], a0_ref[...], kk_ref[...],
            ka_ref[...], rk_ref[...], wla_ref[...], g2_ref[...], e_ref[...])
        qt_ref[...] = (-kk).T
        wt_ref[...] = jnp.exp(lw).T
        rt_ref[...] = r.T
        bt_ref[...] = (kk * a).T
        kt_ref[...] = kp.T
        vt_ref[...] = v.T
        g_ref[...] = g
        bonus_ref[...] = bonus

    def per_value_row(vi, carry):
        for hl in range(2):
            row0 = pl.multiple_of((2 * hp + hl) * HEAD_DIM, HEAD_DIM)
            ks = pl.ds(row0, HEAD_DIM)
            s = s_ref[hl, vi]
            sa = jnp.sum(s * qt_ref[ks, :], axis=0, keepdims=True)
            s_new = (s * wt_ref[ks, :] + sa * bt_ref[ks, :]
                     + vt_ref[pl.ds(row0 + vi, 1), :] * kt_ref[ks, :])
            yt_ref[pl.ds(row0 + vi, 1), :] = jnp.sum(s_new * rt_ref[ks, :], axis=0, keepdims=True)
            sout_ref[hl, vi] = s_new
        return carry

    lax.fori_loop(0, HEAD_DIM, per_value_row, 0, unroll=2)

    @pl.when(hp == pl.num_programs(0) - 1)
    def _():
        out_ref[...] = _rwkv_output(yt_ref[...].T, bonus_ref[...], g_ref[...], lg_ref[...],
                                    lb_ref[...], e_ref[...], wo_ref[...])


def _rwkv_step(prw, prev, state, rowp, wla, g2b, eones, wo):
    n = prw.shape[0]
    consts = list(rowp) + [wla, g2b, eones, wo]
    state_t = jnp.transpose(state, (1, 2, 3, 0))
    sspec = pl.BlockSpec((2, HEAD_DIM, HEAD_DIM, n), lambda hp: (hp, 0, 0, 0))
    out, s_new_t = pl.pallas_call(
        _rwkv_step_kernel,
        grid=(N_PAIRS,),
        in_specs=[_full_spec((n, SHIFT_WIDTH)), _full_spec((n, SHIFT_WIDTH)), sspec]
        + [_full_spec(c.shape) for c in consts],
        out_specs=[_full_spec((n, D_MODEL)), sspec],
        out_shape=[jax.ShapeDtypeStruct((n, D_MODEL), F32),
                   jax.ShapeDtypeStruct((RW_HEADS, HEAD_DIM, HEAD_DIM, n), F32)],
        scratch_shapes=[pltpu.VMEM((RW_WIDTH, n), F32) for _ in range(7)]
        + [pltpu.VMEM((n, RW_WIDTH), F32) for _ in range(2)],
        compiler_params=_params(("arbitrary",)),
        name="rwkv_step",
    )(prw, prev, state_t, *consts)
    return out, jnp.transpose(s_new_t, (3, 0, 1, 2))


def _zoh(lr, li, dt):
    mag = jnp.exp(lr * dt)
    ar, ai = mag * jnp.cos(li * dt), mag * jnp.sin(li * dt)
    den = lr * lr + li * li
    fr = ((ar - 1.0) * lr + ai * li) / den
    fi = (ai * lr - (ar - 1.0) * li) / den
    return ar, ai, fr, fi


def _dot_split(a, b):
    a1 = a.astype(BF16)
    a2 = (a - a1.astype(F32)).astype(BF16)
    b1 = b.astype(BF16)
    b2 = (b - b1.astype(F32)).astype(BF16)
    d = lambda x, y: jnp.dot(x, y, preferred_element_type=F32)
    return d(a1, b1) + (d(a1, b2) + d(a2, b1))


def _s5_prep_kernel(lrc_ref, lic_ref, lrr_ref, lir_ref, ldt_ref, bre_ref, bim_ref, cr_ref, ci_ref,
                    m_ref, pre_ref, pim_ref, cre_ref, cim_ref, a16_ref,
                    brt_ref, bit_ref, arr_ref, air_ref):
    L, C, N = S5_CHUNK, SSM_GROUP, SSM_STATE
    lag = lax.broadcasted_iota(jnp.int32, (1, LANES), 1).astype(F32)
    sel_row = lax.broadcasted_iota(jnp.int32, (LANES, L * C), 0)
    sel_lag = lax.shift_right_logical(lax.broadcasted_iota(jnp.int32, (LANES, L * C), 1), 4)
    pick = lambda cond: jnp.where(cond, 1.0, 0.0).astype(BF16)
    r_same, r_next, r_rev = (pick(sel_row == sel_lag), pick(sel_row == sel_lag + 1),
                             pick(sel_row == (L - 1) - sel_lag))

    def select(e, r, dims=(((1,), (0,)), ((), ()))):
        d = lambda h: lax.dot_general(h, r, dims, preferred_element_type=F32)
        h1, h2, h3 = _split3(e)
        return d(h1) + d(h2) + d(h3)

    spread = select
    on_rows = (((0,), (0,)), ((), ()))
    r_tile = pick(lax.broadcasted_iota(jnp.int32, (C, L * C), 0)
                  == (lax.broadcasted_iota(jnp.int32, (C, L * C), 1) & (C - 1)))
    ident = pick(lax.broadcasted_iota(jnp.int32, (N, N), 0)
                 == lax.broadcasted_iota(jnp.int32, (N, N), 1))

    lane2 = lax.broadcasted_iota(jnp.int32, (C, L * C), 1)
    zero = jnp.zeros((N, L * C), F32)
    a16r, a16i = [], []
    for k in range(2):
        dt = jnp.exp(ldt_ref[k])
        lrc, lic = lrc_ref[k], lic_ref[k]
        mag = jnp.exp(lrc * dt * lag)
        er, ei = mag * jnp.cos(lic * dt * lag), mag * jnp.sin(lic * dt * lag)
        e0r, e0i = spread(er, r_same), spread(ei, r_same)
        e1r, e1i = spread(er, r_next), spread(ei, r_next)
        e2r, e2i = spread(er, r_rev), spread(ei, r_rev)
        _, _, frc, fic = _zoh(lrc, lic, dt)
        brl, bil = select(bre_ref[k], r_tile), select(bim_ref[k], r_tile)
        crl = select(cr_ref[k], r_tile, on_rows)
        cil = select(ci_ref[k], r_tile, on_rows)
        bbr = frc * brl - fic * bil
        bbi = frc * bil + fic * brl
        halves = lambda x: jnp.concatenate([x, zero] if k == 0 else [zero, x], axis=1)
        rows = slice(k * N, (k + 1) * N)
        pre_ref[0, rows, :] = halves(e2r * bbr - e2i * bbi).astype(BF16)
        pim_ref[0, rows, :] = halves(e2r * bbi + e2i * bbr).astype(BF16)
        cre_ref[0, rows, :] = halves(crl * e1r - cil * e1i).astype(BF16)
        cim_ref[0, rows, :] = halves(-(crl * e1i + cil * e1r)).astype(BF16)

        lrr, lir = lrr_ref[k], lir_ref[k]
        arr, air, frr, fir = _zoh(lrr, lir, dt)
        bret, bimt = select(bre_ref[k], ident, on_rows), select(bim_ref[k], ident, on_rows)
        brt = frr * bret - fir * bimt
        bit = frr * bimt + fir * bret
        brt_ref[k], bit_ref[k], arr_ref[k], air_ref[k] = brt, bit, arr, air
        m16 = jnp.exp(lrr * dt * L)
        a16r.append(m16 * jnp.cos(lir * dt * L))
        a16i.append(m16 * jnp.sin(lir * dt * L))
        klt = (_dot_split(brt, crl * e0r - cil * e0i)
               - _dot_split(bit, crl * e0i + cil * e0r))
        for j in range(L):
            blk = klt if j == 0 else jnp.where(lane2 >= j * C, pltpu.roll(klt, j * C, axis=1), 0.0)
            m_ref[k, j * C:(j + 1) * C, :] = blk.astype(BF16)
    a16_ref[0, 0:1, :] = jnp.concatenate(a16r, axis=1)
    a16_ref[0, 1:2, :] = jnp.concatenate(a16i, axis=1)


def _s5_prep(A_re, A_im, log_dt, B_re, B_im, C_re, C_im):
    L, G, N, C = S5_CHUNK, SSM_GROUPS, SSM_STATE, SSM_GROUP
    ins = [A_re[:, :, None], A_im[:, :, None], A_re[:, None, :], A_im[:, None, :],
           log_dt[:, None, None], B_re, B_im, C_re, C_im]
    two = lambda a: pl.BlockSpec((2,) + a.shape[1:], lambda p: (p, 0, 0))
    one = lambda r, c: pl.BlockSpec((1, r, c), lambda p: (p, 0, 0))
    outs = [((G, L * C, L * C), BF16), ((G // 2, 2 * N, 2 * L * C), BF16),
            ((G // 2, 2 * N, 2 * L * C), BF16), ((G // 2, 2 * N, 2 * L * C), BF16),
            ((G // 2, 2 * N, 2 * L * C), BF16), ((G // 2, 2, 2 * N), F32),
            ((G, C, N), F32), ((G, C, N), F32), ((G, 1, N), F32), ((G, 1, N), F32)]
    out_specs = [pl.BlockSpec((2, L * C, L * C), lambda p: (p, 0, 0))] \
        + [one(2 * N, 2 * L * C)] * 4 + [one(2, 2 * N)] \
        + [pl.BlockSpec((2,) + s[1:], lambda p: (p, 0, 0)) for s, _ in outs[6:]]
    res = pl.pallas_call(
        _s5_prep_kernel,
        grid=(G // 2,),
        in_specs=[two(a) for a in ins],
        out_specs=out_specs,
        out_shape=[jax.ShapeDtypeStruct(s, d) for s, d in outs],
        compiler_params=_params(("arbitrary",)),
        name="s5_prep",
    )(*ins)
    names = ('m', 'pre', 'pim', 'cre', 'cim', 'a16', 'brt', 'bit', 'arr', 'air')
    return dict(zip(names, res))


S5_GQ = LANES // SSM_GROUP


def _s5_prompt_kernel(u_ref, m_ref, pre_ref, pim_ref, cre_ref, cim_ref, a16_ref,
                      y_ref, xr_ref, xi_ref,
                      u2_ref, bre_ref, bim_ref, xpr_ref, xpi_ref, *, nb, nch):
    L, C, GQ = S5_CHUNK, SSM_GROUP, S5_GQ
    npair = GQ // 2

    def relayout_in(b, carry):
        uts = [u_ref[b, pl.ds(j, nch, stride=L), :].T for j in range(L)]
        for g in range(GQ):
            for hf in range(2):
                xt = jnp.concatenate([ut[g * C:(g + 1) * C, :]
                                      for ut in uts[hf * L // 2:(hf + 1) * L // 2]], axis=0)
                u2_ref[g, hf, pl.ds(b, nch, stride=nb), :] = xt.T
        return carry

    u2 = lambda g: jnp.concatenate([u2_ref[g, 0], u2_ref[g, 1]], axis=1)

    lax.fori_loop(0, nb, relayout_in, 0)

    for m in range(npair):
        ub2 = jnp.concatenate([u2(2 * m), u2(2 * m + 1)], axis=1).astype(BF16)
        bre_ref[m] = _dot_nt(ub2, pre_ref[m])
        bim_ref[m] = _dot_nt(ub2, pim_ref[m])

    ars = [a16_ref[m, 0:1, :] for m in range(npair)]
    ais = [a16_ref[m, 1:2, :] for m in range(npair)]

    def step(cidx, carry):
        o = pl.multiple_of(cidx * nb, nb)
        out = []
        for m in range(npair):
            xr, xi = carry[2 * m], carry[2 * m + 1]
            xpr_ref[m, pl.ds(o, nb), :] = xr
            xpi_ref[m, pl.ds(o, nb), :] = xi
            out.append(ars[m] * xr - ais[m] * xi + bre_ref[m, pl.ds(o, nb), :])
            out.append(ars[m] * xi + ais[m] * xr + bim_ref[m, pl.ds(o, nb), :])
        return tuple(out)

    z = jnp.zeros((nb, LANES), F32)
    fin = lax.fori_loop(0, nch, step, (z,) * GQ)
    for m in range(npair):
        xr_ref[m] = fin[2 * m]
        xi_ref[m] = fin[2 * m + 1]

    for m in range(npair):
        ys = _dot(xpr_ref[m], cre_ref[m]) + _dot(xpi_ref[m], cim_ref[m])
        for k in range(2):
            g = 2 * m + k
            yg = _dot(u2(g), m_ref[g]) + ys[:, k * L * C:(k + 1) * L * C]
            u2_ref[g, 0] = yg[:, :LANES]
            u2_ref[g, 1] = yg[:, LANES:]

    def relayout_out(b, carry):
        for hf in range(2):
            yts = [u2_ref[g, hf, pl.ds(b, nch, stride=nb), :].T for g in range(GQ)]
            for i in range(L // 2):
                yt = jnp.concatenate([t_[i * C:(i + 1) * C, :] for t_ in yts], axis=0)
                y_ref[b, pl.ds(hf * L // 2 + i, nch, stride=L), :] = yt.T
        return carry

    lax.fori_loop(0, nb, relayout_out, 0)


def _s5_prompt(u, tabs, nb, t):
    L, G, N, C = S5_CHUNK, SSM_GROUPS, SSM_STATE, SSM_GROUP
    nch = t // L
    nc = nch * nb
    nq = SSM_WIDTH // LANES
    npair = S5_GQ // 2
    spec = lambda blk: pl.BlockSpec(blk, lambda q: (q, 0, 0))
    y, xr, xi = pl.pallas_call(
        functools.partial(_s5_prompt_kernel, nb=nb, nch=nch),
        grid=(nq,),
        in_specs=[pl.BlockSpec((nb, t, LANES), lambda q: (0, 0, q), pipeline_mode=pl.Buffered(1)),
                  spec((S5_GQ, L * C, L * C)),
                  spec((npair, 2 * N, 2 * L * C)), spec((npair, 2 * N, 2 * L * C)),
                  spec((npair, 2 * N, 2 * L * C)), spec((npair, 2 * N, 2 * L * C)),
                  spec((npair, 2, 2 * N))],
        out_specs=[pl.BlockSpec((nb, t, LANES), lambda q: (0, 0, q)),
                   spec((npair, nb, 2 * N)), spec((npair, nb, 2 * N))],
        out_shape=[jax.ShapeDtypeStruct((nb, t, SSM_WIDTH), F32),
                   jax.ShapeDtypeStruct((G // 2, nb, 2 * N), F32),
                   jax.ShapeDtypeStruct((G // 2, nb, 2 * N), F32)],
        scratch_shapes=[pltpu.VMEM((S5_GQ, 2, nc, LANES), F32)]
        + [pltpu.VMEM((npair, nc, 2 * N), F32) for _ in range(4)],
        compiler_params=_params(("arbitrary",)),
        name="s5_prompt",
    )(u.reshape(nb, t, SSM_WIDTH), tabs['m'], tabs['pre'], tabs['pim'], tabs['cre'], tabs['cim'],
      tabs['a16'])
    unpair = lambda x: x.reshape(G // 2, nb, 2, N).transpose(1, 0, 2, 3).reshape(nb, G, N)
    return y.reshape(nb * t, SSM_WIDTH), unpair(xr), unpair(xi)


def _s5_step_kernel(u_ref, xr_ref, xi_ref, ar_ref, ai_ref, brt_ref, bit_ref, cr_ref, ci_ref,
                    y_ref, nr_ref, ni_ref, bd_ref):
    G, N, C = SSM_GROUPS, SSM_STATE, SSM_GROUP

    def block_diag(slot, src_ref):
        bd_ref[slot] = jnp.zeros((G * C, G * N), BF16)
        for g in range(G):
            bd_ref[slot, g * C:(g + 1) * C, g * N:(g + 1) * N] = src_ref[g].astype(BF16)
        return bd_ref[slot]

    ub = u_ref[...].astype(BF16)
    xr, xi, ar, ai = xr_ref[...], xi_ref[...], ar_ref[...], ai_ref[...]
    nr = ar * xr - ai * xi + jnp.dot(ub, block_diag(0, brt_ref), preferred_element_type=F32)
    ni = ar * xi + ai * xr + jnp.dot(ub, block_diag(1, bit_ref), preferred_element_type=F32)
    nr_ref[...] = nr
    ni_ref[...] = ni
    y_ref[...] = _dot_nt(nr, block_diag(2, cr_ref)) - _dot_nt(ni, block_diag(3, ci_ref))


def _s5_step(u, st_re, st_im, tabs, C_re, C_im):
    G, N, C = SSM_GROUPS, SSM_STATE, SSM_GROUP
    n = u.shape[0]
    ops = (u, st_re.reshape(n, G * N), st_im.reshape(n, G * N),
           tabs['arr'].reshape(1, G * N), tabs['air'].reshape(1, G * N),
           tabs['brt'], tabs['bit'], C_re, C_im)
    y, nr, ni = pl.pallas_call(
        _s5_step_kernel,
        grid=(1,),
        in_specs=[_full_spec(o.shape) for o in ops],
        out_specs=[_full_spec((n, SSM_WIDTH)), _full_spec((n, G * N)), _full_spec((n, G * N))],
        out_shape=[jax.ShapeDtypeStruct((n, SSM_WIDTH), F32),
                   jax.ShapeDtypeStruct((n, G * N), F32),
                   jax.ShapeDtypeStruct((n, G * N), F32)],
        scratch_shapes=[pltpu.VMEM((4, G * C, G * N), BF16)],
        compiler_params=_params(("arbitrary",)),
        name="s5_step",
    )(*ops)
    return y, nr.reshape(n, G, N), ni.reshape(n, G, N)


def _tail_kernel(*refs, tm, nsub, tiles_per_seq, decode, final):
    if decode:
        (x_ref, rw_ref, ys_ref, u_ref, gl_ref, pe_ref, st2_ref, st1_ref,
         dsk_ref, wglu_ref, wout_ref, ln2_ref, wfi_ref, cw_ref, cb_ref, wfo_ref,
         ln3_ref, wpg_ref, wple_ref, fin_ref, y_ref, conv_ref) = refs
    else:
        (x_ref, rw_ref, ys_ref, u_ref, gl_ref, pe_ref,
         dsk_ref, wglu_ref, wout_ref, ln2_ref, wfi_ref, cw_ref, cb_ref, wfo_ref,
         ln3_ref, wpg_ref, wple_ref, fin_ref, y_ref, conv_ref, carry_ref) = refs

    sub = tm // nsub
    cw = cw_ref[...]
    last2 = {}
    if not decode:
        @pl.when(pl.program_id(0) % tiles_per_seq == 0)
        def _():
            carry_ref[...] = jnp.zeros_like(carry_ref)

        for cols in FFN_CHUNKS:
            last2[-1, cols.start] = carry_ref[:, cols]

    def row_block(s):
        rs = slice(s * sub, (s + 1) * sub)
        x = x_ref[rs, :]
        z = _gelu(ys_ref[rs, :] + dsk_ref[...] * u_ref[rs, :])
        zz = _dot(z, wglu_ref[...])
        yield
        s5 = zz[:, :D_MODEL] * _sigmoid(zz[:, D_MODEL:])
        gl = gl_ref[rs, :]
        merged = _sigmoid(gl[:, :D_MODEL]) * rw_ref[rs, :] + _sigmoid(gl[:, D_MODEL:]) * s5
        x = x + _dot(merged, wout_ref[...])
        yield
        h2 = _rmsnorm(x, ln2_ref[...]).astype(BF16)
        rows = lax.broadcasted_iota(jnp.int32, (sub, 1), 0)

        def up(cols):
            a_c = jnp.dot(h2, wfi_ref[:, cols], preferred_element_type=F32)
            b_c = jnp.dot(h2, wfi_ref[:, slice(D_FF + cols.start, D_FF + cols.stop)],
                          preferred_element_type=F32)
            if decode:
                conv_ref[rs, cols] = a_c
            else:
                last2[s, cols.start] = a_c[sub - 2:sub, :]
            return a_c, b_c

        def down(a_c, b_c, cols):
            if decode:
                prev2, prev1 = st2_ref[rs, cols], st1_ref[rs, cols]
            else:
                before = last2[s - 1, cols.start]
                c0, c1 = before[0:1, :], before[1:2, :]
                prev1 = jnp.where(rows == 0, c1, pltpu.roll(a_c, 1, axis=0))
                prev2 = jnp.where(rows == 0, c0,
                                  jnp.where(rows == 1, c1, pltpu.roll(a_c, 2, axis=0)))
            a_conv = (cw[0:1, cols] * prev2 + cw[1:2, cols] * prev1 + cw[2:3, cols] * a_c
                      + cb_ref[:, cols])
            return _dot(_gelu(a_conv) * b_c, wfo_ref[cols, :])

        ups = {0: up(FFN_CHUNKS[0])}
        for c, cols in enumerate(FFN_CHUNKS):
            if c + 1 < len(FFN_CHUNKS):
                ups[c + 1] = up(FFN_CHUNKS[c + 1])
            yield
            x = x + down(*ups.pop(c), cols)
        yield
        pg = _sigmoid(_dot(_rmsnorm(x, ln3_ref[...]), wpg_ref[...]))
        x = x + pg * _dot(pe_ref[rs, :], wple_ref[...])
        y_ref[rs, :] = _rmsnorm(x, fin_ref[...]) if final else x
        yield

    blocks = [row_block(s) for s in range(nsub)]
    for _ in range(4 + len(FFN_CHUNKS)):
        for blk in blocks:
            next(blk)
    if not decode:
        for cols in FFN_CHUNKS:
            carry_ref[:, cols] = last2[nsub - 1, cols.start]
            conv_ref[0, :, cols] = last2[nsub - 1, cols.start]


def _tail(x2d, rw, ys, u, gl, pe, st_conv, D_skip, wglu, wout, ln2_g, wfi, conv_w, conv_b, wfo,
          ln3_g, wpg, wple, final_g, *, tm, tiles_per_seq, decode, final):
    n = x2d.shape[0]
    row = lambda w: pl.BlockSpec((tm, w), lambda i: (i, 0))
    consts = [D_skip.reshape(1, SSM_WIDTH), wglu, wout, ln2_g.reshape(1, D_MODEL), wfi,
              conv_w, conv_b.reshape(1, D_FF), wfo, ln3_g.reshape(1, D_MODEL), wpg, wple,
              final_g.reshape(1, D_MODEL)]
    acts = [x2d, rw, ys, u, gl, pe]
    in_specs = [row(a.shape[1]) for a in acts]
    scratch = []
    if decode:
        acts += [st_conv[:, 0], st_conv[:, 1]]
        in_specs += [row(D_FF), row(D_FF)]
        conv_spec = row(D_FF)
        conv_shape = (n, D_FF)
    else:
        nseq = n // (tm * tiles_per_seq)
        conv_spec = pl.BlockSpec((1, 2, D_FF), lambda i: (i // tiles_per_seq, 0, 0))
        conv_shape = (nseq, 2, D_FF)
        scratch = [pltpu.VMEM((2, D_FF), F32)]
    in_specs += [pl.BlockSpec(c.shape, lambda i, nd=c.ndim: (0,) * nd,
                              pipeline_mode=pl.Buffered(1)) for c in consts]
    y, conv = pl.pallas_call(
        functools.partial(_tail_kernel, tm=tm, nsub=1 if decode else TAIL_ROW_BLOCKS,
                          tiles_per_seq=tiles_per_seq, decode=decode,
                          final=final),
        grid=(n // tm,),
        in_specs=in_specs,
        out_specs=[row(D_MODEL), conv_spec],
        out_shape=[jax.ShapeDtypeStruct((n, D_MODEL), F32),
                   jax.ShapeDtypeStruct(conv_shape, F32)],
        scratch_shapes=scratch,
        compiler_params=_params(("arbitrary",)),
        name="tail_decode" if decode else "tail_prompt",
    )(*acts, *consts)
    if decode:
        conv = jnp.stack([st_conv[:, 1], conv], axis=1)
    return y, conv


def _layer(x, pe, states, W, *, decode, final, tm_proj, tt, tm_tail):
    nb, t, _ = x.shape
    n = nb * t
    x2d = x.reshape(n, D_MODEL)
    to_cast = [k for k in TAIL_WEIGHTS if W[k].dtype != BF16]
    (prw, u, gl), casted = _proj(x2d, W['ln1_g'], W['w_in'], tm_proj, [W[k] for k in to_cast])
    W.update(zip(to_cast, casted))
    if decode:
        st_shift, st_wkv, st_re, st_im, st_conv = states
        rw, s_new = _rwkv_step(prw, st_shift, st_wkv, W['rowp'], W['wla'], W['g2'], W['eones'],
                               W['w_rw_out'])
        ys, xr, xi = _s5_step(u, st_re, st_im, W['s5'], W['C_re'], W['C_im'])
        new_shift = prw
    else:
        st_conv = None
        rw, s_new = _rwkv_prompt(prw, W['rowp'], W['wla'], W['g2'], W['eones'], W['w_rw_out'],
                                 nb, t, tt)
        ys, xr, xi = _s5_prompt(u, W['s5'], nb, t)
        new_shift = prw.reshape(nb, t, SHIFT_WIDTH)[:, -1]
    y, new_conv = _tail(x2d, rw, ys, u, gl, pe.reshape(n, PLE_DIM), st_conv, W['D_skip'],
                        W['w_glu'], W['w_out'], W['ln2_g'], W['w_ffn_in'], W['conv_w'],
                        W['conv_b'], W['w_ffn_out'], W['ln3_g'], W['w_ple_gate'], W['w_ple'],
                        W['final_g'], tm=tm_tail, tiles_per_seq=max(t // tm_tail, 1),
                        decode=decode, final=final)
    return y.reshape(nb, t, D_MODEL), (new_shift, s_new, xr, xi, new_conv)


def kernel(x_prompt, x_sample, p_prompt, p_sample, state_shift, state_wkv, state_ssm_re, state_ssm_im, state_conv, ln1_g, w_in, mu_shift, w0, w2, a0, a2, g2, k_k, k_a, r_k, lnx_g, lnx_b, w_rw_out, A_re, A_im, log_dt, B_re, B_im, C_re, C_im, D_skip, w_glu, w_out, ln2_g, w_ffn_in, conv_w, conv_b, w_ffn_out, ln3_g, w_ple_gate, w_ple, final_g):
    depth = w_in.shape[0]
    xp, xs = x_prompt, x_sample
    pst = [[] for _ in range(5)]
    sst = [[] for _ in range(5)]
    for i in range(depth):
        wla, g2b, wo = _rwkv_weights(w2[i], a2[i], g2[i], w_rw_out[i])
        W = dict(
            ln1_g=ln1_g[i], w_in=w_in[i],
            rowp=_row_params(mu_shift[i], w0[i], a0[i], k_k[i], k_a[i], r_k[i].reshape(-1),
                             lnx_g[i], lnx_b[i]),
            wla=wla, g2=g2b, eones=_rwkv_consts(), w_rw_out=wo,
            s5=_s5_prep(A_re[i], A_im[i], log_dt[i], B_re[i], B_im[i], C_re[i], C_im[i]),
            C_re=C_re[i], C_im=C_im[i], D_skip=D_skip[i],
            w_glu=w_glu[i], w_out=w_out[i], ln2_g=ln2_g[i],
            w_ffn_in=w_ffn_in[i], conv_w=conv_w[i], conv_b=conv_b[i],
            w_ffn_out=w_ffn_out[i], ln3_g=ln3_g[i],
            w_ple_gate=w_ple_gate[i], w_ple=w_ple[i], final_g=final_g)
        final = i == depth - 1
        tp = xp.shape[1]
        xp, sp = _layer(xp, p_prompt[i], None, W, decode=False, final=final,
                        tm_proj=min(TM_PROJ, tp), tt=min(TT_RWKV, tp), tm_tail=min(TM_TAIL, tp))
        ns = xs.shape[0]
        xs, ss = _layer(xs, p_sample[i],
                        (state_shift[i], state_wkv[i], state_ssm_re[i], state_ssm_im[i],
                         state_conv[i]),
                        W, decode=True, final=final, tm_proj=ns, tt=None, tm_tail=ns)
        for j in range(5):
            pst[j].append(sp[j])
            sst[j].append(ss[j])
    return (xp, xs,
            jnp.stack(pst[0]), jnp.stack(pst[1]), jnp.stack(pst[2]), jnp.stack(pst[3]),
            jnp.stack(pst[4]),
            jnp.stack(sst[0]), jnp.stack(sst[1]), jnp.stack(sst[2]), jnp.stack(sst[3]),
            jnp.stack(sst[4]))
```

```python
import functools
import math

import jax
import jax.numpy as jnp
import numpy as np
from jax import lax
from jax.experimental import pallas as pl
from jax.experimental.pallas import tpu as pltpu

F32 = jnp.float32
BF16 = jnp.bfloat16

D_MODEL = 1024
RW_HEADS = 8
HEAD_DIM = 64
RW_WIDTH = RW_HEADS * HEAD_DIM
LORA_W = 64
LORA_A = 64
LORA_G = 128
SHIFT_WIDTH = 3 * RW_WIDTH + LORA_W + LORA_A + LORA_G
SSM_WIDTH = 512
SSM_GROUP = 16
SSM_GROUPS = SSM_WIDTH // SSM_GROUP
SSM_STATE = 64
GATE_OFF = SHIFT_WIDTH + SSM_WIDTH
IN_WIDTH = SHIFT_WIDTH + SSM_WIDTH + 2 * D_MODEL
D_FF = 2816
PLE_DIM = 256
EPS = 1e-6
GN_EPS = 64e-5

LANES = 128
WKV_CHUNK = 64
S5_CHUNK = 16
N_PAIRS = RW_HEADS // 2
VMEM_LIMIT = 60 * 1024 * 1024
TM_PROJ = 512
TT_RWKV = 256
TM_TAIL = 512
RWKV_SEQ_GROUPS = 2
TAIL_ROW_BLOCKS = 2


def _dot(a, b):
    return jnp.dot(a.astype(BF16), b.astype(BF16), preferred_element_type=F32)


def _dot_nt(a, b):
    return lax.dot_general(a.astype(BF16), b.astype(BF16), (((1,), (1,)), ((), ())),
                           preferred_element_type=F32)


def _dot_tn(a, b):
    return lax.dot_general(a.astype(BF16), b.astype(BF16), (((0,), (0,)), ((), ())),
                           preferred_element_type=F32)


def _split3(x):
    h1 = x.astype(BF16)
    r1 = x - h1.astype(F32)
    h2 = r1.astype(BF16)
    h3 = (r1 - h2.astype(F32)).astype(BF16)
    return h1, h2, h3


def _dot_exact_lhs(a_bf16, x):
    h1, h2, h3 = _split3(x)
    d = lambda h: jnp.dot(a_bf16, h, preferred_element_type=F32)
    return d(h1) + d(h2) + d(h3)


def _sigmoid(x):
    return 1.0 / (1.0 + jnp.exp(-x))


def _gelu(x):
    c = math.sqrt(2.0 / math.pi)
    return 0.5 * x * (1.0 + jnp.tanh(c * (x + 0.044715 * (x * x * x))))


def _rmsnorm(x, g):
    return x * lax.rsqrt(jnp.mean(x * x, axis=-1, keepdims=True) + EPS) * g


def _full_spec(shape):
    nd = len(shape)
    return pl.BlockSpec(shape, lambda *_: (0,) * nd)


def _params(sem):
    return pltpu.CompilerParams(dimension_semantics=sem, vmem_limit_bytes=VMEM_LIMIT)


BF16_ROWS = 16
FFN_CHUNKS = (slice(0, 768), slice(768, 1536), slice(1536, 2304), slice(2304, D_FF))
TAIL_WEIGHTS = ('w_glu', 'w_out', 'w_ffn_in', 'w_ffn_out', 'w_ple_gate', 'w_ple')


def _proj_kernel(x_ref, g_ref, w_ref, *refs, n_cast):
    cast_in, (prw_ref, u_ref, gl_ref), cast_out = refs[:n_cast], refs[n_cast:n_cast + 3], \
        refs[n_cast + 3:]
    h = _rmsnorm(x_ref[...], g_ref[...])
    prw_ref[...] = _dot(h, w_ref[:, :SHIFT_WIDTH])
    u_ref[...] = _dot(h, w_ref[:, SHIFT_WIDTH:GATE_OFF])
    gl_ref[...] = _dot(h, w_ref[:, GATE_OFF:])
    for src, dst in zip(cast_in, cast_out):
        dst[...] = src[...].astype(BF16)


def _proj(x2d, ln1_g, w_in, tm, cast_ws=()):
    n = x2d.shape[0]
    steps = n // tm
    row = lambda w: pl.BlockSpec((tm, w), lambda i: (i, 0))

    def slab(w):
        rows = w.shape[0]
        span = 1
        while rows % (steps // span) or (rows // (steps // span)) % BF16_ROWS:
            span *= 2
        return pl.BlockSpec((rows // (steps // span), w.shape[1]), lambda i: (i // span, 0))

    slabs = [slab(w) for w in cast_ws]
    res = pl.pallas_call(
        functools.partial(_proj_kernel, n_cast=len(cast_ws)),
        grid=(steps,),
        in_specs=[row(D_MODEL), _full_spec((1, D_MODEL)),
                  pl.BlockSpec((D_MODEL, IN_WIDTH), lambda i: (0, 0),
                               pipeline_mode=pl.Buffered(1))] + slabs,
        out_specs=[row(SHIFT_WIDTH), row(SSM_WIDTH), row(2 * D_MODEL)] + slabs,
        out_shape=[jax.ShapeDtypeStruct((n, SHIFT_WIDTH), F32),
                   jax.ShapeDtypeStruct((n, SSM_WIDTH), F32),
                   jax.ShapeDtypeStruct((n, 2 * D_MODEL), F32)]
        + [jax.ShapeDtypeStruct(w.shape, BF16) for w in cast_ws],
        compiler_params=_params(("arbitrary",)),
        name="proj",
    )(x2d, ln1_g.reshape(1, D_MODEL), w_in, *cast_ws)
    return res[:3], res[3:]


def _rwkv_token_math_stages(p, prev, mu, w0, a0, k_k, k_a, r_k, wla, g2, eones):
    xs = p + (prev - p) * mu
    r = xs[:, 0:RW_WIDTH]
    k = xs[:, RW_WIDTH:2 * RW_WIDTH]
    v = xs[:, 2 * RW_WIDTH:3 * RW_WIDTH]
    xwa = xs[:, 3 * RW_WIDTH:3 * RW_WIDTH + LORA_W + LORA_A]
    xg = xs[:, 3 * RW_WIDTH + LORA_W + LORA_A:]
    lane = lax.broadcasted_iota(jnp.int32, xwa.shape, 1)
    la = _dot(jnp.where(lane < LORA_W, jnp.tanh(xwa), xwa), wla)
    g = _dot(_sigmoid(xg), g2)
    yield None
    wpre = w0 + la[:, :RW_WIDTH]
    softplus = jnp.maximum(-wpre, 0.0) + jnp.log(1.0 + jnp.exp(-jnp.abs(wpre)))
    lw = -jnp.exp(-softplus - 0.5)
    a = _sigmoid(a0 + la[:, RW_WIDTH:])
    kkr = k * k_k
    ss = _dot(kkr * kkr, eones)
    kk = kkr / jnp.maximum(jnp.sqrt(ss), 1e-12)
    kp = k * (1.0 + (a - 1.0) * k_a)
    bonus = _dot(r * kp * r_k, eones) * v
    yield r, v, lw, a, g, kk, kp, bonus


def _rwkv_token_math(*args):
    *_, out = _rwkv_token_math_stages(*args)
    return out


def _rwkv_output(y, bonus, g, lnx_g, lnx_b, eones, w_o):
    emean = eones * (1.0 / HEAD_DIM)
    mean = _dot(y, emean)
    d = y - mean
    var = _dot(d * d, emean)
    yn = d * lax.rsqrt(var + GN_EPS) * lnx_g + lnx_b
    return _dot((yn + bonus) * g, w_o)


def _head_masks(shape):
    lane = lax.broadcasted_iota(jnp.int32, shape, 1)
    return lane < HEAD_DIM, lane >= HEAD_DIM


def _stack_heads(x, m_a, m_b):
    return jnp.concatenate([jnp.where(m_a, x, 0.0), jnp.where(m_b, x, 0.0)], axis=0)


def _rwkv_prompt_kernel(p_ref, mu_ref, w0_ref, a0_ref, kk_ref, ka_ref, rk_ref, lg_ref, lb_ref,
                        wla_ref, g2_ref, e_ref, ltri_ref, wo_ref,
                        out_ref, sout_ref,
                        carry_ref, sp_ref, y_ref, pend_ref, pendf_ref, gend_ref,
                        new_ref, newf_ref, gnew_ref, *, tt, nt, nh):
    s = pl.program_id(0)
    nch = tt // WKV_CHUNK
    eones = e_ref[...]
    halves = range(nh)
    pairs = range(N_PAIRS)
    rows_of = lambda ch: slice(ch * WKV_CHUNK, (ch + 1) * WKV_CHUNK)
    lanes_of = lambda j: slice(j * LANES, (j + 1) * LANES)

    @pl.when(s == 0)
    def _():
        pend_ref[...] = jnp.zeros_like(pend_ref)
        pendf_ref[...] = jnp.zeros_like(pendf_ref)
        gend_ref[...] = jnp.zeros_like(gend_ref)
        sp_ref[...] = jnp.zeros_like(sp_ref)

    @pl.when(s % nt == 0)
    def _():
        carry_ref[...] = jnp.zeros_like(carry_ref)

    @pl.when(s % nt == 1 % nt)
    def _():
        sp_ref[...] = jnp.zeros_like(sp_ref)

    rows = lax.broadcasted_iota(jnp.int32, (tt, 1), 0)
    token_math = []
    for h in halves:
        p = p_ref[h]
        prev = jnp.where(rows == 0, carry_ref[h], pltpu.roll(p, 1, axis=0))
        carry_ref[h] = p[tt - 1:tt, :]
        token_math.append(_rwkv_token_math_stages(
            p, prev, mu_ref[...], w0_ref[...], a0_ref[...], kk_ref[...], ka_ref[...],
            rk_ref[...], wla_ref[...], g2_ref[...], eones))
        next(token_math[h])

    m_a, m_b = _head_masks((WKV_CHUNK, LANES))
    ri = lax.broadcasted_iota(jnp.int32, (2 * WKV_CHUNK, 2 * WKV_CHUNK), 0)
    ci = lax.broadcasted_iota(jnp.int32, (2 * WKV_CHUNK, 2 * WKV_CHUNK), 1)
    same_head = (ri >= WKV_CHUNK) == (ci >= WKV_CHUNK)
    strict = same_head & (ri > ci)
    incl = same_head & (ri >= ci)
    blocks = [(h, ch, j) for h in halves for ch in range(nch) for j in pairs]
    blk = lambda i, h, ch, j: pend_ref[h, i, rows_of(ch), lanes_of(j)]
    stack = lambda i, h, ch, j: _stack_heads(blk(i, h, ch, j), m_a, m_b)
    vss = {b: stack(4, *b) for b in blocks}
    gms = {b: _dot_nt(jnp.concatenate([stack(0, *b), stack(1, *b)], axis=0),
                      jnp.concatenate([stack(2, *b), stack(3, *b)], axis=0)) for b in blocks}
    n_abs = {b: jnp.where(strict, gms[b][:LANES, :LANES], 0.0) for b in blocks}
    wvs = {b: _dot(jnp.where(strict, gms[b][:LANES, LANES:], 0.0), vss[b]) for b in blocks}
    a_ys = {b: jnp.concatenate([jnp.where(incl, gms[b][LANES:, :LANES], 0.0),
                                jnp.where(incl, gms[b][LANES:, LANES:], 0.0)],
                               axis=1).astype(BF16) for b in blocks}

    tm = [next(token_math[h]) for h in halves]
    cs = [_dot_exact_lhs(ltri_ref[...], tm[h][2]) for h in halves]

    eye = (ri == ci).astype(F32)
    tinvs = {b: eye + n_abs[b] for b in blocks}
    pows = n_abs
    for _ in range(5):
        pows = {b: _dot(pows[b], pows[b]).astype(BF16) for b in blocks}
        tinvs = {b: tinvs[b] + _dot(pows[b], tinvs[b]) for b in blocks}

    for h in halves:
        r, v_n, lw, a, g_n, kk, kp, bonus_n = tm[h]
        c = cs[h]
        einv = jnp.exp(-c)
        bvec = kk * a
        new_ref[h, 0] = (-kk * jnp.exp(c - lw)).astype(BF16)
        new_ref[h, 1] = (r * jnp.exp(c)).astype(BF16)
        new_ref[h, 2] = (bvec * einv).astype(BF16)
        new_ref[h, 3] = (kp * einv).astype(BF16)
        new_ref[h, 4] = v_n.astype(BF16)
        for ch in range(nch):
            c_end = c[(ch + 1) * WKV_CHUNK - 1:(ch + 1) * WKV_CHUNK, :]
            e_rest = jnp.exp(c_end - c[rows_of(ch), :])
            gnew_ref[h, ch] = jnp.exp(c_end)
            new_ref[h, 5, rows_of(ch), :] = (bvec[rows_of(ch), :] * e_rest).astype(BF16)
            new_ref[h, 6, rows_of(ch), :] = (kp[rows_of(ch), :] * e_rest).astype(BF16)
        newf_ref[h, 0] = bonus_n
        newf_ref[h, 1] = g_n

    sps = {(h, j): sp_ref[h, j] for h in halves for j in pairs}
    for ch in range(nch):
        hj = [(h, j) for h in halves for j in pairs]
        arss = {(h, j): _dot_nt(jnp.concatenate([blk(0, h, ch, j), blk(1, h, ch, j)], axis=0),
                                sps[h, j]) for h, j in hj}
        uvs = {(h, j): jnp.concatenate(
            [_dot(tinvs[h, ch, j],
                  _stack_heads(arss[h, j][:WKV_CHUNK], m_a, m_b) + wvs[h, ch, j]).astype(BF16),
             vss[h, ch, j]], axis=0) for h, j in hj}
        yss = {(h, j): _dot(a_ys[h, ch, j], uvs[h, j]) for h, j in hj}
        incs = {(h, j): _dot_tn(uvs[h, j], jnp.concatenate(
            [stack(5, h, ch, j), stack(6, h, ch, j)], axis=0)) for h, j in hj}
        for h, j in hj:
            y_ref[h, rows_of(ch), lanes_of(j)] = (yss[h, j][:WKV_CHUNK] + yss[h, j][WKV_CHUNK:]
                                                  + arss[h, j][WKV_CHUNK:])
            sps[h, j] = sps[h, j] * gend_ref[h, ch][:, lanes_of(j)] + incs[h, j]
    for h in halves:
        for j in pairs:
            sp_ref[h, j] = sps[h, j]
        out_ref[h] = _rwkv_output(y_ref[h], pendf_ref[h, 0], pendf_ref[h, 1], lg_ref[...],
                                  lb_ref[...], eones, wo_ref[...])

    pend_ref[...] = new_ref[...]
    pendf_ref[...] = newf_ref[...]
    gend_ref[...] = gnew_ref[...]

    @pl.when((s % nt == 0) & (s > 0))
    def _():
        for h in halves:
            for j in pairs:
                sp = sp_ref[h, j]
                sout_ref[h, 0, 2 * j] = sp[:HEAD_DIM, :HEAD_DIM]
                sout_ref[h, 0, 2 * j + 1] = sp[HEAD_DIM:, HEAD_DIM:]


def _rwkv_consts():
    idx = np.arange(RW_WIDTH)
    return jnp.asarray(idx[:, None] // HEAD_DIM == idx[None, :] // HEAD_DIM, dtype=BF16)


def _rwkv_weights(w2, a2, g2, w_rw_out):
    wla = jnp.zeros((LORA_W + LORA_A, 2 * RW_WIDTH), F32)
    wla = wla.at[:LORA_W, :RW_WIDTH].set(w2).at[LORA_W:, RW_WIDTH:].set(a2)
    return wla.astype(BF16), g2.astype(BF16), w_rw_out.astype(BF16)


def _row_params(mu_shift, w0, a0, k_k, k_a, r_k, lnx_g, lnx_b):
    return (mu_shift.reshape(1, SHIFT_WIDTH), w0.reshape(1, RW_WIDTH), a0.reshape(1, RW_WIDTH),
            k_k.reshape(1, RW_WIDTH), k_a.reshape(1, RW_WIDTH), r_k.reshape(1, RW_WIDTH),
            lnx_g.reshape(1, RW_WIDTH), lnx_b.reshape(1, RW_WIDTH))


def _rwkv_prompt(prw, rowp, wla, g2b, eones, wo, nb, t, tt):
    nt = t // tt
    idx = np.arange(tt)
    ltri = jnp.asarray((idx[:, None] // WKV_CHUNK == idx[None, :] // WKV_CHUNK)
                       & (idx[:, None] >= idx[None, :]), dtype=BF16)
    consts = list(rowp) + [wla, g2b, eones, ltri, wo]
    nh = RWKV_SEQ_GROUPS if nb % RWKV_SEQ_GROUPS == 0 else 1
    ntiles = (nb // nh) * nt
    nch = tt // WKV_CHUNK
    done = lambda s: jnp.maximum(s - 1, 0)
    out, s_new = pl.pallas_call(
        functools.partial(_rwkv_prompt_kernel, tt=tt, nt=nt, nh=nh),
        grid=(ntiles + 1,),
        in_specs=[pl.BlockSpec((nh, tt, SHIFT_WIDTH),
                               lambda s: (0, jnp.minimum(s, ntiles - 1), 0))]
        + [_full_spec(c.shape) for c in consts],
        out_specs=[pl.BlockSpec((nh, tt, D_MODEL), lambda s: (0, done(s), 0)),
                   pl.BlockSpec((nh, 1, RW_HEADS, HEAD_DIM, HEAD_DIM),
                                lambda s: (0, done(s) // nt, 0, 0, 0))],
        out_shape=[jax.ShapeDtypeStruct((nh, nb * t // nh, D_MODEL), F32),
                   jax.ShapeDtypeStruct((nh, nb // nh, RW_HEADS, HEAD_DIM, HEAD_DIM), F32)],
        scratch_shapes=[pltpu.VMEM((nh, 1, SHIFT_WIDTH), F32),
                        pltpu.VMEM((nh, N_PAIRS, LANES, LANES), F32),
                        pltpu.VMEM((nh, tt, RW_WIDTH), F32),
                        pltpu.VMEM((nh, 7, tt, RW_WIDTH), BF16),
                        pltpu.VMEM((nh, 2, tt, RW_WIDTH), F32),
                        pltpu.VMEM((nh, nch, 1, RW_WIDTH), F32),
                        pltpu.VMEM((nh, 7, tt, RW_WIDTH), BF16),
                        pltpu.VMEM((nh, 2, tt, RW_WIDTH), F32),
                        pltpu.VMEM((nh, nch, 1, RW_WIDTH), F32)],
        compiler_params=_params(("arbitrary",)),
        name="rwkv_prompt",
    )(prw.reshape(nh, nb * t // nh, SHIFT_WIDTH), *consts)
    return (out.reshape(nb * t, D_MODEL),
            s_new.reshape(nb, RW_HEADS, HEAD_DIM, HEAD_DIM))


def _rwkv_step_kernel(p_ref, prev_ref, s_ref, mu_ref, w0_ref, a0_ref, kk_ref, ka_ref, rk_ref,
                      lg_ref, lb_ref, wla_ref, g2_ref, e_ref, wo_ref,
                      out_ref, sout_ref,
                      qt_ref, wt_ref, rt_ref, bt_ref, kt_ref, vt_ref, yt_ref, g_ref, bonus_ref):
    hp = pl.program_id(0)

    @pl.when(hp == 0)
    def _():
        r, v, lw, a, g, kk, kp, bonus = _rwkv_token_math(
            p_ref[...], prev_ref[...], mu_ref[...], w0_ref[...], a0_ref[...], kk_ref[...],
            ka_ref[...], rk_ref[...], wla_ref[...], g2_ref[...], e_ref[...])
        qt_ref[...] = (-kk).T
        wt_ref[...] = jnp.exp(lw).T
        rt_ref[...] = r.T
        bt_ref[...] = (kk * a).T
        kt_ref[...] = kp.T
        vt_ref[...] = v.T
        g_ref[...] = g
        bonus_ref[...] = bonus

    def per_value_row(vi, carry):
        for hl in range(2):
            row0 = pl.multiple_of((2 * hp + hl) * HEAD_DIM, HEAD_DIM)
            ks = pl.ds(row0, HEAD_DIM)
            s = s_ref[hl, vi]
            sa = jnp.sum(s * qt_ref[ks, :], axis=0, keepdims=True)
            s_new = (s * wt_ref[ks, :] + sa * bt_ref[ks, :]
                     + vt_ref[pl.ds(row0 + vi, 1), :] * kt_ref[ks, :])
            yt_ref[pl.ds(row0 + vi, 1), :] = jnp.sum(s_new * rt_ref[ks, :], axis=0, keepdims=True)
            sout_ref[hl, vi] = s_new
        return carry

    lax.fori_loop(0, HEAD_DIM, per_value_row, 0, unroll=2)

    @pl.when(hp == pl.num_programs(0) - 1)
    def _():
        out_ref[...] = _rwkv_output(yt_ref[...].T, bonus_ref[...], g_ref[...], lg_ref[...],
                                    lb_ref[...], e_ref[...], wo_ref[...])


def _rwkv_step(prw, prev, state, rowp, wla, g2b, eones, wo):
    n = prw.shape[0]
    consts = list(rowp) + [wla, g2b, eones, wo]
    state_t = jnp.transpose(state, (1, 2, 3, 0))
    sspec = pl.BlockSpec((2, HEAD_DIM, HEAD_DIM, n), lambda hp: (hp, 0, 0, 0))
    out, s_new_t = pl.pallas_call(
        _rwkv_step_kernel,
        grid=(N_PAIRS,),
        in_specs=[_full_spec((n, SHIFT_WIDTH)), _full_spec((n, SHIFT_WIDTH)), sspec]
        + [_full_spec(c.shape) for c in consts],
        out_specs=[_full_spec((n, D_MODEL)), sspec],
        out_shape=[jax.ShapeDtypeStruct((n, D_MODEL), F32),
                   jax.ShapeDtypeStruct((RW_HEADS, HEAD_DIM, HEAD_DIM, n), F32)],
        scratch_shapes=[pltpu.VMEM((RW_WIDTH, n), F32) for _ in range(7)]
        + [pltpu.VMEM((n, RW_WIDTH), F32) for _ in range(2)],
        compiler_params=_params(("arbitrary",)),
        name="rwkv_step",
    )(prw, prev, state_t, *consts)
    return out, jnp.transpose(s_new_t, (3, 0, 1, 2))


def _zoh(lr, li, dt):
    mag = jnp.exp(lr * dt)
    ar, ai = mag * jnp.cos(li * dt), mag * jnp.sin(li * dt)
    den = lr * lr + li * li
    fr = ((ar - 1.0) * lr + ai * li) / den
    fi = (ai * lr - (ar - 1.0) * li) / den
    return ar, ai, fr, fi


def _dot_split(a, b):
    a1 = a.astype(BF16)
    a2 = (a - a1.astype(F32)).astype(BF16)
    b1 = b.astype(BF16)
    b2 = (b - b1.astype(F32)).astype(BF16)
    d = lambda x, y: jnp.dot(x, y, preferred_element_type=F32)
    return d(a1, b1) + (d(a1, b2) + d(a2, b1))


def _s5_prep_kernel(lrc_ref, lic_ref, lrr_ref, lir_ref, ldt_ref, bre_ref, bim_ref, cr_ref, ci_ref,
                    m_ref, pre_ref, pim_ref, cre_ref, cim_ref, a16_ref,
                    brt_ref, bit_ref, arr_ref, air_ref):
    L, C, N = S5_CHUNK, SSM_GROUP, SSM_STATE
    lag = lax.broadcasted_iota(jnp.int32, (1, LANES), 1).astype(F32)
    sel_row = lax.broadcasted_iota(jnp.int32, (LANES, L * C), 0)
    sel_lag = lax.shift_right_logical(lax.broadcasted_iota(jnp.int32, (LANES, L * C), 1),
                                      C.bit_length() - 1)
    pick = lambda cond: jnp.where(cond, 1.0, 0.0).astype(BF16)
    r_same, r_next, r_rev = (pick(sel_row == sel_lag), pick(sel_row == sel_lag + 1),
                             pick(sel_row == (L - 1) - sel_lag))

    def select(e, r, dims=(((1,), (0,)), ((), ()))):
        d = lambda h: lax.dot_general(h, r, dims, preferred_element_type=F32)
        h1, h2, h3 = _split3(e)
        return d(h1) + d(h2) + d(h3)

    spread = select
    on_rows = (((0,), (0,)), ((), ()))
    r_tile = pick(lax.broadcasted_iota(jnp.int32, (C, L * C), 0)
                  == (lax.broadcasted_iota(jnp.int32, (C, L * C), 1) & (C - 1)))
    ident = pick(lax.broadcasted_iota(jnp.int32, (N, N), 0)
                 == lax.broadcasted_iota(jnp.int32, (N, N), 1))

    lane2 = lax.broadcasted_iota(jnp.int32, (C, L * C), 1)
    zero = jnp.zeros((N, L * C), F32)
    a16r, a16i = [], []
    for k in range(2):
        dt = jnp.exp(ldt_ref[k])
        lrc, lic = lrc_ref[k], lic_ref[k]
        mag = jnp.exp(lrc * dt * lag)
        er, ei = mag * jnp.cos(lic * dt * lag), mag * jnp.sin(lic * dt * lag)
        e0r, e0i = spread(er, r_same), spread(ei, r_same)
        e1r, e1i = spread(er, r_next), spread(ei, r_next)
        e2r, e2i = spread(er, r_rev), spread(ei, r_rev)
        _, _, frc, fic = _zoh(lrc, lic, dt)
        brl, bil = select(bre_ref[k], r_tile), select(bim_ref[k], r_tile)
        crl = select(cr_ref[k], r_tile, on_rows)
        cil = select(ci_ref[k], r_tile, on_rows)
        bbr = frc * brl - fic * bil
        bbi = frc * bil + fic * brl
        halves = lambda x: jnp.concatenate([x, zero] if k == 0 else [zero, x], axis=1)
        rows = slice(k * N, (k + 1) * N)
        pre_ref[0, rows, :] = halves(e2r * bbr - e2i * bbi).astype(BF16)
        pim_ref[0, rows, :] = halves(e2r * bbi + e2i * bbr).astype(BF16)
        cre_ref[0, rows, :] = halves(crl * e1r - cil * e1i).astype(BF16)
        cim_ref[0, rows, :] = halves(-(crl * e1i + cil * e1r)).astype(BF16)

        lrr, lir = lrr_ref[k], lir_ref[k]
        arr, air, frr, fir = _zoh(lrr, lir, dt)
        bret, bimt = select(bre_ref[k], ident, on_rows), select(bim_ref[k], ident, on_rows)
        brt = frr * bret - fir * bimt
        bit = frr * bimt + fir * bret
        brt_ref[k], bit_ref[k], arr_ref[k], air_ref[k] = brt, bit, arr, air
        m16 = jnp.exp(lrr * dt * L)
        a16r.append(m16 * jnp.cos(lir * dt * L))
        a16i.append(m16 * jnp.sin(lir * dt * L))
        klt = (_dot_split(brt, crl * e0r - cil * e0i)
               - _dot_split(bit, crl * e0i + cil * e0r))
        for j in range(L):
            blk = klt if j == 0 else jnp.where(lane2 >= j * C, pltpu.roll(klt, j * C, axis=1), 0.0)
            m_ref[k, j * C:(j + 1) * C, :] = blk.astype(BF16)
    a16_ref[0, 0:1, :] = jnp.concatenate(a16r, axis=1)
    a16_ref[0, 1:2, :] = jnp.concatenate(a16i, axis=1)


def _s5_prep(A_re, A_im, log_dt, B_re, B_im, C_re, C_im):
    L, G, N, C = S5_CHUNK, SSM_GROUPS, SSM_STATE, SSM_GROUP
    ins = [A_re[:, :, None], A_im[:, :, None], A_re[:, None, :], A_im[:, None, :],
           log_dt[:, None, None], B_re, B_im, C_re, C_im]
    two = lambda a: pl.BlockSpec((2,) + a.shape[1:], lambda p: (p, 0, 0))
    one = lambda r, c: pl.BlockSpec((1, r, c), lambda p: (p, 0, 0))
    outs = [((G, L * C, L * C), BF16), ((G // 2, 2 * N, 2 * L * C), BF16),
            ((G // 2, 2 * N, 2 * L * C), BF16), ((G // 2, 2 * N, 2 * L * C), BF16),
            ((G // 2, 2 * N, 2 * L * C), BF16), ((G // 2, 2, 2 * N), F32),
            ((G, C, N), F32), ((G, C, N), F32), ((G, 1, N), F32), ((G, 1, N), F32)]
    out_specs = [pl.BlockSpec((2, L * C, L * C), lambda p: (p, 0, 0))] \
        + [one(2 * N, 2 * L * C)] * 4 + [one(2, 2 * N)] \
        + [pl.BlockSpec((2,) + s[1:], lambda p: (p, 0, 0)) for s, _ in outs[6:]]
    res = pl.pallas_call(
        _s5_prep_kernel,
        grid=(G // 2,),
        in_specs=[two(a) for a in ins],
        out_specs=out_specs,
        out_shape=[jax.ShapeDtypeStruct(s, d) for s, d in outs],
        compiler_params=_params(("arbitrary",)),
        name="s5_prep",
    )(*ins)
    names = ('m', 'pre', 'pim', 'cre', 'cim', 'a16', 'brt', 'bit', 'arr', 'air')
    return dict(zip(names, res))


S5_GQ = LANES // SSM_GROUP


def _s5_prompt_kernel(u_ref, m_ref, pre_ref, pim_ref, cre_ref, cim_ref, a16_ref,
                      y_ref, xr_ref, xi_ref,
                      u2_ref, bre_ref, bim_ref, xpr_ref, xpi_ref, *, nb, nch):
    L, C, GQ = S5_CHUNK, SSM_GROUP, S5_GQ
    npair = GQ // 2

    def relayout_in(b, carry):
        uts = [u_ref[b, pl.ds(j, nch, stride=L), :].T for j in range(L)]
        for g in range(GQ):
            for hf in range(2):
                xt = jnp.concatenate([ut[g * C:(g + 1) * C, :]
                                      for ut in uts[hf * L // 2:(hf + 1) * L // 2]], axis=0)
                u2_ref[g, hf, pl.ds(b, nch, stride=nb), :] = xt.T
        return carry

    u2 = lambda g: jnp.concatenate([u2_ref[g, 0], u2_ref[g, 1]], axis=1)

    lax.fori_loop(0, nb, relayout_in, 0)

    for m in range(npair):
        ub2 = jnp.concatenate([u2(2 * m), u2(2 * m + 1)], axis=1).astype(BF16)
        bre_ref[m] = _dot_nt(ub2, pre_ref[m])
        bim_ref[m] = _dot_nt(ub2, pim_ref[m])

    ars = [a16_ref[m, 0:1, :] for m in range(npair)]
    ais = [a16_ref[m, 1:2, :] for m in range(npair)]

    def step(cidx, carry):
        o = pl.multiple_of(cidx * nb, nb)
        out = []
        for m in range(npair):
            xr, xi = carry[2 * m], carry[2 * m + 1]
            xpr_ref[m, pl.ds(o, nb), :] = xr
            xpi_ref[m, pl.ds(o, nb), :] = xi
            out.append(ars[m] * xr - ais[m] * xi + bre_ref[m, pl.ds(o, nb), :])
            out.append(ars[m] * xi + ais[m] * xr + bim_ref[m, pl.ds(o, nb), :])
        return tuple(out)

    z = jnp.zeros((nb, LANES), F32)
    fin = lax.fori_loop(0, nch, step, (z,) * GQ)
    for m in range(npair):
        xr_ref[m] = fin[2 * m]
        xi_ref[m] = fin[2 * m + 1]

    for m in range(npair):
        ys = _dot(xpr_ref[m], cre_ref[m]) + _dot(xpi_ref[m], cim_ref[m])
        for k in range(2):
            g = 2 * m + k
            yg = _dot(u2(g), m_ref[g]) + ys[:, k * L * C:(k + 1) * L * C]
            u2_ref[g, 0] = yg[:, :LANES]
            u2_ref[g, 1] = yg[:, LANES:]

    def relayout_out(b, carry):
        for hf in range(2):
            yts = [u2_ref[g, hf, pl.ds(b, nch, stride=nb), :].T for g in range(GQ)]
            for i in range(L // 2):
                yt = jnp.concatenate([t_[i * C:(i + 1) * C, :] for t_ in yts], axis=0)
                y_ref[b, pl.ds(hf * L // 2 + i, nch, stride=L), :] = yt.T
        return carry

    lax.fori_loop(0, nb, relayout_out, 0)


def _s5_prompt(u, tabs, nb, t):
    L, G, N, C = S5_CHUNK, SSM_GROUPS, SSM_STATE, SSM_GROUP
    nch = t // L
    nc = nch * nb
    nq = SSM_WIDTH // LANES
    npair = S5_GQ // 2
    spec = lambda blk: pl.BlockSpec(blk, lambda q: (q, 0, 0))
    y, xr, xi = pl.pallas_call(
        functools.partial(_s5_prompt_kernel, nb=nb, nch=nch),
        grid=(nq,),
        in_specs=[pl.BlockSpec((nb, t, LANES), lambda q: (0, 0, q), pipeline_mode=pl.Buffered(1)),
                  spec((S5_GQ, L * C, L * C)),
                  spec((npair, 2 * N, 2 * L * C)), spec((npair, 2 * N, 2 * L * C)),
                  spec((npair, 2 * N, 2 * L * C)), spec((npair, 2 * N, 2 * L * C)),
                  spec((npair, 2, 2 * N))],
        out_specs=[pl.BlockSpec((nb, t, LANES), lambda q: (0, 0, q)),
                   spec((npair, nb, 2 * N)), spec((npair, nb, 2 * N))],
        out_shape=[jax.ShapeDtypeStruct((nb, t, SSM_WIDTH), F32),
                   jax.ShapeDtypeStruct((G // 2, nb, 2 * N), F32),
                   jax.ShapeDtypeStruct((G // 2, nb, 2 * N), F32)],
        scratch_shapes=[pltpu.VMEM((S5_GQ, 2, nc, LANES), F32)]
        + [pltpu.VMEM((npair, nc, 2 * N), F32) for _ in range(4)],
        compiler_params=_params(("arbitrary",)),
        name="s5_prompt",
    )(u.reshape(nb, t, SSM_WIDTH), tabs['m'], tabs['pre'], tabs['pim'], tabs['cre'], tabs['cim'],
      tabs['a16'])
    unpair = lambda x: x.reshape(G // 2, nb, 2, N).transpose(1, 0, 2, 3).reshape(nb, G, N)
    return y.reshape(nb * t, SSM_WIDTH), unpair(xr), unpair(xi)


def _s5_step_kernel(u_ref, xr_ref, xi_ref, ar_ref, ai_ref, brt_ref, bit_ref, cr_ref, ci_ref,
                    y_ref, nr_ref, ni_ref, bd_ref):
    G, N, C = SSM_GROUPS, SSM_STATE, SSM_GROUP

    def block_diag(slot, src_ref):
        bd_ref[slot] = jnp.zeros((G * C, G * N), BF16)
        for g in range(G):
            bd_ref[slot, g * C:(g + 1) * C, g * N:(g + 1) * N] = src_ref[g].astype(BF16)
        return bd_ref[slot]

    ub = u_ref[...].astype(BF16)
    xr, xi, ar, ai = xr_ref[...], xi_ref[...], ar_ref[...], ai_ref[...]
    nr = ar * xr - ai * xi + jnp.dot(ub, block_diag(0, brt_ref), preferred_element_type=F32)
    ni = ar * xi + ai * xr + jnp.dot(ub, block_diag(1, bit_ref), preferred_element_type=F32)
    nr_ref[...] = nr
    ni_ref[...] = ni
    y_ref[...] = _dot_nt(nr, block_diag(2, cr_ref)) - _dot_nt(ni, block_diag(3, ci_ref))


def _s5_step(u, st_re, st_im, tabs, C_re, C_im):
    G, N, C = SSM_GROUPS, SSM_STATE, SSM_GROUP
    n = u.shape[0]
    ops = (u, st_re.reshape(n, G * N), st_im.reshape(n, G * N),
           tabs['arr'].reshape(1, G * N), tabs['air'].reshape(1, G * N),
           tabs['brt'], tabs['bit'], C_re, C_im)
    y, nr, ni = pl.pallas_call(
        _s5_step_kernel,
        grid=(1,),
        in_specs=[_full_spec(o.shape) for o in ops],
        out_specs=[_full_spec((n, SSM_WIDTH)), _full_spec((n, G * N)), _full_spec((n, G * N))],
        out_shape=[jax.ShapeDtypeStruct((n, SSM_WIDTH), F32),
                   jax.ShapeDtypeStruct((n, G * N), F32),
                   jax.ShapeDtypeStruct((n, G * N), F32)],
        scratch_shapes=[pltpu.VMEM((4, G * C, G * N), BF16)],
        compiler_params=_params(("arbitrary",)),
        name="s5_step",
    )(*ops)
    return y, nr.reshape(n, G, N), ni.reshape(n, G, N)


def _tail_kernel(*refs, tm, nsub, tiles_per_seq, decode, final):
    if decode:
        (x_ref, rw_ref, ys_ref, u_ref, gl_ref, pe_ref, st2_ref, st1_ref,
         dsk_ref, wglu_ref, wout_ref, ln2_ref, wfi_ref, cw_ref, cb_ref, wfo_ref,
         ln3_ref, wpg_ref, wple_ref, fin_ref, y_ref, conv_ref) = refs
    else:
        (x_ref, rw_ref, ys_ref, u_ref, gl_ref, pe_ref,
         dsk_ref, wglu_ref, wout_ref, ln2_ref, wfi_ref, cw_ref, cb_ref, wfo_ref,
         ln3_ref, wpg_ref, wple_ref, fin_ref, y_ref, conv_ref, carry_ref) = refs

    sub = tm // nsub
    cw = cw_ref[...]
    last2 = {}
    if not decode:
        @pl.when(pl.program_id(0) % tiles_per_seq == 0)
        def _():
            carry_ref[...] = jnp.zeros_like(carry_ref)

        for cols in FFN_CHUNKS:
            last2[-1, cols.start] = carry_ref[:, cols]

    def row_block(s):
        rs = slice(s * sub, (s + 1) * sub)
        x = x_ref[rs, :]
        z = _gelu(ys_ref[rs, :] + dsk_ref[...] * u_ref[rs, :])
        zz = _dot(z, wglu_ref[...])
        yield
        s5 = zz[:, :D_MODEL] * _sigmoid(zz[:, D_MODEL:])
        gl = gl_ref[rs, :]
        merged = _sigmoid(gl[:, :D_MODEL]) * rw_ref[rs, :] + _sigmoid(gl[:, D_MODEL:]) * s5
        x = x + _dot(merged, wout_ref[...])
        yield
        h2 = _rmsnorm(x, ln2_ref[...]).astype(BF16)
        rows = lax.broadcasted_iota(jnp.int32, (sub, 1), 0)

        def up(cols):
            a_c = jnp.dot(h2, wfi_ref[:, cols], preferred_element_type=F32)
            b_c = jnp.dot(h2, wfi_ref[:, slice(D_FF + cols.start, D_FF + cols.stop)],
                          preferred_element_type=F32)
            if decode:
                conv_ref[rs, cols] = a_c
            else:
                last2[s, cols.start] = a_c[sub - 2:sub, :]
            return a_c, b_c

        def down(a_c, b_c, cols):
            if decode:
                prev2, prev1 = st2_ref[rs, cols], st1_ref[rs, cols]
            else:
                before = last2[s - 1, cols.start]
                c0, c1 = before[0:1, :], before[1:2, :]
                prev1 = jnp.where(rows == 0, c1, pltpu.roll(a_c, 1, axis=0))
                prev2 = jnp.where(rows == 0, c0,
                                  jnp.where(rows == 1, c1, pltpu.roll(a_c, 2, axis=0)))
            a_conv = (cw[0:1, cols] * prev2 + cw[1:2, cols] * prev1 + cw[2:3, cols] * a_c
                      + cb_ref[:, cols])
            return _dot(_gelu(a_conv) * b_c, wfo_ref[cols, :])

        ups = {0: up(FFN_CHUNKS[0])}
        for c, cols in enumerate(FFN_CHUNKS):
            if c + 1 < len(FFN_CHUNKS):
                ups[c + 1] = up(FFN_CHUNKS[c + 1])
            yield
            x = x + down(*ups.pop(c), cols)
        yield
        pg = _sigmoid(_dot(_rmsnorm(x, ln3_ref[...]), wpg_ref[...]))
        x = x + pg * _dot(pe_ref[rs, :], wple_ref[...])
        y_ref[rs, :] = _rmsnorm(x, fin_ref[...]) if final else x
        yield

    blocks = [row_block(s) for s in range(nsub)]
    for _ in range(4 + len(FFN_CHUNKS)):
        for blk in blocks:
            next(blk)
    if not decode:
        for cols in FFN_CHUNKS:
            carry_ref[:, cols] = last2[nsub - 1, cols.start]
            conv_ref[0, :, cols] = last2[nsub - 1, cols.start]


def _tail(x2d, rw, ys, u, gl, pe, st_conv, D_skip, wglu, wout, ln2_g, wfi, conv_w, conv_b, wfo,
          ln3_g, wpg, wple, final_g, *, tm, tiles_per_seq, decode, final):
    n = x2d.shape[0]
    row = lambda w: pl.BlockSpec((tm, w), lambda i: (i, 0))
    consts = [D_skip.reshape(1, SSM_WIDTH), wglu, wout, ln2_g.reshape(1, D_MODEL), wfi,
              conv_w, conv_b.reshape(1, D_FF), wfo, ln3_g.reshape(1, D_MODEL), wpg, wple,
              final_g.reshape(1, D_MODEL)]
    acts = [x2d, rw, ys, u, gl, pe]
    in_specs = [row(a.shape[1]) for a in acts]
    scratch = []
    if decode:
        acts += [st_conv[:, 0], st_conv[:, 1]]
        in_specs += [row(D_FF), row(D_FF)]
        conv_spec = row(D_FF)
        conv_shape = (n, D_FF)
    else:
        nseq = n // (tm * tiles_per_seq)
        conv_spec = pl.BlockSpec((1, 2, D_FF), lambda i: (i // tiles_per_seq, 0, 0))
        conv_shape = (nseq, 2, D_FF)
        scratch = [pltpu.VMEM((2, D_FF), F32)]
    in_specs += [pl.BlockSpec(c.shape, lambda i, nd=c.ndim: (0,) * nd,
                              pipeline_mode=pl.Buffered(1)) for c in consts]
    y, conv = pl.pallas_call(
        functools.partial(_tail_kernel, tm=tm, nsub=1 if decode else TAIL_ROW_BLOCKS,
                          tiles_per_seq=tiles_per_seq, decode=decode,
                          final=final),
        grid=(n // tm,),
        in_specs=in_specs,
        out_specs=[row(D_MODEL), conv_spec],
        out_shape=[jax.ShapeDtypeStruct((n, D_MODEL), F32),
                   jax.ShapeDtypeStruct(conv_shape, F32)],
        scratch_shapes=scratch,
        compiler_params=_params(("arbitrary",)),
        name="tail_decode" if decode else "tail_prompt",
    )(*acts, *consts)
    if decode:
        conv = jnp.stack([st_conv[:, 1], conv], axis=1)
    return y, conv


def _layer(x, pe, states, W, *, decode, final, tm_proj, tt, tm_tail):
    nb, t, _ = x.shape
    n = nb * t
    x2d = x.reshape(n, D_MODEL)
    to_cast = [k for k in TAIL_WEIGHTS if W[k].dtype != BF16]
    (prw, u, gl), casted = _proj(x2d, W['ln1_g'], W['w_in'], tm_proj, [W[k] for k in to_cast])
    W.update(zip(to_cast, casted))
    if decode:
        st_shift, st_wkv, st_re, st_im, st_conv = states
        rw, s_new = _rwkv_step(prw, st_shift, st_wkv, W['rowp'], W['wla'], W['g2'], W['eones'],
                               W['w_rw_out'])
        ys, xr, xi = _s5_step(u, st_re, st_im, W['s5'], W['C_re'], W['C_im'])
        new_shift = prw
    else:
        st_conv = None
        rw, s_new = _rwkv_prompt(prw, W['rowp'], W['wla'], W['g2'], W['eones'], W['w_rw_out'],
                                 nb, t, tt)
        ys, xr, xi = _s5_prompt(u, W['s5'], nb, t)
        new_shift = prw.reshape(nb, t, SHIFT_WIDTH)[:, -1]
    y, new_conv = _tail(x2d, rw, ys, u, gl, pe.reshape(n, PLE_DIM), st_conv, W['D_skip'],
                        W['w_glu'], W['w_out'], W['ln2_g'], W['w_ffn_in'], W['conv_w'],
                        W['conv_b'], W['w_ffn_out'], W['ln3_g'], W['w_ple_gate'], W['w_ple'],
                        W['final_g'], tm=tm_tail, tiles_per_seq=max(t // tm_tail, 1),
                        decode=decode, final=final)
    return y.reshape(nb, t, D_MODEL), (new_shift, s_new, xr, xi, new_conv)


def kernel(x_prompt, x_sample, p_prompt, p_sample, state_shift, state_wkv, state_ssm_re, state_ssm_im, state_conv, ln1_g, w_in, mu_shift, w0, w2, a0, a2, g2, k_k, k_a, r_k, lnx_g, lnx_b, w_rw_out, A_re, A_im, log_dt, B_re, B_im, C_re, C_im, D_skip, w_glu, w_out, ln2_g, w_ffn_in, conv_w, conv_b, w_ffn_out, ln3_g, w_ple_gate, w_ple, final_g):
    depth = w_in.shape[0]
    xp, xs = x_prompt, x_sample
    pst = [[] for _ in range(5)]
    sst = [[] for _ in range(5)]
    for i in range(depth):
        wla, g2b, wo = _rwkv_weights(w2[i], a2[i], g2[i], w_rw_out[i])
        W = dict(
            ln1_g=ln1_g[i], w_in=w_in[i],
            rowp=_row_params(mu_shift[i], w0[i], a0[i], k_k[i], k_a[i], r_k[i].reshape(-1),
                             lnx_g[i], lnx_b[i]),
            wla=wla, g2=g2b, eones=_rwkv_consts(), w_rw_out=wo,
            s5=_s5_prep(A_re[i], A_im[i], log_dt[i], B_re[i], B_im[i], C_re[i], C_im[i]),
            C_re=C_re[i], C_im=C_im[i], D_skip=D_skip[i],
            w_glu=w_glu[i], w_out=w_out[i], ln2_g=ln2_g[i],
            w_ffn_in=w_ffn_in[i], conv_w=conv_w[i], conv_b=conv_b[i],
            w_ffn_out=w_ffn_out[i], ln3_g=ln3_g[i],
            w_ple_gate=w_ple_gate[i], w_ple=w_ple[i], final_g=final_g)
        final = i == depth - 1
        tp = xp.shape[1]
        xp, sp = _layer(xp, p_prompt[i], None, W, decode=False, final=final,
                        tm_proj=min(TM_PROJ, tp), tt=min(TT_RWKV, tp), tm_tail=min(TM_TAIL, tp))
        ns = xs.shape[0]
        xs, ss = _layer(xs, p_sample[i],
                        (state_shift[i], state_wkv[i], state_ssm_re[i], state_ssm_im[i],
                         state_conv[i]),
                        W, decode=True, final=final, tm_proj=ns, tt=None, tm_tail=ns)
        for j in range(5):
            pst[j].append(sp[j])
            sst[j].append(ss[j])
    return (xp, xs,
            jnp.stack(pst[0]), jnp.stack(pst[1]), jnp.stack(pst[2]), jnp.stack(pst[3]),
            jnp.stack(pst[4]),
            jnp.stack(sst[0]), jnp.stack(sst[1]), jnp.stack(sst[2]), jnp.stack(sst[3]),
            jnp.stack(sst[4]))
```

```python
import functools
import math

import jax
import jax.numpy as jnp
import numpy as np
from jax import lax
from jax.experimental import pallas as pl
from jax.experimental.pallas import tpu as pltpu

F32 = jnp.float32
BF16 = jnp.bfloat16

D_MODEL = 1024
RW_HEADS = 8
HEAD_DIM = 64
RW_WIDTH = RW_HEADS * HEAD_DIM
LORA_W = 64
LORA_A = 64
LORA_G = 128
SHIFT_WIDTH = 3 * RW_WIDTH + LORA_W + LORA_A + LORA_G
SSM_WIDTH = 512
SSM_GROUP = 16
SSM_GROUPS = SSM_WIDTH // SSM_GROUP
SSM_STATE = 64
GATE_OFF = SHIFT_WIDTH + SSM_WIDTH
IN_WIDTH = SHIFT_WIDTH + SSM_WIDTH + 2 * D_MODEL
D_FF = 2816
PLE_DIM = 256
EPS = 1e-6
GN_EPS = 64e-5

LANES = 128
WKV_CHUNK = 64
S5_CHUNK = 16
N_PAIRS = RW_HEADS // 2
VMEM_LIMIT = 60 * 1024 * 1024
TM_PROJ = 512
TT_RWKV = 256
TM_TAIL = 512
RWKV_SEQ_GROUPS = 2
TAIL_ROW_BLOCKS = 2


def _dot(a, b):
    return jnp.dot(a.astype(BF16), b.astype(BF16), preferred_element_type=F32)


def _dot_nt(a, b):
    return lax.dot_general(a.astype(BF16), b.astype(BF16), (((1,), (1,)), ((), ())),
                           preferred_element_type=F32)


def _dot_tn(a, b):
    return lax.dot_general(a.astype(BF16), b.astype(BF16), (((0,), (0,)), ((), ())),
                           preferred_element_type=F32)


def _split3(x):
    h1 = x.astype(BF16)
    r1 = x - h1.astype(F32)
    h2 = r1.astype(BF16)
    h3 = (r1 - h2.astype(F32)).astype(BF16)
    return h1, h2, h3


def _dot_exact_lhs(a_bf16, x):
    h1, h2, h3 = _split3(x)
    d = lambda h: jnp.dot(a_bf16, h, preferred_element_type=F32)
    return d(h1) + d(h2) + d(h3)


def _sigmoid(x):
    return 1.0 / (1.0 + jnp.exp(-x))


def _gelu(x):
    c = math.sqrt(2.0 / math.pi)
    return 0.5 * x * (1.0 + jnp.tanh(c * (x + 0.044715 * (x * x * x))))


def _rmsnorm(x, g):
    return x * lax.rsqrt(jnp.mean(x * x, axis=-1, keepdims=True) + EPS) * g


def _full_spec(shape):
    nd = len(shape)
    return pl.BlockSpec(shape, lambda *_: (0,) * nd)


def _params(sem):
    return pltpu.CompilerParams(dimension_semantics=sem, vmem_limit_bytes=VMEM_LIMIT)


BF16_ROWS = 16
FFN_CHUNKS = (slice(0, 768), slice(768, 1536), slice(1536, 2304), slice(2304, D_FF))
TAIL_WEIGHTS = ('w_glu', 'w_out', 'w_ffn_in', 'w_ffn_out', 'w_ple_gate', 'w_ple')


def _proj_kernel(x_ref, g_ref, w_ref, *refs, n_cast):
    cast_in, (prw_ref, u_ref, gl_ref), cast_out = refs[:n_cast], refs[n_cast:n_cast + 3], \
        refs[n_cast + 3:]
    h = _rmsnorm(x_ref[...], g_ref[...])
    prw_ref[...] = _dot(h, w_ref[:, :SHIFT_WIDTH])
    u_ref[...] = _dot(h, w_ref[:, SHIFT_WIDTH:GATE_OFF])
    gl_ref[...] = _dot(h, w_ref[:, GATE_OFF:])
    for src, dst in zip(cast_in, cast_out):
        dst[...] = src[...].astype(BF16)


def _proj(x2d, ln1_g, w_in, tm, cast_ws=()):
    n = x2d.shape[0]
    steps = n // tm
    row = lambda w: pl.BlockSpec((tm, w), lambda i: (i, 0))

    def slab(w):
        rows = w.shape[0]
        span = 1
        while rows % (steps // span) or (rows // (steps // span)) % BF16_ROWS:
            span *= 2
        return pl.BlockSpec((rows // (steps // span), w.shape[1]), lambda i: (i // span, 0))

    slabs = [slab(w) for w in cast_ws]
    res = pl.pallas_call(
        functools.partial(_proj_kernel, n_cast=len(cast_ws)),
        grid=(steps,),
        in_specs=[row(D_MODEL), _full_spec((1, D_MODEL)),
                  pl.BlockSpec((D_MODEL, IN_WIDTH), lambda i: (0, 0),
                               pipeline_mode=pl.Buffered(1))] + slabs,
        out_specs=[row(SHIFT_WIDTH), row(SSM_WIDTH), row(2 * D_MODEL)] + slabs,
        out_shape=[jax.ShapeDtypeStruct((n, SHIFT_WIDTH), F32),
                   jax.ShapeDtypeStruct((n, SSM_WIDTH), F32),
                   jax.ShapeDtypeStruct((n, 2 * D_MODEL), F32)]
        + [jax.ShapeDtypeStruct(w.shape, BF16) for w in cast_ws],
        compiler_params=_params(("arbitrary",)),
        name="proj",
    )(x2d, ln1_g.reshape(1, D_MODEL), w_in, *cast_ws)
    return res[:3], res[3:]


def _rwkv_token_math_stages(p, prev, mu, w0, a0, k_k, k_a, r_k, wla, g2, eones):
    xs = p + (prev - p) * mu
    r = xs[:, 0:RW_WIDTH]
    k = xs[:, RW_WIDTH:2 * RW_WIDTH]
    v = xs[:, 2 * RW_WIDTH:3 * RW_WIDTH]
    xwa = xs[:, 3 * RW_WIDTH:3 * RW_WIDTH + LORA_W + LORA_A]
    xg = xs[:, 3 * RW_WIDTH + LORA_W + LORA_A:]
    lane = lax.broadcasted_iota(jnp.int32, xwa.shape, 1)
    la = _dot(jnp.where(lane < LORA_W, jnp.tanh(xwa), xwa), wla)
    g = _dot(_sigmoid(xg), g2)
    yield None
    wpre = w0 + la[:, :RW_WIDTH]
    softplus = jnp.maximum(-wpre, 0.0) + jnp.log(1.0 + jnp.exp(-jnp.abs(wpre)))
    lw = -jnp.exp(-softplus - 0.5)
    a = _sigmoid(a0 + la[:, RW_WIDTH:])
    kkr = k * k_k
    ss = _dot(kkr * kkr, eones)
    kk = kkr / jnp.maximum(jnp.sqrt(ss), 1e-12)
    kp = k * (1.0 + (a - 1.0) * k_a)
    bonus = _dot(r * kp * r_k, eones) * v
    yield r, v, lw, a, g, kk, kp, bonus


def _rwkv_token_math(*args):
    *_, out = _rwkv_token_math_stages(*args)
    return out


def _rwkv_output(y, bonus, g, lnx_g, lnx_b, eones, w_o):
    emean = eones * (1.0 / HEAD_DIM)
    mean = _dot(y, emean)
    d = y - mean
    var = _dot(d * d, emean)
    yn = d * lax.rsqrt(var + GN_EPS) * lnx_g + lnx_b
    return _dot((yn + bonus) * g, w_o)


def _head_masks(shape):
    lane = lax.broadcasted_iota(jnp.int32, shape, 1)
    return lane < HEAD_DIM, lane >= HEAD_DIM


def _stack_heads(x, m_a, m_b):
    return jnp.concatenate([jnp.where(m_a, x, 0.0), jnp.where(m_b, x, 0.0)], axis=0)


def _rwkv_prompt_kernel(p_ref, mu_ref, w0_ref, a0_ref, kk_ref, ka_ref, rk_ref, lg_ref, lb_ref,
                        wla_ref, g2_ref, e_ref, ltri_ref, wo_ref,
                        out_ref, sout_ref,
                        carry_ref, sp_ref, y_ref, pend_ref, pendf_ref, gend_ref,
                        new_ref, newf_ref, gnew_ref, *, tt, nt, nh):
    s = pl.program_id(0)
    nch = tt // WKV_CHUNK
    eones = e_ref[...]
    halves = range(nh)
    pairs = range(N_PAIRS)
    rows_of = lambda ch: slice(ch * WKV_CHUNK, (ch + 1) * WKV_CHUNK)
    lanes_of = lambda j: slice(j * LANES, (j + 1) * LANES)

    @pl.when(s == 0)
    def _():
        pend_ref[...] = jnp.zeros_like(pend_ref)
        pendf_ref[...] = jnp.zeros_like(pendf_ref)
        gend_ref[...] = jnp.zeros_like(gend_ref)
        sp_ref[...] = jnp.zeros_like(sp_ref)

    @pl.when(s % nt == 0)
    def _():
        carry_ref[...] = jnp.zeros_like(carry_ref)

    @pl.when(s % nt == 1 % nt)
    def _():
        sp_ref[...] = jnp.zeros_like(sp_ref)

    rows = lax.broadcasted_iota(jnp.int32, (tt, 1), 0)
    token_math = []
    for h in halves:
        p = p_ref[h]
        prev = jnp.where(rows == 0, carry_ref[h], pltpu.roll(p, 1, axis=0))
        carry_ref[h] = p[tt - 1:tt, :]
        token_math.append(_rwkv_token_math_stages(
            p, prev, mu_ref[...], w0_ref[...], a0_ref[...], kk_ref[...], ka_ref[...],
            rk_ref[...], wla_ref[...], g2_ref[...], eones))
        next(token_math[h])

    m_a, m_b = _head_masks((WKV_CHUNK, LANES))
    ri = lax.broadcasted_iota(jnp.int32, (2 * WKV_CHUNK, 2 * WKV_CHUNK), 0)
    ci = lax.broadcasted_iota(jnp.int32, (2 * WKV_CHUNK, 2 * WKV_CHUNK), 1)
    same_head = (ri >= WKV_CHUNK) == (ci >= WKV_CHUNK)
    strict = same_head & (ri > ci)
    incl = same_head & (ri >= ci)
    blocks = [(h, ch, j) for h in halves for ch in range(nch) for j in pairs]
    blk = lambda i, h, ch, j: pend_ref[h, i, rows_of(ch), lanes_of(j)]
    stack = lambda i, h, ch, j: _stack_heads(blk(i, h, ch, j), m_a, m_b)
    vss = {b: stack(4, *b) for b in blocks}
    gms = {b: _dot_nt(jnp.concatenate([stack(0, *b), stack(1, *b)], axis=0),
                      jnp.concatenate([stack(2, *b), stack(3, *b)], axis=0)) for b in blocks}
    n_abs = {b: jnp.where(strict, gms[b][:LANES, :LANES], 0.0) for b in blocks}
    wvs = {b: _dot(jnp.where(strict, gms[b][:LANES, LANES:], 0.0), vss[b]) for b in blocks}
    a_ys = {b: jnp.concatenate([jnp.where(incl, gms[b][LANES:, :LANES], 0.0),
                                jnp.where(incl, gms[b][LANES:, LANES:], 0.0)],
                               axis=1).astype(BF16) for b in blocks}

    tm = [next(token_math[h]) for h in halves]
    cs = [_dot_exact_lhs(ltri_ref[...], tm[h][2]) for h in halves]

    eye = (ri == ci).astype(F32)
    tinvs = {b: eye + n_abs[b] for b in blocks}
    pows = n_abs
    for _ in range(5):
        pows = {b: _dot(pows[b], pows[b]).astype(BF16) for b in blocks}
        tinvs = {b: tinvs[b] + _dot(pows[b], tinvs[b]) for b in blocks}

    for h in halves:
        r, v_n, lw, a, g_n, kk, kp, bonus_n = tm[h]
        c = cs[h]
        einv = jnp.exp(-c)
        bvec = kk * a
        new_ref[h, 0] = (-kk * jnp.exp(c - lw)).astype(BF16)
        new_ref[h, 1] = (r * jnp.exp(c)).astype(BF16)
        new_ref[h, 2] = (bvec * einv).astype(BF16)
        new_ref[h, 3] = (kp * einv).astype(BF16)
        new_ref[h, 4] = v_n.astype(BF16)
        for ch in range(nch):
            c_end = c[(ch + 1) * WKV_CHUNK - 1:(ch + 1) * WKV_CHUNK, :]
            e_rest = jnp.exp(c_end - c[rows_of(ch), :])
            gnew_ref[h, ch] = jnp.exp(c_end)
            new_ref[h, 5, rows_of(ch), :] = (bvec[rows_of(ch), :] * e_rest).astype(BF16)
            new_ref[h, 6, rows_of(ch), :] = (kp[rows_of(ch), :] * e_rest).astype(BF16)
        newf_ref[h, 0] = bonus_n
        newf_ref[h, 1] = g_n

    sps = {(h, j): sp_ref[h, j] for h in halves for j in pairs}
    for ch in range(nch):
        hj = [(h, j) for h in halves for j in pairs]
        arss = {(h, j): _dot_nt(jnp.concatenate([blk(0, h, ch, j), blk(1, h, ch, j)], axis=0),
                                sps[h, j]) for h, j in hj}
        uvs = {(h, j): jnp.concatenate(
            [_dot(tinvs[h, ch, j],
                  _stack_heads(arss[h, j][:WKV_CHUNK], m_a, m_b) + wvs[h, ch, j]).astype(BF16),
             vss[h, ch, j]], axis=0) for h, j in hj}
        yss = {(h, j): _dot(a_ys[h, ch, j], uvs[h, j]) for h, j in hj}
        incs = {(h, j): _dot_tn(uvs[h, j], jnp.concatenate(
            [stack(5, h, ch, j), stack(6, h, ch, j)], axis=0)) for h, j in hj}
        for h, j in hj:
            y_ref[h, rows_of(ch), lanes_of(j)] = (yss[h, j][:WKV_CHUNK] + yss[h, j][WKV_CHUNK:]
                                                  + arss[h, j][WKV_CHUNK:])
            sps[h, j] = sps[h, j] * gend_ref[h, ch][:, lanes_of(j)] + incs[h, j]
    for h in halves:
        for j in pairs:
            sp_ref[h, j] = sps[h, j]
        out_ref[h] = _rwkv_output(y_ref[h], pendf_ref[h, 0], pendf_ref[h, 1], lg_ref[...],
                                  lb_ref[...], eones, wo_ref[...])

    pend_ref[...] = new_ref[...]
    pendf_ref[...] = newf_ref[...]
    gend_ref[...] = gnew_ref[...]

    @pl.when((s % nt == 0) & (s > 0))
    def _():
        for h in halves:
            for j in pairs:
                sp = sp_ref[h, j]
                sout_ref[h, 0, 2 * j] = sp[:HEAD_DIM, :HEAD_DIM]
                sout_ref[h, 0, 2 * j + 1] = sp[HEAD_DIM:, HEAD_DIM:]


def _rwkv_consts():
    idx = np.arange(RW_WIDTH)
    return jnp.asarray(idx[:, None] // HEAD_DIM == idx[None, :] // HEAD_DIM, dtype=BF16)


def _rwkv_weights(w2, a2, g2, w_rw_out):
    wla = jnp.zeros((LORA_W + LORA_A, 2 * RW_WIDTH), F32)
    wla = wla.at[:LORA_W, :RW_WIDTH].set(w2).at[LORA_W:, RW_WIDTH:].set(a2)
    return wla.astype(BF16), g2.astype(BF16), w_rw_out.astype(BF16)


def _row_params(mu_shift, w0, a0, k_k, k_a, r_k, lnx_g, lnx_b):
    return (mu_shift.reshape(1, SHIFT_WIDTH), w0.reshape(1, RW_WIDTH), a0.reshape(1, RW_WIDTH),
            k_k.reshape(1, RW_WIDTH), k_a.reshape(1, RW_WIDTH), r_k.reshape(1, RW_WIDTH),
            lnx_g.reshape(1, RW_WIDTH), lnx_b.reshape(1, RW_WIDTH))


def _rwkv_prompt(prw, rowp, wla, g2b, eones, wo, nb, t, tt):
    nt = t // tt
    idx = np.arange(tt)
    ltri = jnp.asarray((idx[:, None] // WKV_CHUNK == idx[None, :] // WKV_CHUNK)
                       & (idx[:, None] >= idx[None, :]), dtype=BF16)
    consts = list(rowp) + [wla, g2b, eones, ltri, wo]
    nh = RWKV_SEQ_GROUPS if nb % RWKV_SEQ_GROUPS == 0 else 1
    ntiles = (nb // nh) * nt
    nch = tt // WKV_CHUNK
    done = lambda s: jnp.maximum(s - 1, 0)
    out, s_new = pl.pallas_call(
        functools.partial(_rwkv_prompt_kernel, tt=tt, nt=nt, nh=nh),
        grid=(ntiles + 1,),
        in_specs=[pl.BlockSpec((nh, tt, SHIFT_WIDTH),
                               lambda s: (0, jnp.minimum(s, ntiles - 1), 0))]
        + [_full_spec(c.shape) for c in consts],
        out_specs=[pl.BlockSpec((nh, tt, D_MODEL), lambda s: (0, done(s), 0)),
                   pl.BlockSpec((nh, 1, RW_HEADS, HEAD_DIM, HEAD_DIM),
                                lambda s: (0, done(s) // nt, 0, 0, 0))],
        out_shape=[jax.ShapeDtypeStruct((nh, nb * t // nh, D_MODEL), F32),
                   jax.ShapeDtypeStruct((nh, nb // nh, RW_HEADS, HEAD_DIM, HEAD_DIM), F32)],
        scratch_shapes=[pltpu.VMEM((nh, 1, SHIFT_WIDTH), F32),
                        pltpu.VMEM((nh, N_PAIRS, LANES, LANES), F32),
                        pltpu.VMEM((nh, tt, RW_WIDTH), F32),
                        pltpu.VMEM((nh, 7, tt, RW_WIDTH), BF16),
                        pltpu.VMEM((nh, 2, tt, RW_WIDTH), F32),
                        pltpu.VMEM((nh, nch, 1, RW_WIDTH), F32),
                        pltpu.VMEM((nh, 7, tt, RW_WIDTH), BF16),
                        pltpu.VMEM((nh, 2, tt, RW_WIDTH), F32),
                        pltpu.VMEM((nh, nch, 1, RW_WIDTH), F32)],
        compiler_params=_params(("arbitrary",)),
        name="rwkv_prompt",
    )(prw.reshape(nh, nb * t // nh, SHIFT_WIDTH), *consts)
    return (out.reshape(nb * t, D_MODEL),
            s_new.reshape(nb, RW_HEADS, HEAD_DIM, HEAD_DIM))


def _rwkv_step_kernel(p_ref, prev_ref, s_ref, mu_ref, w0_ref, a0_ref, kk_ref, ka_ref, rk_ref,
                      lg_ref, lb_ref, wla_ref, g2_ref, e_ref, wo_ref,
                      out_ref, sout_ref,
                      qt_ref, wt_ref, rt_ref, bt_ref, kt_ref, vt_ref, yt_ref, g_ref, bonus_ref):
    hp = pl.program_id(0)

    @pl.when(hp == 0)
    def _():
        r, v, lw, a, g, kk, kp, bonus = _rwkv_token_math(
            p_ref[...], prev_ref[...], mu_ref[...], w0_ref[...], a0_ref[...], kk_ref[...],
            ka_ref[...], rk_ref[...], wla_ref[...], g2_ref[...], e_ref[...])
        qt_ref[...] = (-kk).T
        wt_ref[...] = jnp.exp(lw).T
        rt_ref[...] = r.T
        bt_ref[...] = (kk * a).T
        kt_ref[...] = kp.T
        vt_ref[...] = v.T
        g_ref[...] = g
        bonus_ref[...] = bonus

    def per_value_row(vi, carry):
        for hl in range(2):
            row0 = pl.multiple_of((2 * hp + hl) * HEAD_DIM, HEAD_DIM)
            ks = pl.ds(row0, HEAD_DIM)
            s = s_ref[hl, vi]
            sa = jnp.sum(s * qt_ref[ks, :], axis=0, keepdims=True)
            s_new = (s * wt_ref[ks, :] + sa * bt_ref[ks, :]
                     + vt_ref[pl.ds(row0 + vi, 1), :] * kt_ref[ks, :])
            yt_ref[pl.ds(row0 + vi, 1), :] = jnp.sum(s_new * rt_ref[ks, :], axis=0, keepdims=True)
            sout_ref[hl, vi] = s_new
        return carry

    lax.fori_loop(0, HEAD_DIM, per_value_row, 0, unroll=2)

    @pl.when(hp == pl.num_programs(0) - 1)
    def _():
        out_ref[...] = _rwkv_output(yt_ref[...].T, bonus_ref[...], g_ref[...], lg_ref[...],
                                    lb_ref[...], e_ref[...], wo_ref[...])


def _rwkv_step(prw, prev, state, rowp, wla, g2b, eones, wo):
    n = prw.shape[0]
    consts = list(rowp) + [wla, g2b, eones, wo]
    state_t = jnp.transpose(state, (1, 2, 3, 0))
    sspec = pl.BlockSpec((2, HEAD_DIM, HEAD_DIM, n), lambda hp: (hp, 0, 0, 0))
    out, s_new_t = pl.pallas_call(
        _rwkv_step_kernel,
        grid=(N_PAIRS,),
        in_specs=[_full_spec((n, SHIFT_WIDTH)), _full_spec((n, SHIFT_WIDTH)), sspec]
        + [_full_spec(c.shape) for c in consts],
        out_specs=[_full_spec((n, D_MODEL)), sspec],
        out_shape=[jax.ShapeDtypeStruct((n, D_MODEL), F32),
                   jax.ShapeDtypeStruct((RW_HEADS, HEAD_DIM, HEAD_DIM, n), F32)],
        scratch_shapes=[pltpu.VMEM((RW_WIDTH, n), F32) for _ in range(7)]
        + [pltpu.VMEM((n, RW_WIDTH), F32) for _ in range(2)],
        compiler_params=_params(("arbitrary",)),
        name="rwkv_step",
    )(prw, prev, state_t, *consts)
    return out, jnp.transpose(s_new_t, (3, 0, 1, 2))


def _zoh(lr, li, dt):
    mag = jnp.exp(lr * dt)
    ar, ai = mag * jnp.cos(li * dt), mag * jnp.sin(li * dt)
    den = lr * lr + li * li
    fr = ((ar - 1.0) * lr + ai * li) / den
    fi = (ai * lr - (ar - 1.0) * li) / den
    return ar, ai, fr, fi


def _dot_split(a, b):
    a1 = a.astype(BF16)
    a2 = (a - a1.astype(F32)).astype(BF16)
    b1 = b.astype(BF16)
    b2 = (b - b1.astype(F32)).astype(BF16)
    d = lambda x, y: jnp.dot(x, y, preferred_element_type=F32)
    return d(a1, b1) + (d(a1, b2) + d(a2, b1))


def _s5_prep_kernel(lrc_ref, lic_ref, lrr_ref, lir_ref, ldt_ref, bre_ref, bim_ref, cr_ref, ci_ref,
                    m_ref, pre_ref, pim_ref, cre_ref, cim_ref, a16_ref,
                    brt_ref, bit_ref, arr_ref, air_ref):
    L, C, N = S5_CHUNK, SSM_GROUP, SSM_STATE
    lag = lax.broadcasted_iota(jnp.int32, (1, LANES), 1).astype(F32)
    sel_row = lax.broadcasted_iota(jnp.int32, (LANES, L * C), 0)
    sel_lag = lax.shift_right_logical(lax.broadcasted_iota(jnp.int32, (LANES, L * C), 1),
                                      C.bit_length() - 1)
    pick = lambda cond: jnp.where(cond, 1.0, 0.0).astype(BF16)
    r_same, r_next, r_rev = (pick(sel_row == sel_lag), pick(sel_row == sel_lag + 1),
                             pick(sel_row == (L - 1) - sel_lag))

    def select(e, r, dims=(((1,), (0,)), ((), ()))):
        d = lambda h: lax.dot_general(h, r, dims, preferred_element_type=F32)
        h1, h2, h3 = _split3(e)
        return d(h1) + d(h2) + d(h3)

    spread = select
    on_rows = (((0,), (0,)), ((), ()))
    r_tile = pick(lax.broadcasted_iota(jnp.int32, (C, L * C), 0)
                  == (lax.broadcasted_iota(jnp.int32, (C, L * C), 1) & (C - 1)))
    ident = pick(lax.broadcasted_iota(jnp.int32, (N, N), 0)
                 == lax.broadcasted_iota(jnp.int32, (N, N), 1))

    lane2 = lax.broadcasted_iota(jnp.int32, (C, L * C), 1)
    zero = jnp.zeros((N, L * C), F32)
    a16r, a16i = [], []
    for k in range(2):
        dt = jnp.exp(ldt_ref[k])
        lrc, lic = lrc_ref[k], lic_ref[k]
        mag = jnp.exp(lrc * dt * lag)
        er, ei = mag * jnp.cos(lic * dt * lag), mag * jnp.sin(lic * dt * lag)
        e0r, e0i = spread(er, r_same), spread(ei, r_same)
        e1r, e1i = spread(er, r_next), spread(ei, r_next)
        e2r, e2i = spread(er, r_rev), spread(ei, r_rev)
        _, _, frc, fic = _zoh(lrc, lic, dt)
        brl, bil = select(bre_ref[k], r_tile), select(bim_ref[k], r_tile)
        crl = select(cr_ref[k], r_tile, on_rows)
        cil = select(ci_ref[k], r_tile, on_rows)
        bbr = frc * brl - fic * bil
        bbi = frc * bil + fic * brl
        halves = lambda x: jnp.concatenate([x, zero] if k == 0 else [zero, x], axis=1)
        rows = slice(k * N, (k + 1) * N)
        pre_ref[0, rows, :] = halves(e2r * bbr - e2i * bbi).astype(BF16)
        pim_ref[0, rows, :] = halves(e2r * bbi + e2i * bbr).astype(BF16)
        cre_ref[0, rows, :] = halves(crl * e1r - cil * e1i).astype(BF16)
        cim_ref[0, rows, :] = halves(-(crl * e1i + cil * e1r)).astype(BF16)

        lrr, lir = lrr_ref[k], lir_ref[k]
        arr, air, frr, fir = _zoh(lrr, lir, dt)
        bret, bimt = select(bre_ref[k], ident, on_rows), select(bim_ref[k], ident, on_rows)
        brt = frr * bret - fir * bimt
        bit = frr * bimt + fir * bret
        brt_ref[k], bit_ref[k], arr_ref[k], air_ref[k] = brt, bit, arr, air
        m16 = jnp.exp(lrr * dt * L)
        a16r.append(m16 * jnp.cos(lir * dt * L))
        a16i.append(m16 * jnp.sin(lir * dt * L))
        klt = (_dot_split(brt, crl * e0r - cil * e0i)
               - _dot_split(bit, crl * e0i + cil * e0r))
        for j in range(L):
            blk = klt if j == 0 else jnp.where(lane2 >= j * C, pltpu.roll(klt, j * C, axis=1), 0.0)
            m_ref[k, j * C:(j + 1) * C, :] = blk.astype(BF16)
    a16_ref[0, 0:1, :] = jnp.concatenate(a16r, axis=1)
    a16_ref[0, 1:2, :] = jnp.concatenate(a16i, axis=1)


def _s5_prep(A_re, A_im, log_dt, B_re, B_im, C_re, C_im):
    L, G, N, C = S5_CHUNK, SSM_GROUPS, SSM_STATE, SSM_GROUP
    ins = [A_re[:, :, None], A_im[:, :, None], A_re[:, None, :], A_im[:, None, :],
           log_dt[:, None, None], B_re, B_im, C_re, C_im]
    two = lambda a: pl.BlockSpec((2,) + a.shape[1:], lambda p: (p, 0, 0))
    one = lambda r, c: pl.BlockSpec((1, r, c), lambda p: (p, 0, 0))
    outs = [((G, L * C, L * C), BF16), ((G // 2, 2 * N, 2 * L * C), BF16),
            ((G // 2, 2 * N, 2 * L * C), BF16), ((G // 2, 2 * N, 2 * L * C), BF16),
            ((G // 2, 2 * N, 2 * L * C), BF16), ((G // 2, 2, 2 * N), F32),
            ((G, C, N), F32), ((G, C, N), F32), ((G, 1, N), F32), ((G, 1, N), F32)]
    out_specs = [pl.BlockSpec((2, L * C, L * C), lambda p: (p, 0, 0))] \
        + [one(2 * N, 2 * L * C)] * 4 + [one(2, 2 * N)] \
        + [pl.BlockSpec((2,) + s[1:], lambda p: (p, 0, 0)) for s, _ in outs[6:]]
    res = pl.pallas_call(
        _s5_prep_kernel,
        grid=(G // 2,),
        in_specs=[two(a) for a in ins],
        out_specs=out_specs,
        out_shape=[jax.ShapeDtypeStruct(s, d) for s, d in outs],
        compiler_params=_params(("arbitrary",)),
        name="s5_prep",
    )(*ins)
    names = ('m', 'pre', 'pim', 'cre', 'cim', 'a16', 'brt', 'bit', 'arr', 'air')
    return dict(zip(names, res))


S5_GQ = LANES // SSM_GROUP


def _s5_prompt_kernel(u_ref, m_ref, pre_ref, pim_ref, cre_ref, cim_ref, a16_ref,
                      y_ref, xr_ref, xi_ref,
                      u2_ref, bre_ref, bim_ref, xpr_ref, xpi_ref, *, nb, nch):
    L, C, GQ = S5_CHUNK, SSM_GROUP, S5_GQ
    npair = GQ // 2

    def relayout_in(b, carry):
        uts = [u_ref[b, pl.ds(j, nch, stride=L), :].T for j in range(L)]
        for g in range(GQ):
            for hf in range(2):
                xt = jnp.concatenate([ut[g * C:(g + 1) * C, :]
                                      for ut in uts[hf * L // 2:(hf + 1) * L // 2]], axis=0)
                u2_ref[g, hf, pl.ds(b, nch, stride=nb), :] = xt.T
        return carry

    u2 = lambda g: jnp.concatenate([u2_ref[g, 0], u2_ref[g, 1]], axis=1)

    lax.fori_loop(0, nb, relayout_in, 0)

    for m in range(npair):
        ub2 = jnp.concatenate([u2(2 * m), u2(2 * m + 1)], axis=1).astype(BF16)
        bre_ref[m] = _dot_nt(ub2, pre_ref[m])
        bim_ref[m] = _dot_nt(ub2, pim_ref[m])

    ars = [a16_ref[m, 0:1, :] for m in range(npair)]
    ais = [a16_ref[m, 1:2, :] for m in range(npair)]

    def step(cidx, carry):
        o = pl.multiple_of(cidx * nb, nb)
        out = []
        for m in range(npair):
            xr, xi = carry[2 * m], carry[2 * m + 1]
            xpr_ref[m, pl.ds(o, nb), :] = xr
            xpi_ref[m, pl.ds(o, nb), :] = xi
            out.append(ars[m] * xr - ais[m] * xi + bre_ref[m, pl.ds(o, nb), :])
            out.append(ars[m] * xi + ais[m] * xr + bim_ref[m, pl.ds(o, nb), :])
        return tuple(out)

    z = jnp.zeros((nb, LANES), F32)
    fin = lax.fori_loop(0, nch, step, (z,) * GQ)
    for m in range(npair):
        xr_ref[m] = fin[2 * m]
        xi_ref[m] = fin[2 * m + 1]

    for m in range(npair):
        ys = _dot(xpr_ref[m], cre_ref[m]) + _dot(xpi_ref[m], cim_ref[m])
        for k in range(2):
            g = 2 * m + k
            yg = _dot(u2(g), m_ref[g]) + ys[:, k * L * C:(k + 1) * L * C]
            u2_ref[g, 0] = yg[:, :LANES]
            u2_ref[g, 1] = yg[:, LANES:]

    def relayout_out(b, carry):
        for hf in range(2):
            yts = [u2_ref[g, hf, pl.ds(b, nch, stride=nb), :].T for g in range(GQ)]
            for i in range(L // 2):
                yt = jnp.concatenate([t_[i * C:(i + 1) * C, :] for t_ in yts], axis=0)
                y_ref[b, pl.ds(hf * L // 2 + i, nch, stride=L), :] = yt.T
        return carry

    lax.fori_loop(0, nb, relayout_out, 0)


def _s5_prompt(u, tabs, nb, t):
    L, G, N, C = S5_CHUNK, SSM_GROUPS, SSM_STATE, SSM_GROUP
    nch = t // L
    nc = nch * nb
    nq = SSM_WIDTH // LANES
    npair = S5_GQ // 2
    spec = lambda blk: pl.BlockSpec(blk, lambda q: (q, 0, 0))
    y, xr, xi = pl.pallas_call(
        functools.partial(_s5_prompt_kernel, nb=nb, nch=nch),
        grid=(nq,),
        in_specs=[pl.BlockSpec((nb, t, LANES), lambda q: (0, 0, q)),
                  spec((S5_GQ, L * C, L * C)),
                  spec((npair, 2 * N, 2 * L * C)), spec((npair, 2 * N, 2 * L * C)),
                  spec((npair, 2 * N, 2 * L * C)), spec((npair, 2 * N, 2 * L * C)),
                  spec((npair, 2, 2 * N))],
        out_specs=[pl.BlockSpec((nb, t, LANES), lambda q: (0, 0, q)),
                   spec((npair, nb, 2 * N)), spec((npair, nb, 2 * N))],
        out_shape=[jax.ShapeDtypeStruct((nb, t, SSM_WIDTH), F32),
                   jax.ShapeDtypeStruct((G // 2, nb, 2 * N), F32),
                   jax.ShapeDtypeStruct((G // 2, nb, 2 * N), F32)],
        scratch_shapes=[pltpu.VMEM((S5_GQ, 2, nc, LANES), F32)]
        + [pltpu.VMEM((npair, nc, 2 * N), F32) for _ in range(4)],
        compiler_params=_params(("arbitrary",)),
        name="s5_prompt",
    )(u.reshape(nb, t, SSM_WIDTH), tabs['m'], tabs['pre'], tabs['pim'], tabs['cre'], tabs['cim'],
      tabs['a16'])
    unpair = lambda x: x.reshape(G // 2, nb, 2, N).transpose(1, 0, 2, 3).reshape(nb, G, N)
    return y.reshape(nb * t, SSM_WIDTH), unpair(xr), unpair(xi)


def _s5_step_kernel(u_ref, xr_ref, xi_ref, ar_ref, ai_ref, brt_ref, bit_ref, cr_ref, ci_ref,
                    y_ref, nr_ref, ni_ref, bd_ref):
    G, N, C = SSM_GROUPS, SSM_STATE, SSM_GROUP

    def block_diag(slot, src_ref):
        bd_ref[slot] = jnp.zeros((G * C, G * N), BF16)
        for g in range(G):
            bd_ref[slot, g * C:(g + 1) * C, g * N:(g + 1) * N] = src_ref[g].astype(BF16)
        return bd_ref[slot]

    ub = u_ref[...].astype(BF16)
    xr, xi, ar, ai = xr_ref[...], xi_ref[...], ar_ref[...], ai_ref[...]
    nr = ar * xr - ai * xi + jnp.dot(ub, block_diag(0, brt_ref), preferred_element_type=F32)
    ni = ar * xi + ai * xr + jnp.dot(ub, block_diag(1, bit_ref), preferred_element_type=F32)
    nr_ref[...] = nr
    ni_ref[...] = ni
    y_ref[...] = _dot_nt(nr, block_diag(2, cr_ref)) - _dot_nt(ni, block_diag(3, ci_ref))


def _s5_step(u, st_re, st_im, tabs, C_re, C_im):
    G, N, C = SSM_GROUPS, SSM_STATE, SSM_GROUP
    n = u.shape[0]
    ops = (u, st_re.reshape(n, G * N), st_im.reshape(n, G * N),
           tabs['arr'].reshape(1, G * N), tabs['air'].reshape(1, G * N),
           tabs['brt'], tabs['bit'], C_re, C_im)
    y, nr, ni = pl.pallas_call(
        _s5_step_kernel,
        grid=(1,),
        in_specs=[_full_spec(o.shape) for o in ops],
        out_specs=[_full_spec((n, SSM_WIDTH)), _full_spec((n, G * N)), _full_spec((n, G * N))],
        out_shape=[jax.ShapeDtypeStruct((n, SSM_WIDTH), F32),
                   jax.ShapeDtypeStruct((n, G * N), F32),
                   jax.ShapeDtypeStruct((n, G * N), F32)],
        scratch_shapes=[pltpu.VMEM((4, G * C, G * N), BF16)],
        compiler_params=_params(("arbitrary",)),
        name="s5_step",
    )(*ops)
    return y, nr.reshape(n, G, N), ni.reshape(n, G, N)


def _tail_kernel(*refs, tm, nsub, tiles_per_seq, decode, final):
    if decode:
        (x_ref, rw_ref, ys_ref, u_ref, gl_ref, pe_ref, st2_ref, st1_ref,
         dsk_ref, wglu_ref, wout_ref, ln2_ref, wfi_ref, cw_ref, cb_ref, wfo_ref,
         ln3_ref, wpg_ref, wple_ref, fin_ref, y_ref, conv_ref) = refs
    else:
        (x_ref, rw_ref, ys_ref, u_ref, gl_ref, pe_ref,
         dsk_ref, wglu_ref, wout_ref, ln2_ref, wfi_ref, cw_ref, cb_ref, wfo_ref,
         ln3_ref, wpg_ref, wple_ref, fin_ref, y_ref, conv_ref, carry_ref) = refs

    sub = tm // nsub
    cw = cw_ref[...]
    last2 = {}
    if not decode:
        @pl.when(pl.program_id(0) % tiles_per_seq == 0)
        def _():
            carry_ref[...] = jnp.zeros_like(carry_ref)

        for cols in FFN_CHUNKS:
            last2[-1, cols.start] = carry_ref[:, cols]

    def row_block(s):
        rs = slice(s * sub, (s + 1) * sub)
        x = x_ref[rs, :]
        z = _gelu(ys_ref[rs, :] + dsk_ref[...] * u_ref[rs, :])
        zz = _dot(z, wglu_ref[...])
        yield
        s5 = zz[:, :D_MODEL] * _sigmoid(zz[:, D_MODEL:])
        gl = gl_ref[rs, :]
        merged = _sigmoid(gl[:, :D_MODEL]) * rw_ref[rs, :] + _sigmoid(gl[:, D_MODEL:]) * s5
        x = x + _dot(merged, wout_ref[...])
        yield
        h2 = _rmsnorm(x, ln2_ref[...]).astype(BF16)
        rows = lax.broadcasted_iota(jnp.int32, (sub, 1), 0)

        def up(cols):
            a_c = jnp.dot(h2, wfi_ref[:, cols], preferred_element_type=F32)
            b_c = jnp.dot(h2, wfi_ref[:, slice(D_FF + cols.start, D_FF + cols.stop)],
                          preferred_element_type=F32)
            if decode:
                conv_ref[rs, cols] = a_c
            else:
                last2[s, cols.start] = a_c[sub - 2:sub, :]
            return a_c, b_c

        def down(a_c, b_c, cols):
            if decode:
                prev2, prev1 = st2_ref[rs, cols], st1_ref[rs, cols]
            else:
                before = last2[s - 1, cols.start]
                c0, c1 = before[0:1, :], before[1:2, :]
                prev1 = jnp.where(rows == 0, c1, pltpu.roll(a_c, 1, axis=0))
                prev2 = jnp.where(rows == 0, c0,
                                  jnp.where(rows == 1, c1, pltpu.roll(a_c, 2, axis=0)))
            a_conv = (cw[0:1, cols] * prev2 + cw[1:2, cols] * prev1 + cw[2:3, cols] * a_c
                      + cb_ref[:, cols])
            return _dot(_gelu(a_conv) * b_c, wfo_ref[cols, :])

        ups = {0: up(FFN_CHUNKS[0])}
        for c, cols in enumerate(FFN_CHUNKS):
            if c + 1 < len(FFN_CHUNKS):
                ups[c + 1] = up(FFN_CHUNKS[c + 1])
            yield
            x = x + down(*ups.pop(c), cols)
        yield
        pg = _sigmoid(_dot(_rmsnorm(x, ln3_ref[...]), wpg_ref[...]))
        x = x + pg * _dot(pe_ref[rs, :], wple_ref[...])
        y_ref[rs, :] = _rmsnorm(x, fin_ref[...]) if final else x
        yield

    blocks = [row_block(s) for s in range(nsub)]
    for _ in range(4 + len(FFN_CHUNKS)):
        for blk in blocks:
            next(blk)
    if not decode:
        for cols in FFN_CHUNKS:
            carry_ref[:, cols] = last2[nsub - 1, cols.start]
            conv_ref[0, :, cols] = last2[nsub - 1, cols.start]


def _tail(x2d, rw, ys, u, gl, pe, st_conv, D_skip, wglu, wout, ln2_g, wfi, conv_w, conv_b, wfo,
          ln3_g, wpg, wple, final_g, *, tm, tiles_per_seq, decode, final):
    n = x2d.shape[0]
    row = lambda w: pl.BlockSpec((tm, w), lambda i: (i, 0))
    consts = [D_skip.reshape(1, SSM_WIDTH), wglu, wout, ln2_g.reshape(1, D_MODEL), wfi,
              conv_w, conv_b.reshape(1, D_FF), wfo, ln3_g.reshape(1, D_MODEL), wpg, wple,
              final_g.reshape(1, D_MODEL)]
    acts = [x2d, rw, ys, u, gl, pe]
    in_specs = [row(a.shape[1]) for a in acts]
    scratch = []
    if decode:
        acts += [st_conv[:, 0], st_conv[:, 1]]
        in_specs += [row(D_FF), row(D_FF)]
        conv_spec = row(D_FF)
        conv_shape = (n, D_FF)
    else:
        nseq = n // (tm * tiles_per_seq)
        conv_spec = pl.BlockSpec((1, 2, D_FF), lambda i: (i // tiles_per_seq, 0, 0))
        conv_shape = (nseq, 2, D_FF)
        scratch = [pltpu.VMEM((2, D_FF), F32)]
    in_specs += [pl.BlockSpec(c.shape, lambda i, nd=c.ndim: (0,) * nd,
                              pipeline_mode=pl.Buffered(1)) for c in consts]
    y, conv = pl.pallas_call(
        functools.partial(_tail_kernel, tm=tm, nsub=1 if decode else TAIL_ROW_BLOCKS,
                          tiles_per_seq=tiles_per_seq, decode=decode,
                          final=final),
        grid=(n // tm,),
        in_specs=in_specs,
        out_specs=[row(D_MODEL), conv_spec],
        out_shape=[jax.ShapeDtypeStruct((n, D_MODEL), F32),
                   jax.ShapeDtypeStruct(conv_shape, F32)],
        scratch_shapes=scratch,
        compiler_params=_params(("arbitrary",)),
        name="tail_decode" if decode else "tail_prompt",
    )(*acts, *consts)
    if decode:
        conv = jnp.stack([st_conv[:, 1], conv], axis=1)
    return y, conv


def _layer(x, pe, states, W, *, decode, final, tm_proj, tt, tm_tail):
    nb, t, _ = x.shape
    n = nb * t
    x2d = x.reshape(n, D_MODEL)
    to_cast = [k for k in TAIL_WEIGHTS if W[k].dtype != BF16]
    (prw, u, gl), casted = _proj(x2d, W['ln1_g'], W['w_in'], tm_proj, [W[k] for k in to_cast])
    W.update(zip(to_cast, casted))
    if decode:
        st_shift, st_wkv, st_re, st_im, st_conv = states
        rw, s_new = _rwkv_step(prw, st_shift, st_wkv, W['rowp'], W['wla'], W['g2'], W['eones'],
                               W['w_rw_out'])
        ys, xr, xi = _s5_step(u, st_re, st_im, W['s5'], W['C_re'], W['C_im'])
        new_shift = prw
    else:
        st_conv = None
        rw, s_new = _rwkv_prompt(prw, W['rowp'], W['wla'], W['g2'], W['eones'], W['w_rw_out'],
                                 nb, t, tt)
        ys, xr, xi = _s5_prompt(u, W['s5'], nb, t)
        new_shift = prw.reshape(nb, t, SHIFT_WIDTH)[:, -1]
    y, new_conv = _tail(x2d, rw, ys, u, gl, pe.reshape(n, PLE_DIM), st_conv, W['D_skip'],
                        W['w_glu'], W['w_out'], W['ln2_g'], W['w_ffn_in'], W['conv_w'],
                        W['conv_b'], W['w_ffn_out'], W['ln3_g'], W['w_ple_gate'], W['w_ple'],
                        W['final_g'], tm=tm_tail, tiles_per_seq=max(t // tm_tail, 1),
                        decode=decode, final=final)
    return y.reshape(nb, t, D_MODEL), (new_shift, s_new, xr, xi, new_conv)


def kernel(x_prompt, x_sample, p_prompt, p_sample, state_shift, state_wkv, state_ssm_re, state_ssm_im, state_conv, ln1_g, w_in, mu_shift, w0, w2, a0, a2, g2, k_k, k_a, r_k, lnx_g, lnx_b, w_rw_out, A_re, A_im, log_dt, B_re, B_im, C_re, C_im, D_skip, w_glu, w_out, ln2_g, w_ffn_in, conv_w, conv_b, w_ffn_out, ln3_g, w_ple_gate, w_ple, final_g):
    depth = w_in.shape[0]
    xp, xs = x_prompt, x_sample
    pst = [[] for _ in range(5)]
    sst = [[] for _ in range(5)]
    for i in range(depth):
        wla, g2b, wo = _rwkv_weights(w2[i], a2[i], g2[i], w_rw_out[i])
        W = dict(
            ln1_g=ln1_g[i], w_in=w_in[i],
            rowp=_row_params(mu_shift[i], w0[i], a0[i], k_k[i], k_a[i], r_k[i].reshape(-1),
                             lnx_g[i], lnx_b[i]),
            wla=wla, g2=g2b, eones=_rwkv_consts(), w_rw_out=wo,
            s5=_s5_prep(A_re[i], A_im[i], log_dt[i], B_re[i], B_im[i], C_re[i], C_im[i]),
            C_re=C_re[i], C_im=C_im[i], D_skip=D_skip[i],
            w_glu=w_glu[i], w_out=w_out[i], ln2_g=ln2_g[i],
            w_ffn_in=w_ffn_in[i], conv_w=conv_w[i], conv_b=conv_b[i],
            w_ffn_out=w_ffn_out[i], ln3_g=ln3_g[i],
            w_ple_gate=w_ple_gate[i], w_ple=w_ple[i], final_g=final_g)
        final = i == depth - 1
        tp = xp.shape[1]
        xp, sp = _layer(xp, p_prompt[i], None, W, decode=False, final=final,
                        tm_proj=min(TM_PROJ, tp), tt=min(TT_RWKV, tp), tm_tail=min(TM_TAIL, tp))
        ns = xs.shape[0]
        xs, ss = _layer(xs, p_sample[i],
                        (state_shift[i], state_wkv[i], state_ssm_re[i], state_ssm_im[i],
                         state_conv[i]),
                        W, decode=True, final=final, tm_proj=ns, tt=None, tm_tail=ns)
        for j in range(5):
            pst[j].append(sp[j])
            sst[j].append(ss[j])
    return (xp, xs,
            jnp.stack(pst[0]), jnp.stack(pst[1]), jnp.stack(pst[2]), jnp.stack(pst[3]),
            jnp.stack(pst[4]),
            jnp.stack(sst[0]), jnp.stack(sst[1]), jnp.stack(sst[2]), jnp.stack(sst[3]),
            jnp.stack(sst[4]))
```

```python
import functools
import math

import jax
import jax.numpy as jnp
import numpy as np
from jax import lax
from jax.experimental import pallas as pl
from jax.experimental.pallas import tpu as pltpu

F32 = jnp.float32
BF16 = jnp.bfloat16

D_MODEL = 1024
RW_HEADS = 8
HEAD_DIM = 64
RW_WIDTH = RW_HEADS * HEAD_DIM
LORA_W = 64
LORA_A = 64
LORA_G = 128
SHIFT_WIDTH = 3 * RW_WIDTH + LORA_W + LORA_A + LORA_G
SSM_WIDTH = 512
SSM_GROUP = 16
SSM_GROUPS = SSM_WIDTH // SSM_GROUP
SSM_STATE = 64
GATE_OFF = SHIFT_WIDTH + SSM_WIDTH
IN_WIDTH = SHIFT_WIDTH + SSM_WIDTH + 2 * D_MODEL
D_FF = 2816
PLE_DIM = 256
EPS = 1e-6
GN_EPS = 64e-5

LANES = 128
WKV_CHUNK = 64
S5_CHUNK = 16
N_PAIRS = RW_HEADS // 2
VMEM_LIMIT = 60 * 1024 * 1024
TM_PROJ = 512
TT_RWKV = 256
TM_TAIL = 512
RWKV_SEQ_GROUPS = 2
TAIL_ROW_BLOCKS = 2


def _dot(a, b):
    return jnp.dot(a.astype(BF16), b.astype(BF16), preferred_element_type=F32)


def _dot_nt(a, b):
    return lax.dot_general(a.astype(BF16), b.astype(BF16), (((1,), (1,)), ((), ())),
                           preferred_element_type=F32)


def _dot_tn(a, b):
    return lax.dot_general(a.astype(BF16), b.astype(BF16), (((0,), (0,)), ((), ())),
                           preferred_element_type=F32)


def _split3(x):
    h1 = x.astype(BF16)
    r1 = x - h1.astype(F32)
    h2 = r1.astype(BF16)
    h3 = (r1 - h2.astype(F32)).astype(BF16)
    return h1, h2, h3


def _dot_exact_lhs(a_bf16, x):
    h1, h2, h3 = _split3(x)
    d = lambda h: jnp.dot(a_bf16, h, preferred_element_type=F32)
    return d(h1) + d(h2) + d(h3)


def _sigmoid(x):
    return 1.0 / (1.0 + jnp.exp(-x))


def _gelu(x):
    c = math.sqrt(2.0 / math.pi)
    return 0.5 * x * (1.0 + jnp.tanh(c * (x + 0.044715 * (x * x * x))))


def _rmsnorm(x, g):
    return x * lax.rsqrt(jnp.mean(x * x, axis=-1, keepdims=True) + EPS) * g


def _full_spec(shape):
    nd = len(shape)
    return pl.BlockSpec(shape, lambda *_: (0,) * nd)


def _params(sem):
    return pltpu.CompilerParams(dimension_semantics=sem, vmem_limit_bytes=VMEM_LIMIT)


BF16_ROWS = 16
FFN_CHUNKS = (slice(0, 768), slice(768, 1536), slice(1536, 2304), slice(2304, D_FF))
TAIL_WEIGHTS = ('w_glu', 'w_out', 'w_ffn_in', 'w_ffn_out', 'w_ple_gate', 'w_ple')


def _proj_kernel(x_ref, g_ref, w_ref, *refs, n_cast):
    cast_in, (prw_ref, u_ref, gl_ref), cast_out = refs[:n_cast], refs[n_cast:n_cast + 3], \
        refs[n_cast + 3:]
    h = _rmsnorm(x_ref[...], g_ref[...])
    prw_ref[...] = _dot(h, w_ref[:, :SHIFT_WIDTH])
    u_ref[...] = _dot(h, w_ref[:, SHIFT_WIDTH:GATE_OFF])
    gl_ref[...] = _dot(h, w_ref[:, GATE_OFF:])
    for src, dst in zip(cast_in, cast_out):
        dst[...] = src[...].astype(BF16)


def _proj(x2d, ln1_g, w_in, tm, cast_ws=()):
    n = x2d.shape[0]
    steps = n // tm
    row = lambda w: pl.BlockSpec((tm, w), lambda i: (i, 0))

    def slab(w):
        rows = w.shape[0]
        span = 1
        while rows % (steps // span) or (rows // (steps // span)) % BF16_ROWS:
            span *= 2
        return pl.BlockSpec((rows // (steps // span), w.shape[1]), lambda i: (i // span, 0))

    slabs = [slab(w) for w in cast_ws]
    res = pl.pallas_call(
        functools.partial(_proj_kernel, n_cast=len(cast_ws)),
        grid=(steps,),
        in_specs=[row(D_MODEL), _full_spec((1, D_MODEL)),
                  pl.BlockSpec((D_MODEL, IN_WIDTH), lambda i: (0, 0),
                               pipeline_mode=pl.Buffered(1))] + slabs,
        out_specs=[row(SHIFT_WIDTH), row(SSM_WIDTH), row(2 * D_MODEL)] + slabs,
        out_shape=[jax.ShapeDtypeStruct((n, SHIFT_WIDTH), F32),
                   jax.ShapeDtypeStruct((n, SSM_WIDTH), F32),
                   jax.ShapeDtypeStruct((n, 2 * D_MODEL), F32)]
        + [jax.ShapeDtypeStruct(w.shape, BF16) for w in cast_ws],
        compiler_params=_params(("arbitrary",)),
        name="proj",
    )(x2d, ln1_g.reshape(1, D_MODEL), w_in, *cast_ws)
    return res[:3], res[3:]


def _rwkv_token_math_stages(p, prev, mu, w0, a0, k_k, k_a, r_k, wla, g2, eones):
    xs = p + (prev - p) * mu
    r = xs[:, 0:RW_WIDTH]
    k = xs[:, RW_WIDTH:2 * RW_WIDTH]
    v = xs[:, 2 * RW_WIDTH:3 * RW_WIDTH]
    xwa = xs[:, 3 * RW_WIDTH:3 * RW_WIDTH + LORA_W + LORA_A]
    xg = xs[:, 3 * RW_WIDTH + LORA_W + LORA_A:]
    lane = lax.broadcasted_iota(jnp.int32, xwa.shape, 1)
    la = _dot(jnp.where(lane < LORA_W, jnp.tanh(xwa), xwa), wla)
    g = _dot(_sigmoid(xg), g2)
    yield None
    wpre = w0 + la[:, :RW_WIDTH]
    softplus = jnp.maximum(-wpre, 0.0) + jnp.log(1.0 + jnp.exp(-jnp.abs(wpre)))
    lw = -jnp.exp(-softplus - 0.5)
    a = _sigmoid(a0 + la[:, RW_WIDTH:])
    kkr = k * k_k
    ss = _dot(kkr * kkr, eones)
    kk = kkr / jnp.maximum(jnp.sqrt(ss), 1e-12)
    kp = k * (1.0 + (a - 1.0) * k_a)
    bonus = _dot(r * kp * r_k, eones) * v
    yield r, v, lw, a, g, kk, kp, bonus


def _rwkv_token_math(*args):
    *_, out = _rwkv_token_math_stages(*args)
    return out


def _rwkv_output(y, bonus, g, lnx_g, lnx_b, eones, w_o):
    emean = eones * (1.0 / HEAD_DIM)
    mean = _dot(y, emean)
    d = y - mean
    var = _dot(d * d, emean)
    yn = d * lax.rsqrt(var + GN_EPS) * lnx_g + lnx_b
    return _dot((yn + bonus) * g, w_o)


def _head_masks(shape):
    lane = lax.broadcasted_iota(jnp.int32, shape, 1)
    return lane < HEAD_DIM, lane >= HEAD_DIM


def _stack_heads(x, m_a, m_b):
    return jnp.concatenate([jnp.where(m_a, x, 0.0), jnp.where(m_b, x, 0.0)], axis=0)


def _rwkv_prompt_kernel(p_ref, mu_ref, w0_ref, a0_ref, kk_ref, ka_ref, rk_ref, lg_ref, lb_ref,
                        wla_ref, g2_ref, e_ref, ltri_ref, wo_ref,
                        out_ref, sout_ref,
                        carry_ref, sp_ref, y_ref, pend_ref, pendf_ref, gend_ref,
                        new_ref, newf_ref, gnew_ref, *, tt, nt, nh):
    s = pl.program_id(0)
    nch = tt // WKV_CHUNK
    eones = e_ref[...]
    halves = range(nh)
    pairs = range(N_PAIRS)
    rows_of = lambda ch: slice(ch * WKV_CHUNK, (ch + 1) * WKV_CHUNK)
    lanes_of = lambda j: slice(j * LANES, (j + 1) * LANES)

    @pl.when(s == 0)
    def _():
        pend_ref[...] = jnp.zeros_like(pend_ref)
        pendf_ref[...] = jnp.zeros_like(pendf_ref)
        gend_ref[...] = jnp.zeros_like(gend_ref)
        sp_ref[...] = jnp.zeros_like(sp_ref)

    @pl.when(s % nt == 0)
    def _():
        carry_ref[...] = jnp.zeros_like(carry_ref)

    @pl.when(s % nt == 1 % nt)
    def _():
        sp_ref[...] = jnp.zeros_like(sp_ref)

    rows = lax.broadcasted_iota(jnp.int32, (tt, 1), 0)
    token_math = []
    for h in halves:
        p = p_ref[h]
        prev = jnp.where(rows == 0, carry_ref[h], pltpu.roll(p, 1, axis=0))
        carry_ref[h] = p[tt - 1:tt, :]
        token_math.append(_rwkv_token_math_stages(
            p, prev, mu_ref[...], w0_ref[...], a0_ref[...], kk_ref[...], ka_ref[...],
            rk_ref[...], wla_ref[...], g2_ref[...], eones))
        next(token_math[h])

    m_a, m_b = _head_masks((WKV_CHUNK, LANES))
    ri = lax.broadcasted_iota(jnp.int32, (2 * WKV_CHUNK, 2 * WKV_CHUNK), 0)
    ci = lax.broadcasted_iota(jnp.int32, (2 * WKV_CHUNK, 2 * WKV_CHUNK), 1)
    same_head = (ri >= WKV_CHUNK) == (ci >= WKV_CHUNK)
    strict = same_head & (ri > ci)
    incl = same_head & (ri >= ci)
    blocks = [(h, ch, j) for h in halves for ch in range(nch) for j in pairs]
    blk = lambda i, h, ch, j: pend_ref[h, i, rows_of(ch), lanes_of(j)]
    stack = lambda i, h, ch, j: _stack_heads(blk(i, h, ch, j), m_a, m_b)
    vss = {b: stack(4, *b) for b in blocks}
    gms = {b: _dot_nt(jnp.concatenate([stack(0, *b), stack(1, *b)], axis=0),
                      jnp.concatenate([stack(2, *b), stack(3, *b)], axis=0)) for b in blocks}
    n_abs = {b: jnp.where(strict, gms[b][:LANES, :LANES], 0.0) for b in blocks}
    wvs = {b: _dot(jnp.where(strict, gms[b][:LANES, LANES:], 0.0), vss[b]) for b in blocks}
    a_ys = {b: jnp.concatenate([jnp.where(incl, gms[b][LANES:, :LANES], 0.0),
                                jnp.where(incl, gms[b][LANES:, LANES:], 0.0)],
                               axis=1).astype(BF16) for b in blocks}

    tm = [next(token_math[h]) for h in halves]
    cs = [_dot_exact_lhs(ltri_ref[...], tm[h][2]) for h in halves]

    eye = (ri == ci).astype(F32)
    tinvs = {b: eye + n_abs[b] for b in blocks}
    pows = n_abs
    for _ in range(5):
        pows = {b: _dot(pows[b], pows[b]).astype(BF16) for b in blocks}
        tinvs = {b: tinvs[b] + _dot(pows[b], tinvs[b]) for b in blocks}

    for h in halves:
        r, v_n, lw, a, g_n, kk, kp, bonus_n = tm[h]
        c = cs[h]
        einv = jnp.exp(-c)
        bvec = kk * a
        new_ref[h, 0] = (-kk * jnp.exp(c - lw)).astype(BF16)
        new_ref[h, 1] = (r * jnp.exp(c)).astype(BF16)
        new_ref[h, 2] = (bvec * einv).astype(BF16)
        new_ref[h, 3] = (kp * einv).astype(BF16)
        new_ref[h, 4] = v_n.astype(BF16)
        for ch in range(nch):
            c_end = c[(ch + 1) * WKV_CHUNK - 1:(ch + 1) * WKV_CHUNK, :]
            e_rest = jnp.exp(c_end - c[rows_of(ch), :])
            gnew_ref[h, ch] = jnp.exp(c_end)
            new_ref[h, 5, rows_of(ch), :] = (bvec[rows_of(ch), :] * e_rest).astype(BF16)
            new_ref[h, 6, rows_of(ch), :] = (kp[rows_of(ch), :] * e_rest).astype(BF16)
        newf_ref[h, 0] = bonus_n
        newf_ref[h, 1] = g_n

    sps = {(h, j): sp_ref[h, j] for h in halves for j in pairs}
    for ch in range(nch):
        hj = [(h, j) for h in halves for j in pairs]
        arss = {(h, j): _dot_nt(jnp.concatenate([blk(0, h, ch, j), blk(1, h, ch, j)], axis=0),
                                sps[h, j]) for h, j in hj}
        uvs = {(h, j): jnp.concatenate(
            [_dot(tinvs[h, ch, j],
                  _stack_heads(arss[h, j][:WKV_CHUNK], m_a, m_b) + wvs[h, ch, j]).astype(BF16),
             vss[h, ch, j]], axis=0) for h, j in hj}
        yss = {(h, j): _dot(a_ys[h, ch, j], uvs[h, j]) for h, j in hj}
        incs = {(h, j): _dot_tn(uvs[h, j], jnp.concatenate(
            [stack(5, h, ch, j), stack(6, h, ch, j)], axis=0)) for h, j in hj}
        for h, j in hj:
            y_ref[h, rows_of(ch), lanes_of(j)] = (yss[h, j][:WKV_CHUNK] + yss[h, j][WKV_CHUNK:]
                                                  + arss[h, j][WKV_CHUNK:])
            sps[h, j] = sps[h, j] * gend_ref[h, ch][:, lanes_of(j)] + incs[h, j]
    for h in halves:
        for j in pairs:
            sp_ref[h, j] = sps[h, j]
        out_ref[h] = _rwkv_output(y_ref[h], pendf_ref[h, 0], pendf_ref[h, 1], lg_ref[...],
                                  lb_ref[...], eones, wo_ref[...])

    pend_ref[...] = new_ref[...]
    pendf_ref[...] = newf_ref[...]
    gend_ref[...] = gnew_ref[...]

    @pl.when((s % nt == 0) & (s > 0))
    def _():
        for h in halves:
            for j in pairs:
                sp = sp_ref[h, j]
                sout_ref[h, 0, 2 * j] = sp[:HEAD_DIM, :HEAD_DIM]
                sout_ref[h, 0, 2 * j + 1] = sp[HEAD_DIM:, HEAD_DIM:]


def _rwkv_consts():
    idx = np.arange(RW_WIDTH)
    return jnp.asarray(idx[:, None] // HEAD_DIM == idx[None, :] // HEAD_DIM, dtype=BF16)


def _rwkv_weights(w2, a2, g2, w_rw_out):
    wla = jnp.zeros((LORA_W + LORA_A, 2 * RW_WIDTH), F32)
    wla = wla.at[:LORA_W, :RW_WIDTH].set(w2).at[LORA_W:, RW_WIDTH:].set(a2)
    return wla.astype(BF16), g2.astype(BF16), w_rw_out.astype(BF16)


def _row_params(mu_shift, w0, a0, k_k, k_a, r_k, lnx_g, lnx_b):
    return (mu_shift.reshape(1, SHIFT_WIDTH), w0.reshape(1, RW_WIDTH), a0.reshape(1, RW_WIDTH),
            k_k.reshape(1, RW_WIDTH), k_a.reshape(1, RW_WIDTH), r_k.reshape(1, RW_WIDTH),
            lnx_g.reshape(1, RW_WIDTH), lnx_b.reshape(1, RW_WIDTH))


def _rwkv_prompt(prw, rowp, wla, g2b, eones, wo, nb, t, tt):
    nt = t // tt
    idx = np.arange(tt)
    ltri = jnp.asarray((idx[:, None] // WKV_CHUNK == idx[None, :] // WKV_CHUNK)
                       & (idx[:, None] >= idx[None, :]), dtype=BF16)
    consts = list(rowp) + [wla, g2b, eones, ltri, wo]
    nh = RWKV_SEQ_GROUPS if nb % RWKV_SEQ_GROUPS == 0 else 1
    ntiles = (nb // nh) * nt
    nch = tt // WKV_CHUNK
    done = lambda s: jnp.maximum(s - 1, 0)
    out, s_new = pl.pallas_call(
        functools.partial(_rwkv_prompt_kernel, tt=tt, nt=nt, nh=nh),
        grid=(ntiles + 1,),
        in_specs=[pl.BlockSpec((nh, tt, SHIFT_WIDTH),
                               lambda s: (0, jnp.minimum(s, ntiles - 1), 0))]
        + [_full_spec(c.shape) for c in consts],
        out_specs=[pl.BlockSpec((nh, tt, D_MODEL), lambda s: (0, done(s), 0)),
                   pl.BlockSpec((nh, 1, RW_HEADS, HEAD_DIM, HEAD_DIM),
                                lambda s: (0, done(s) // nt, 0, 0, 0))],
        out_shape=[jax.ShapeDtypeStruct((nh, nb * t // nh, D_MODEL), F32),
                   jax.ShapeDtypeStruct((nh, nb // nh, RW_HEADS, HEAD_DIM, HEAD_DIM), F32)],
        scratch_shapes=[pltpu.VMEM((nh, 1, SHIFT_WIDTH), F32),
                        pltpu.VMEM((nh, N_PAIRS, LANES, LANES), F32),
                        pltpu.VMEM((nh, tt, RW_WIDTH), F32),
                        pltpu.VMEM((nh, 7, tt, RW_WIDTH), BF16),
                        pltpu.VMEM((nh, 2, tt, RW_WIDTH), F32),
                        pltpu.VMEM((nh, nch, 1, RW_WIDTH), F32),
                        pltpu.VMEM((nh, 7, tt, RW_WIDTH), BF16),
                        pltpu.VMEM((nh, 2, tt, RW_WIDTH), F32),
                        pltpu.VMEM((nh, nch, 1, RW_WIDTH), F32)],
        compiler_params=_params(("arbitrary",)),
        name="rwkv_prompt",
    )(prw.reshape(nh, nb * t // nh, SHIFT_WIDTH), *consts)
    return (out.reshape(nb * t, D_MODEL),
            s_new.reshape(nb, RW_HEADS, HEAD_DIM, HEAD_DIM))


def _rwkv_step_kernel(p_ref, prev_ref, s_ref, mu_ref, w0_ref, a0_ref, kk_ref, ka_ref, rk_ref,
                      lg_ref, lb_ref, wla_ref, g2_ref, e_ref, wo_ref,
                      out_ref, sout_ref,
                      qt_ref, wt_ref, rt_ref, bt_ref, kt_ref, vt_ref, yt_ref, g_ref, bonus_ref):
    hp = pl.program_id(0)

    @pl.when(hp == 0)
    def _():
        r, v, lw, a, g, kk, kp, bonus = _rwkv_token_math(
            p_ref[...], prev_ref[...], mu_ref[...], w0_ref[...], a0_ref[...], kk_ref[...],
            ka_ref[...], rk_ref[...], wla_ref[...], g2_ref[...], e_ref[...])
        qt_ref[...] = (-kk).T
        wt_ref[...] = jnp.exp(lw).T
        rt_ref[...] = r.T
        bt_ref[...] = (kk * a).T
        kt_ref[...] = kp.T
        vt_ref[...] = v.T
        g_ref[...] = g
        bonus_ref[...] = bonus

    def per_value_row(vi, carry):
        for hl in range(2):
            row0 = pl.multiple_of((2 * hp + hl) * HEAD_DIM, HEAD_DIM)
            ks = pl.ds(row0, HEAD_DIM)
            s = s_ref[hl, vi]
            sa = jnp.sum(s * qt_ref[ks, :], axis=0, keepdims=True)
            s_new = (s * wt_ref[ks, :] + sa * bt_ref[ks, :]
                     + vt_ref[pl.ds(row0 + vi, 1), :] * kt_ref[ks, :])
            yt_ref[pl.ds(row0 + vi, 1), :] = jnp.sum(s_new * rt_ref[ks, :], axis=0, keepdims=True)
            sout_ref[hl, vi] = s_new
        return carry

    lax.fori_loop(0, HEAD_DIM, per_value_row, 0, unroll=2)

    @pl.when(hp == pl.num_programs(0) - 1)
    def _():
        out_ref[...] = _rwkv_output(yt_ref[...].T, bonus_ref[...], g_ref[...], lg_ref[...],
                                    lb_ref[...], e_ref[...], wo_ref[...])


def _rwkv_step(prw, prev, state, rowp, wla, g2b, eones, wo):
    n = prw.shape[0]
    consts = list(rowp) + [wla, g2b, eones, wo]
    state_t = jnp.transpose(state, (1, 2, 3, 0))
    sspec = pl.BlockSpec((2, HEAD_DIM, HEAD_DIM, n), lambda hp: (hp, 0, 0, 0))
    out, s_new_t = pl.pallas_call(
        _rwkv_step_kernel,
        grid=(N_PAIRS,),
        in_specs=[_full_spec((n, SHIFT_WIDTH)), _full_spec((n, SHIFT_WIDTH)), sspec]
        + [_full_spec(c.shape) for c in consts],
        out_specs=[_full_spec((n, D_MODEL)), sspec],
        out_shape=[jax.ShapeDtypeStruct((n, D_MODEL), F32),
                   jax.ShapeDtypeStruct((RW_HEADS, HEAD_DIM, HEAD_DIM, n), F32)],
        scratch_shapes=[pltpu.VMEM((RW_WIDTH, n), F32) for _ in range(7)]
        + [pltpu.VMEM((n, RW_WIDTH), F32) for _ in range(2)],
        compiler_params=_params(("arbitrary",)),
        name="rwkv_step",
    )(prw, prev, state_t, *consts)
    return out, jnp.transpose(s_new_t, (3, 0, 1, 2))


def _zoh(lr, li, dt):
    mag = jnp.exp(lr * dt)
    ar, ai = mag * jnp.cos(li * dt), mag * jnp.sin(li * dt)
    den = lr * lr + li * li
    fr = ((ar - 1.0) * lr + ai * li) / den
    fi = (ai * lr - (ar - 1.0) * li) / den
    return ar, ai, fr, fi


def _dot_split(a, b):
    a1 = a.astype(BF16)
    a2 = (a - a1.astype(F32)).astype(BF16)
    b1 = b.astype(BF16)
    b2 = (b - b1.astype(F32)).astype(BF16)
    d = lambda x, y: jnp.dot(x, y, preferred_element_type=F32)
    return d(a1, b1) + (d(a1, b2) + d(a2, b1))


def _s5_prep_kernel(lrc_ref, lic_ref, lrr_ref, lir_ref, ldt_ref, bre_ref, bim_ref, cr_ref, ci_ref,
                    m_ref, pre_ref, pim_ref, cre_ref, cim_ref, a16_ref,
                    brt_ref, bit_ref, arr_ref, air_ref):
    L, C, N = S5_CHUNK, SSM_GROUP, SSM_STATE
    lag = lax.broadcasted_iota(jnp.int32, (1, LANES), 1).astype(F32)
    sel_row = lax.broadcasted_iota(jnp.int32, (LANES, L * C), 0)
    sel_lag = lax.shift_right_logical(lax.broadcasted_iota(jnp.int32, (LANES, L * C), 1),
                                      C.bit_length() - 1)
    pick = lambda cond: jnp.where(cond, 1.0, 0.0).astype(BF16)
    r_same, r_next, r_rev = (pick(sel_row == sel_lag), pick(sel_row == sel_lag + 1),
                             pick(sel_row == (L - 1) - sel_lag))

    def select(e, r, dims=(((1,), (0,)), ((), ()))):
        d = lambda h: lax.dot_general(h, r, dims, preferred_element_type=F32)
        h1, h2, h3 = _split3(e)
        return d(h1) + d(h2) + d(h3)

    spread = select
    on_rows = (((0,), (0,)), ((), ()))
    r_tile = pick(lax.broadcasted_iota(jnp.int32, (C, L * C), 0)
                  == (lax.broadcasted_iota(jnp.int32, (C, L * C), 1) & (C - 1)))
    ident = pick(lax.broadcasted_iota(jnp.int32, (N, N), 0)
                 == lax.broadcasted_iota(jnp.int32, (N, N), 1))

    lane2 = lax.broadcasted_iota(jnp.int32, (C, L * C), 1)
    zero = jnp.zeros((N, L * C), F32)
    a16r, a16i = [], []
    def group(k):
        dt = jnp.exp(ldt_ref[k])
        lrc, lic = lrc_ref[k], lic_ref[k]
        mag = jnp.exp(lrc * dt * lag)
        er, ei = mag * jnp.cos(lic * dt * lag), mag * jnp.sin(lic * dt * lag)
        e0r, e0i = spread(er, r_same), spread(ei, r_same)
        e1r, e1i = spread(er, r_next), spread(ei, r_next)
        e2r, e2i = spread(er, r_rev), spread(ei, r_rev)
        yield
        _, _, frc, fic = _zoh(lrc, lic, dt)
        brl, bil = select(bre_ref[k], r_tile), select(bim_ref[k], r_tile)
        crl = select(cr_ref[k], r_tile, on_rows)
        cil = select(ci_ref[k], r_tile, on_rows)
        yield
        bbr = frc * brl - fic * bil
        bbi = frc * bil + fic * brl
        halves = lambda x: jnp.concatenate([x, zero] if k == 0 else [zero, x], axis=1)
        rows = slice(k * N, (k + 1) * N)
        pre_ref[0, rows, :] = halves(e2r * bbr - e2i * bbi).astype(BF16)
        pim_ref[0, rows, :] = halves(e2r * bbi + e2i * bbr).astype(BF16)
        cre_ref[0, rows, :] = halves(crl * e1r - cil * e1i).astype(BF16)
        cim_ref[0, rows, :] = halves(-(crl * e1i + cil * e1r)).astype(BF16)

        lrr, lir = lrr_ref[k], lir_ref[k]
        arr, air, frr, fir = _zoh(lrr, lir, dt)
        bret, bimt = select(bre_ref[k], ident, on_rows), select(bim_ref[k], ident, on_rows)
        yield
        brt = frr * bret - fir * bimt
        bit = frr * bimt + fir * bret
        brt_ref[k], bit_ref[k], arr_ref[k], air_ref[k] = brt, bit, arr, air
        m16 = jnp.exp(lrr * dt * L)
        a16r.append(m16 * jnp.cos(lir * dt * L))
        a16i.append(m16 * jnp.sin(lir * dt * L))
        klt = (_dot_split(brt, crl * e0r - cil * e0i)
               - _dot_split(bit, crl * e0i + cil * e0r))
        yield
        for j in range(L):
            blk = klt if j == 0 else jnp.where(lane2 >= j * C, pltpu.roll(klt, j * C, axis=1), 0.0)
            m_ref[k, j * C:(j + 1) * C, :] = blk.astype(BF16)

    groups = [group(0), group(1)]
    for _ in range(5):
        for grp in groups:
            next(grp, None)
    a16_ref[0, 0:1, :] = jnp.concatenate(a16r, axis=1)
    a16_ref[0, 1:2, :] = jnp.concatenate(a16i, axis=1)


def _s5_prep(A_re, A_im, log_dt, B_re, B_im, C_re, C_im):
    L, G, N, C = S5_CHUNK, SSM_GROUPS, SSM_STATE, SSM_GROUP
    ins = [A_re[:, :, None], A_im[:, :, None], A_re[:, None, :], A_im[:, None, :],
           log_dt[:, None, None], B_re, B_im, C_re, C_im]
    two = lambda a: pl.BlockSpec((2,) + a.shape[1:], lambda p: (p, 0, 0))
    one = lambda r, c: pl.BlockSpec((1, r, c), lambda p: (p, 0, 0))
    outs = [((G, L * C, L * C), BF16), ((G // 2, 2 * N, 2 * L * C), BF16),
            ((G // 2, 2 * N, 2 * L * C), BF16), ((G // 2, 2 * N, 2 * L * C), BF16),
            ((G // 2, 2 * N, 2 * L * C), BF16), ((G // 2, 2, 2 * N), F32),
            ((G, C, N), F32), ((G, C, N), F32), ((G, 1, N), F32), ((G, 1, N), F32)]
    out_specs = [pl.BlockSpec((2, L * C, L * C), lambda p: (p, 0, 0))] \
        + [one(2 * N, 2 * L * C)] * 4 + [one(2, 2 * N)] \
        + [pl.BlockSpec((2,) + s[1:], lambda p: (p, 0, 0)) for s, _ in outs[6:]]
    res = pl.pallas_call(
        _s5_prep_kernel,
        grid=(G // 2,),
        in_specs=[two(a) for a in ins],
        out_specs=out_specs,
        out_shape=[jax.ShapeDtypeStruct(s, d) for s, d in outs],
        compiler_params=_params(("arbitrary",)),
        name="s5_prep",
    )(*ins)
    names = ('m', 'pre', 'pim', 'cre', 'cim', 'a16', 'brt', 'bit', 'arr', 'air')
    return dict(zip(names, res))


S5_GQ = LANES // SSM_GROUP


def _s5_prompt_kernel(u_ref, m_ref, pre_ref, pim_ref, cre_ref, cim_ref, a16_ref,
                      y_ref, xr_ref, xi_ref,
                      u2_ref, bre_ref, bim_ref, xpr_ref, xpi_ref, *, nb, nch):
    L, C, GQ = S5_CHUNK, SSM_GROUP, S5_GQ
    npair = GQ // 2

    def relayout_in(b, carry):
        uts = [u_ref[b, pl.ds(j, nch, stride=L), :].T for j in range(L)]
        for g in range(GQ):
            for hf in range(2):
                xt = jnp.concatenate([ut[g * C:(g + 1) * C, :]
                                      for ut in uts[hf * L // 2:(hf + 1) * L // 2]], axis=0)
                u2_ref[g, hf, pl.ds(b, nch, stride=nb), :] = xt.T
        return carry

    u2 = lambda g: jnp.concatenate([u2_ref[g, 0], u2_ref[g, 1]], axis=1)

    lax.fori_loop(0, nb, relayout_in, 0)

    for m in range(npair):
        ub2 = jnp.concatenate([u2(2 * m), u2(2 * m + 1)], axis=1).astype(BF16)
        bre_ref[m] = _dot_nt(ub2, pre_ref[m])
        bim_ref[m] = _dot_nt(ub2, pim_ref[m])

    ars = [a16_ref[m, 0:1, :] for m in range(npair)]
    ais = [a16_ref[m, 1:2, :] for m in range(npair)]

    def step(cidx, carry):
        o = pl.multiple_of(cidx * nb, nb)
        out = []
        for m in range(npair):
            xr, xi = carry[2 * m], carry[2 * m + 1]
            xpr_ref[m, pl.ds(o, nb), :] = xr
            xpi_ref[m, pl.ds(o, nb), :] = xi
            out.append(ars[m] * xr - ais[m] * xi + bre_ref[m, pl.ds(o, nb), :])
            out.append(ars[m] * xi + ais[m] * xr + bim_ref[m, pl.ds(o, nb), :])
        return tuple(out)

    z = jnp.zeros((nb, LANES), F32)
    fin = lax.fori_loop(0, nch, step, (z,) * GQ)
    for m in range(npair):
        xr_ref[m] = fin[2 * m]
        xi_ref[m] = fin[2 * m + 1]

    for m in range(npair):
        ys = _dot(xpr_ref[m], cre_ref[m]) + _dot(xpi_ref[m], cim_ref[m])
        for k in range(2):
            g = 2 * m + k
            yg = _dot(u2(g), m_ref[g]) + ys[:, k * L * C:(k + 1) * L * C]
            u2_ref[g, 0] = yg[:, :LANES]
            u2_ref[g, 1] = yg[:, LANES:]

    def relayout_out(b, carry):
        for hf in range(2):
            yts = [u2_ref[g, hf, pl.ds(b, nch, stride=nb), :].T for g in range(GQ)]
            for i in range(L // 2):
                yt = jnp.concatenate([t_[i * C:(i + 1) * C, :] for t_ in yts], axis=0)
                y_ref[b, pl.ds(hf * L // 2 + i, nch, stride=L), :] = yt.T
        return carry

    lax.fori_loop(0, nb, relayout_out, 0)


def _s5_prompt(u, tabs, nb, t):
    L, G, N, C = S5_CHUNK, SSM_GROUPS, SSM_STATE, SSM_GROUP
    nch = t // L
    nc = nch * nb
    nq = SSM_WIDTH // LANES
    npair = S5_GQ // 2
    spec = lambda blk: pl.BlockSpec(blk, lambda q: (q, 0, 0))
    y, xr, xi = pl.pallas_call(
        functools.partial(_s5_prompt_kernel, nb=nb, nch=nch),
        grid=(nq,),
        in_specs=[pl.BlockSpec((nb, t, LANES), lambda q: (0, 0, q)),
                  spec((S5_GQ, L * C, L * C)),
                  spec((npair, 2 * N, 2 * L * C)), spec((npair, 2 * N, 2 * L * C)),
                  spec((npair, 2 * N, 2 * L * C)), spec((npair, 2 * N, 2 * L * C)),
                  spec((npair, 2, 2 * N))],
        out_specs=[pl.BlockSpec((nb, t, LANES), lambda q: (0, 0, q)),
                   spec((npair, nb, 2 * N)), spec((npair, nb, 2 * N))],
        out_shape=[jax.ShapeDtypeStruct((nb, t, SSM_WIDTH), F32),
                   jax.ShapeDtypeStruct((G // 2, nb, 2 * N), F32),
                   jax.ShapeDtypeStruct((G // 2, nb, 2 * N), F32)],
        scratch_shapes=[pltpu.VMEM((S5_GQ, 2, nc, LANES), F32)]
        + [pltpu.VMEM((npair, nc, 2 * N), F32) for _ in range(4)],
        compiler_params=_params(("arbitrary",)),
        name="s5_prompt",
    )(u.reshape(nb, t, SSM_WIDTH), tabs['m'], tabs['pre'], tabs['pim'], tabs['cre'], tabs['cim'],
      tabs['a16'])
    unpair = lambda x: x.reshape(G // 2, nb, 2, N).transpose(1, 0, 2, 3).reshape(nb, G, N)
    return y.reshape(nb * t, SSM_WIDTH), unpair(xr), unpair(xi)


def _s5_step_kernel(u_ref, xr_ref, xi_ref, ar_ref, ai_ref, brt_ref, bit_ref, cr_ref, ci_ref,
                    y_ref, nr_ref, ni_ref, bd_ref):
    G, N, C = SSM_GROUPS, SSM_STATE, SSM_GROUP

    def block_diag(slot, src_ref):
        bd_ref[slot] = jnp.zeros((G * C, G * N), BF16)
        for g in range(G):
            bd_ref[slot, g * C:(g + 1) * C, g * N:(g + 1) * N] = src_ref[g].astype(BF16)
        return bd_ref[slot]

    ub = u_ref[...].astype(BF16)
    xr, xi, ar, ai = xr_ref[...], xi_ref[...], ar_ref[...], ai_ref[...]
    nr = ar * xr - ai * xi + jnp.dot(ub, block_diag(0, brt_ref), preferred_element_type=F32)
    ni = ar * xi + ai * xr + jnp.dot(ub, block_diag(1, bit_ref), preferred_element_type=F32)
    nr_ref[...] = nr
    ni_ref[...] = ni
    y_ref[...] = _dot_nt(nr, block_diag(2, cr_ref)) - _dot_nt(ni, block_diag(3, ci_ref))


def _s5_step(u, st_re, st_im, tabs, C_re, C_im):
    G, N, C = SSM_GROUPS, SSM_STATE, SSM_GROUP
    n = u.shape[0]
    ops = (u, st_re.reshape(n, G * N), st_im.reshape(n, G * N),
           tabs['arr'].reshape(1, G * N), tabs['air'].reshape(1, G * N),
           tabs['brt'], tabs['bit'], C_re, C_im)
    y, nr, ni = pl.pallas_call(
        _s5_step_kernel,
        grid=(1,),
        in_specs=[_full_spec(o.shape) for o in ops],
        out_specs=[_full_spec((n, SSM_WIDTH)), _full_spec((n, G * N)), _full_spec((n, G * N))],
        out_shape=[jax.ShapeDtypeStruct((n, SSM_WIDTH), F32),
                   jax.ShapeDtypeStruct((n, G * N), F32),
                   jax.ShapeDtypeStruct((n, G * N), F32)],
        scratch_shapes=[pltpu.VMEM((4, G * C, G * N), BF16)],
        compiler_params=_params(("arbitrary",)),
        name="s5_step",
    )(*ops)
    return y, nr.reshape(n, G, N), ni.reshape(n, G, N)


def _tail_kernel(*refs, tm, nsub, tiles_per_seq, decode, final):
    if decode:
        (x_ref, rw_ref, ys_ref, u_ref, gl_ref, pe_ref, st2_ref, st1_ref,
         dsk_ref, wglu_ref, wout_ref, ln2_ref, wfi_ref, cw_ref, cb_ref, wfo_ref,
         ln3_ref, wpg_ref, wple_ref, fin_ref, y_ref, conv_ref) = refs
    else:
        (x_ref, rw_ref, ys_ref, u_ref, gl_ref, pe_ref,
         dsk_ref, wglu_ref, wout_ref, ln2_ref, wfi_ref, cw_ref, cb_ref, wfo_ref,
         ln3_ref, wpg_ref, wple_ref, fin_ref, y_ref, conv_ref, carry_ref) = refs

    sub = tm // nsub
    cw = cw_ref[...]
    last2 = {}
    if not decode:
        @pl.when(pl.program_id(0) % tiles_per_seq == 0)
        def _():
            carry_ref[...] = jnp.zeros_like(carry_ref)

        for cols in FFN_CHUNKS:
            last2[-1, cols.start] = carry_ref[:, cols]

    def row_block(s):
        rs = slice(s * sub, (s + 1) * sub)
        x = x_ref[rs, :]
        z = _gelu(ys_ref[rs, :] + dsk_ref[...] * u_ref[rs, :])
        zz = _dot(z, wglu_ref[...])
        yield
        s5 = zz[:, :D_MODEL] * _sigmoid(zz[:, D_MODEL:])
        gl = gl_ref[rs, :]
        merged = _sigmoid(gl[:, :D_MODEL]) * rw_ref[rs, :] + _sigmoid(gl[:, D_MODEL:]) * s5
        x = x + _dot(merged, wout_ref[...])
        yield
        h2 = _rmsnorm(x, ln2_ref[...]).astype(BF16)
        rows = lax.broadcasted_iota(jnp.int32, (sub, 1), 0)

        def up(cols):
            a_c = jnp.dot(h2, wfi_ref[:, cols], preferred_element_type=F32)
            b_c = jnp.dot(h2, wfi_ref[:, slice(D_FF + cols.start, D_FF + cols.stop)],
                          preferred_element_type=F32)
            if decode:
                conv_ref[rs, cols] = a_c
            else:
                last2[s, cols.start] = a_c[sub - 2:sub, :]
            return a_c, b_c

        def down(a_c, b_c, cols):
            if decode:
                prev2, prev1 = st2_ref[rs, cols], st1_ref[rs, cols]
            else:
                before = last2[s - 1, cols.start]
                c0, c1 = before[0:1, :], before[1:2, :]
                prev1 = jnp.where(rows == 0, c1, pltpu.roll(a_c, 1, axis=0))
                prev2 = jnp.where(rows == 0, c0,
                                  jnp.where(rows == 1, c1, pltpu.roll(a_c, 2, axis=0)))
            a_conv = (cw[0:1, cols] * prev2 + cw[1:2, cols] * prev1 + cw[2:3, cols] * a_c
                      + cb_ref[:, cols])
            return _dot(_gelu(a_conv) * b_c, wfo_ref[cols, :])

        ups = {0: up(FFN_CHUNKS[0])}
        for c, cols in enumerate(FFN_CHUNKS):
            if c + 1 < len(FFN_CHUNKS):
                ups[c + 1] = up(FFN_CHUNKS[c + 1])
            yield
            x = x + down(*ups.pop(c), cols)
        yield
        pg = _sigmoid(_dot(_rmsnorm(x, ln3_ref[...]), wpg_ref[...]))
        x = x + pg * _dot(pe_ref[rs, :], wple_ref[...])
        y_ref[rs, :] = _rmsnorm(x, fin_ref[...]) if final else x
        yield

    blocks = [row_block(s) for s in range(nsub)]
    for _ in range(4 + len(FFN_CHUNKS)):
        for blk in blocks:
            next(blk)
    if not decode:
        for cols in FFN_CHUNKS:
            carry_ref[:, cols] = last2[nsub - 1, cols.start]
            conv_ref[0, :, cols] = last2[nsub - 1, cols.start]


def _tail(x2d, rw, ys, u, gl, pe, st_conv, D_skip, wglu, wout, ln2_g, wfi, conv_w, conv_b, wfo,
          ln3_g, wpg, wple, final_g, *, tm, tiles_per_seq, decode, final):
    n = x2d.shape[0]
    row = lambda w: pl.BlockSpec((tm, w), lambda i: (i, 0))
    consts = [D_skip.reshape(1, SSM_WIDTH), wglu, wout, ln2_g.reshape(1, D_MODEL), wfi,
              conv_w, conv_b.reshape(1, D_FF), wfo, ln3_g.reshape(1, D_MODEL), wpg, wple,
              final_g.reshape(1, D_MODEL)]
    acts = [x2d, rw, ys, u, gl, pe]
    in_specs = [row(a.shape[1]) for a in acts]
    scratch = []
    if decode:
        acts += [st_conv[:, 0], st_conv[:, 1]]
        in_specs += [row(D_FF), row(D_FF)]
        conv_spec = row(D_FF)
        conv_shape = (n, D_FF)
    else:
        nseq = n // (tm * tiles_per_seq)
        conv_spec = pl.BlockSpec((1, 2, D_FF), lambda i: (i // tiles_per_seq, 0, 0))
        conv_shape = (nseq, 2, D_FF)
        scratch = [pltpu.VMEM((2, D_FF), F32)]
    in_specs += [pl.BlockSpec(c.shape, lambda i, nd=c.ndim: (0,) * nd,
                              pipeline_mode=pl.Buffered(1)) for c in consts]
    y, conv = pl.pallas_call(
        functools.partial(_tail_kernel, tm=tm, nsub=1 if decode else TAIL_ROW_BLOCKS,
                          tiles_per_seq=tiles_per_seq, decode=decode,
                          final=final),
        grid=(n // tm,),
        in_specs=in_specs,
        out_specs=[row(D_MODEL), conv_spec],
        out_shape=[jax.ShapeDtypeStruct((n, D_MODEL), F32),
                   jax.ShapeDtypeStruct(conv_shape, F32)],
        scratch_shapes=scratch,
        compiler_params=_params(("arbitrary",)),
        name="tail_decode" if decode else "tail_prompt",
    )(*acts, *consts)
    if decode:
        conv = jnp.stack([st_conv[:, 1], conv], axis=1)
    return y, conv


def _layer(x, pe, states, W, *, decode, final, tm_proj, tt, tm_tail):
    nb, t, _ = x.shape
    n = nb * t
    x2d = x.reshape(n, D_MODEL)
    to_cast = [k for k in TAIL_WEIGHTS if W[k].dtype != BF16]
    (prw, u, gl), casted = _proj(x2d, W['ln1_g'], W['w_in'], tm_proj, [W[k] for k in to_cast])
    W.update(zip(to_cast, casted))
    if decode:
        st_shift, st_wkv, st_re, st_im, st_conv = states
        rw, s_new = _rwkv_step(prw, st_shift, st_wkv, W['rowp'], W['wla'], W['g2'], W['eones'],
                               W['w_rw_out'])
        ys, xr, xi = _s5_step(u, st_re, st_im, W['s5'], W['C_re'], W['C_im'])
        new_shift = prw
    else:
        st_conv = None
        rw, s_new = _rwkv_prompt(prw, W['rowp'], W['wla'], W['g2'], W['eones'], W['w_rw_out'],
                                 nb, t, tt)
        ys, xr, xi = _s5_prompt(u, W['s5'], nb, t)
        new_shift = prw.reshape(nb, t, SHIFT_WIDTH)[:, -1]
    y, new_conv = _tail(x2d, rw, ys, u, gl, pe.reshape(n, PLE_DIM), st_conv, W['D_skip'],
                        W['w_glu'], W['w_out'], W['ln2_g'], W['w_ffn_in'], W['conv_w'],
                        W['conv_b'], W['w_ffn_out'], W['ln3_g'], W['w_ple_gate'], W['w_ple'],
                        W['final_g'], tm=tm_tail, tiles_per_seq=max(t // tm_tail, 1),
                        decode=decode, final=final)
    return y.reshape(nb, t, D_MODEL), (new_shift, s_new, xr, xi, new_conv)


def kernel(x_prompt, x_sample, p_prompt, p_sample, state_shift, state_wkv, state_ssm_re, state_ssm_im, state_conv, ln1_g, w_in, mu_shift, w0, w2, a0, a2, g2, k_k, k_a, r_k, lnx_g, lnx_b, w_rw_out, A_re, A_im, log_dt, B_re, B_im, C_re, C_im, D_skip, w_glu, w_out, ln2_g, w_ffn_in, conv_w, conv_b, w_ffn_out, ln3_g, w_ple_gate, w_ple, final_g):
    depth = w_in.shape[0]
    xp, xs = x_prompt, x_sample
    pst = [[] for _ in range(5)]
    sst = [[] for _ in range(5)]
    for i in range(depth):
        wla, g2b, wo = _rwkv_weights(w2[i], a2[i], g2[i], w_rw_out[i])
        W = dict(
            ln1_g=ln1_g[i], w_in=w_in[i],
            rowp=_row_params(mu_shift[i], w0[i], a0[i], k_k[i], k_a[i], r_k[i].reshape(-1),
                             lnx_g[i], lnx_b[i]),
            wla=wla, g2=g2b, eones=_rwkv_consts(), w_rw_out=wo,
            s5=_s5_prep(A_re[i], A_im[i], log_dt[i], B_re[i], B_im[i], C_re[i], C_im[i]),
            C_re=C_re[i], C_im=C_im[i], D_skip=D_skip[i],
            w_glu=w_glu[i], w_out=w_out[i], ln2_g=ln2_g[i],
            w_ffn_in=w_ffn_in[i], conv_w=conv_w[i], conv_b=conv_b[i],
            w_ffn_out=w_ffn_out[i], ln3_g=ln3_g[i],
            w_ple_gate=w_ple_gate[i], w_ple=w_ple[i], final_g=final_g)
        final = i == depth - 1
        tp = xp.shape[1]
        xp, sp = _layer(xp, p_prompt[i], None, W, decode=False, final=final,
                        tm_proj=min(TM_PROJ, tp), tt=min(TT_RWKV, tp), tm_tail=min(TM_TAIL, tp))
        ns = xs.shape[0]
        xs, ss = _layer(xs, p_sample[i],
                        (state_shift[i], state_wkv[i], state_ssm_re[i], state_ssm_im[i],
                         state_conv[i]),
                        W, decode=True, final=final, tm_proj=ns, tt=None, tm_tail=ns)
        for j in range(5):
            pst[j].append(sp[j])
            sst[j].append(ss[j])
    return (xp, xs,
            jnp.stack(pst[0]), jnp.stack(pst[1]), jnp.stack(pst[2]), jnp.stack(pst[3]),
            jnp.stack(pst[4]),
            jnp.stack(sst[0]), jnp.stack(sst[1]), jnp.stack(sst[2]), jnp.stack(sst[3]),
            jnp.stack(sst[4]))
```
